```python
import jax
import jax.numpy as jnp
from jax import lax
import numpy as np

D_MODEL = 2048
BATCH = 4
SEQ = 4096
DEPTH = 4

GRID_W = 64
CTX_LEN = 256
MIX_W = D_MODEL
FOURIER_W = D_MODEL // 4
FOURIER_GROUPS = 4
HGRN_DIM = 128
HGRN_W = D_MODEL // 4
HGRN_HEADS = HGRN_W // HGRN_DIM
CHUNK = 64
MLA_NOPE = 128
MLA_ROPE = 64
MLA_V = 128
MLA_W = MIX_W - FOURIER_W - HGRN_W
MLA_HEADS = MLA_W // MLA_V
Q_LORA = D_MODEL // 4
KV_LORA = D_MODEL // 8
MLA_SCALE = (MLA_NOPE + MLA_ROPE) ** -0.5
Q_BLOCK = 128
ROPE_BASE = 10000.0
IN_SPLITS = (FOURIER_W, HGRN_W, HGRN_W, HGRN_W, HGRN_W, HGRN_W, Q_LORA, KV_LORA, MLA_ROPE)
IN_PROJ_W = FOURIER_W + 5 * HGRN_W + Q_LORA + KV_LORA + MLA_ROPE
N_EXPERTS = 32
TOP_K = 4
EXPERT_FF = 3 * D_MODEL // 8
SWIGLU_ALPHA = 1.702
SWIGLU_LIMIT = 7.0
MOE_BLOCK = 128
N_MOD = 6
DEEPNORM_ALPHA = (2 * DEPTH) ** 0.25
DEEPNORM_BETA = (8 * DEPTH) ** -0.25
EPS = 1e-6

kernel_name = 'hybrid_fourier_hgrn2_mla_moe_dit'


def layer_norm(x):
    xf = x.astype(jnp.float32)
    mu = jnp.mean(xf, axis=-1, keepdims=True)
    var = jnp.mean(jnp.square(xf - mu), axis=-1, keepdims=True)
    return ((xf - mu) * lax.rsqrt(var + EPS)).astype(x.dtype)


def rms_norm(x, g):
    xf = x.astype(jnp.float32)
    y = xf * lax.rsqrt(jnp.mean(jnp.square(xf), axis=-1, keepdims=True) + EPS)
    return (y * g).astype(x.dtype)


def modulate(x, shift, scale):
    return layer_norm(x) * (1 + scale) + shift


def post_norm(x, update, g, b):
    return layer_norm(DEEPNORM_ALPHA * x + update) * g + b


def axial_rope(n_lat):
    rows = n_lat // GRID_W
    t = jnp.arange(rows * GRID_W)
    row, col = t // GRID_W, t % GRID_W
    n_freq = MLA_ROPE // 4
    inv = ROPE_BASE ** (-jnp.arange(n_freq, dtype=jnp.float32) / n_freq)
    ang = jnp.concatenate([row[:, None] * inv, col[:, None] * inv], axis=-1)
    return jnp.cos(ang), jnp.sin(ang)


def apply_rope(x, cos, sin):
    xp = x.reshape(*x.shape[:-1], MLA_ROPE // 2, 2)
    a, b = xp[..., 0], xp[..., 1]
    out = jnp.stack([a * cos - b * sin, a * sin + b * cos], axis=-1)
    return out.reshape(x.shape).astype(x.dtype)


def fourier_mix(u):
    B, T, _ = u.shape
    ug = u.astype(jnp.float32).reshape(B, T, FOURIER_GROUPS, FOURIER_W // FOURIER_GROUPS)
    y = jnp.fft.fft2(ug, axes=(1, 3), norm='ortho').real
    return y.reshape(B, T, FOURIER_W).astype(u.dtype)


def mla_queries(cq, q_norm_g, w_uq, rope):
    B, T, _ = cq.shape
    q = (rms_norm(cq, q_norm_g) @ w_uq).reshape(B, T, MLA_HEADS, MLA_NOPE + MLA_ROPE)
    q_nope, q_rope = q[..., :MLA_NOPE], q[..., MLA_NOPE:]
    if rope is not None:
        q_rope = apply_rope(q_rope, rope[0][:, None, :], rope[1][:, None, :])
    return q_nope, q_rope


def mla_keys(ckv, k_rope, kv_norm_g, w_ukv, rope):
    B, T, _ = ckv.shape
    kv = (rms_norm(ckv, kv_norm_g) @ w_ukv).reshape(B, T, MLA_HEADS, MLA_NOPE + MLA_V)
    k_nope, v = kv[..., :MLA_NOPE], kv[..., MLA_NOPE:]
    if rope is not None:
        k_rope = apply_rope(k_rope, rope[0], rope[1])
    return k_nope, k_rope, v


def block_attention(q_nope, q_rope, k_nope, k_rope, v):
    B, T, H, _ = q_nope.shape
    nb = T // Q_BLOCK

    def blocks(a):
        return a.reshape(B, nb, Q_BLOCK, H, a.shape[-1]).swapaxes(0, 1)

    def attend(qs):
        qn, qr = qs
        s = jnp.einsum('bqhd,bkhd->bhqk', qn, k_nope) + jnp.einsum('bqhr,bkr->bhqk', qr, k_rope)
        p = jax.nn.softmax(s.astype(jnp.float32) * MLA_SCALE, axis=-1)
        return jnp.einsum('bhqk,bkhv->bqhv', p.astype(v.dtype), v)

    o = lax.map(attend, (blocks(q_nope), blocks(q_rope)))
    return o.swapaxes(0, 1).reshape(B, T, H * MLA_V)


def hgrn_lower_bounds(logits):
    cs = jnp.cumsum(jax.nn.softmax(logits.astype(jnp.float32), axis=0), axis=0)
    return cs - cs[:1]


def hgrn_inputs(q, i, z, lb, reverse):
    B, T, _ = q.shape
    log_f = jnp.logaddexp(jnp.log(lb), jnp.log1p(-lb) + jax.nn.log_sigmoid(z.astype(jnp.float32)))
    k = -jnp.expm1(log_f)
    out = tuple(a.astype(jnp.float32).reshape(B, T, HGRN_HEADS, HGRN_DIM) for a in (q, k, i, log_f))
    if reverse:
        out = tuple(jnp.flip(a, axis=1) for a in out)
    return out


def chunk_gla(q, k, v, log_f, s0):
    B, T, H, _ = q.shape
    n = T // CHUNK

    def chunks(a):
        return a.reshape(B, n, CHUNK, H, a.shape[-1]).transpose(1, 0, 3, 2, 4)

    a = jnp.cumsum(chunks(log_f), axis=3)
    incl = jnp.tril(jnp.ones((CHUNK, CHUNK), bool))

    def step(s, blk):
        qb, kb, vb, ab = blk
        inter = jnp.einsum('bhtk,bhkv->bhtv', qb * jnp.exp(ab), s)
        rel = ab[:, :, :, None, :] - ab[:, :, None, :, :]
        decay = jnp.exp(jnp.where(incl[:, :, None], rel, -jnp.inf))
        scores = jnp.einsum('bhtk,bhsk,bhtsk->bhts', qb, kb, decay)
        intra = jnp.einsum('bhts,bhsv->bhtv', scores, vb)
        a_end = ab[:, :, -1]
        s_new = jnp.exp(a_end)[..., None] * s + jnp.einsum('bhsk,bhsv->bhkv', kb * jnp.exp(a_end[:, :, None] - ab), vb)
        return s_new, inter + intra

    s_end, o = lax.scan(step, s0, (chunks(q), chunks(k), chunks(v), a))
    return o.transpose(1, 0, 3, 2, 4).reshape(B, T, H, v.shape[-1]), s_end


def hgrn_readout(o, g, norm_g):
    B, T = o.shape[:2]
    o = rms_norm(o, norm_g.reshape(HGRN_HEADS, HGRN_DIM)).reshape(B, T, HGRN_W)
    return (o * jax.nn.silu(g.astype(jnp.float32))).astype(g.dtype)


def hgrn2_mix(parts_c, parts_l, lb, norm_g, last):
    q_c, i_c, g_c, zf_c, zb_c = parts_c
    q_l, i_l, g_l, zf_l, zb_l = parts_l
    s0 = jnp.zeros((q_l.shape[0], HGRN_HEADS, HGRN_DIM, HGRN_DIM), jnp.float32)
    o_c, o_l = [], []
    for d, (z_c, z_l) in enumerate(((zf_c, zf_l), (zb_c, zb_l))):
        rev = d == 1
        oc, s_ctx = chunk_gla(*hgrn_inputs(q_c, i_c, z_c, lb[d], rev), s0)
        ol, _ = chunk_gla(*hgrn_inputs(q_l, i_l, z_l, lb[d], rev), s_ctx)
        o_c.append(jnp.flip(oc, axis=1) if rev else oc)
        o_l.append(jnp.flip(ol, axis=1) if rev else ol)
    out_l = hgrn_readout(o_l[0] + o_l[1], g_l, norm_g)
    if last:
        return None, out_l
    return hgrn_readout(o_c[0] + o_c[1], g_c, norm_g), out_l


def token_mixers(h_c, h_l, rope, lb, w_in, q_norm_g, w_uq, kv_norm_g, w_ukv, hgrn_norm_g, w_out, last):
    idx = np.cumsum(IN_SPLITS)[:-1].tolist()
    f_c, *hg_c, cq_c, ckv_c, kr_c = jnp.split(h_c @ w_in, idx, axis=-1)
    f_l, *hg_l, cq_l, ckv_l, kr_l = jnp.split(h_l @ w_in, idx, axis=-1)
    kn_c, krot_c, v_c = mla_keys(ckv_c, kr_c, kv_norm_g, w_ukv, None)
    kn_l, krot_l, v_l = mla_keys(ckv_l, kr_l, kv_norm_g, w_ukv, rope)
    qn_l, qr_l = mla_queries(cq_l, q_norm_g, w_uq, rope)
    att_l = block_attention(qn_l, qr_l, jnp.concatenate([kn_c, kn_l], axis=1),
                            jnp.concatenate([krot_c, krot_l], axis=1), jnp.concatenate([v_c, v_l], axis=1))
    hg_out_c, hg_out_l = hgrn2_mix(hg_c, hg_l, lb, hgrn_norm_g, last)
    out_l = jnp.concatenate([fourier_mix(f_l), hg_out_l, att_l.astype(h_l.dtype)], axis=-1) @ w_out
    if last:
        return None, out_l
    qn_c, qr_c = mla_queries(cq_c, q_norm_g, w_uq, None)
    att_c = block_attention(qn_c, qr_c, kn_c, krot_c, v_c)
    out_c = jnp.concatenate([fourier_mix(f_c), hg_out_c, att_c.astype(h_c.dtype)], axis=-1) @ w_out
    return out_c, out_l


def moe_ffn(h, router_w, router_b, w1, b1, w2, b2):
    n = h.shape[0]
    logits = (h @ router_w + router_b).astype(jnp.float32)
    top_v, top_e = lax.top_k(logits, TOP_K)
    gates = jax.nn.softmax(top_v, axis=-1)
    flat_e = top_e.reshape(-1)
    order = jnp.argsort(flat_e)
    sorted_e = flat_e[order]
    counts = jnp.bincount(flat_e, length=N_EXPERTS)
    padded = (counts + MOE_BLOCK - 1) // MOE_BLOCK * MOE_BLOCK
    pad_end = jnp.cumsum(padded)
    pad_start = pad_end - padded
    start = jnp.cumsum(counts) - counts
    dest = pad_start[sorted_e] + jnp.arange(n * TOP_K) - start[sorted_e]
    m_pad = -(-n * TOP_K // MOE_BLOCK) * MOE_BLOCK + N_EXPERTS * MOE_BLOCK
    n_blocks = m_pad // MOE_BLOCK
    tok_pad = jnp.zeros((m_pad,), jnp.int32).at[dest].set((order // TOP_K).astype(jnp.int32))
    gate_pad = jnp.zeros((m_pad,), h.dtype).at[dest].set(gates.reshape(-1)[order].astype(h.dtype))
    block_e = jnp.minimum(jnp.searchsorted(pad_end, jnp.arange(n_blocks) * MOE_BLOCK, side='right'), N_EXPERTS - 1)

    def run_block(blk):
        tok, gate, e = blk
        u = h[tok] @ w1[e] + b1[e]
        x_glu = jnp.minimum(u[:, 0::2], SWIGLU_LIMIT)
        x_lin = jnp.clip(u[:, 1::2], -SWIGLU_LIMIT, SWIGLU_LIMIT)
        act = x_glu * jax.nn.sigmoid(SWIGLU_ALPHA * x_glu) * (x_lin + 1)
        return (act @ w2[e] + b2[e]) * gate[:, None]

    y = lax.map(run_block, (tok_pad.reshape(n_blocks, MOE_BLOCK), gate_pad.reshape(n_blocks, MOE_BLOCK), block_e))
    return jnp.zeros_like(h).at[tok_pad].add(y.reshape(m_pad, -1).astype(h.dtype))


def setup_inputs(seed: int = 0) -> dict:
    key = jax.random.key(seed)
    ks = jax.random.split(key, 24)
    f32 = jnp.float32

    def nrm(k, shape, scale):
        return jax.random.normal(k, shape, f32) * scale

    def gain(k, shape):
        return 1.0 + 0.02 * jax.random.normal(k, shape, f32)

    hq = MLA_HEADS * (MLA_NOPE + MLA_ROPE)
    hkv = MLA_HEADS * (MLA_NOPE + MLA_V)
    return {
        'x': nrm(ks[0], (BATCH, SEQ, D_MODEL), 1.0),
        'c': nrm(ks[1], (BATCH, D_MODEL), 1.0),
        'ctx': nrm(ks[2], (BATCH, CTX_LEN, D_MODEL), 1.0),
        'c_ctx': nrm(ks[3], (D_MODEL,), 1.0),
        'w_ada': nrm(ks[4], (DEPTH, D_MODEL, N_MOD * D_MODEL), D_MODEL ** -0.5),
        'b_ada': nrm(ks[5], (DEPTH, N_MOD * D_MODEL), 0.02),
        'w_in': nrm(ks[6], (DEPTH, D_MODEL, IN_PROJ_W), D_MODEL ** -0.5),
        'mla_q_norm': gain(ks[7], (DEPTH, Q_LORA)),
        'w_uq': nrm(ks[8], (DEPTH, Q_LORA, hq), Q_LORA ** -0.5),
        'mla_kv_norm': gain(ks[9], (DEPTH, KV_LORA)),
        'w_ukv': nrm(ks[10], (DEPTH, KV_LORA, hkv), KV_LORA ** -0.5),
        'hgrn_lb_logits': nrm(ks[11], (DEPTH, 2, HGRN_W), 0.1),
        'hgrn_norm': gain(ks[12], (DEPTH, HGRN_W)),
        'w_out': nrm(ks[13], (DEPTH, MIX_W, D_MODEL), DEEPNORM_BETA * MIX_W ** -0.5),
        'ln1_g': gain(ks[14], (DEPTH, D_MODEL)),
        'ln1_b': nrm(ks[15], (DEPTH, D_MODEL), 0.02),
        'router_w': nrm(ks[16], (DEPTH, D_MODEL, N_EXPERTS), D_MODEL ** -0.5),
        'router_b': nrm(ks[17], (DEPTH, N_EXPERTS), 0.01),
        'w1': nrm(ks[18], (DEPTH, N_EXPERTS, D_MODEL, 2 * EXPERT_FF), D_MODEL ** -0.5),
        'b1': nrm(ks[19], (DEPTH, N_EXPERTS, 2 * EXPERT_FF), 0.02),
        'w2': nrm(ks[20], (DEPTH, N_EXPERTS, EXPERT_FF, D_MODEL), DEEPNORM_BETA * EXPERT_FF ** -0.5),
        'b2': nrm(ks[21], (DEPTH, N_EXPERTS, D_MODEL), 0.02),
        'ln2_g': gain(ks[22], (DEPTH, D_MODEL)),
        'ln2_b': nrm(ks[23], (DEPTH, D_MODEL), 0.02),
    }


def reference(x, c, ctx, c_ctx, w_ada, b_ada, w_in, mla_q_norm, w_uq, mla_kv_norm, w_ukv, hgrn_lb_logits,
              hgrn_norm, w_out, ln1_g, ln1_b, router_w, router_b, w1, b1, w2, b2, ln2_g, ln2_b):
    B, n_lat, D = x.shape
    rope = axial_rope(n_lat)
    lower_bounds = hgrn_lower_bounds(hgrn_lb_logits)
    for l in range(DEPTH):
        last = l == DEPTH - 1
        mod_l = jnp.split((jax.nn.silu(c) @ w_ada[l] + b_ada[l])[:, None, :], N_MOD, axis=-1)
        mod_c = jnp.split((jax.nn.silu(c_ctx) @ w_ada[l] + b_ada[l])[None, None, :], N_MOD, axis=-1)
        h_l = modulate(x, mod_l[0], mod_l[1])
        h_c = modulate(ctx, mod_c[0], mod_c[1])
        mix_c, mix_l = token_mixers(h_c, h_l, rope, lower_bounds[l], w_in[l], mla_q_norm[l], w_uq[l],
                                    mla_kv_norm[l], w_ukv[l], hgrn_norm[l], w_out[l], last)
        x = post_norm(x, mod_l[2] * mix_l, ln1_g[l], ln1_b[l])
        h_l = modulate(x, mod_l[3], mod_l[4])
        moe_p = (router_w[l], router_b[l], w1[l], b1[l], w2[l], b2[l])
        if last:
            ffn_l = moe_ffn(h_l.reshape(-1, D), *moe_p).reshape(x.shape)
        else:
            ctx = post_norm(ctx, mod_c[2] * mix_c, ln1_g[l], ln1_b[l])
            h_c = modulate(ctx, mod_c[3], mod_c[4])
            n_ctx = ctx.shape[0] * ctx.shape[1]
            ffn_all = moe_ffn(jnp.concatenate([h_c.reshape(-1, D), h_l.reshape(-1, D)], axis=0), *moe_p)
            ctx = post_norm(ctx, mod_c[5] * ffn_all[:n_ctx].reshape(ctx.shape), ln2_g[l], ln2_b[l])
            ffn_l = ffn_all[n_ctx:].reshape(x.shape)
        x = post_norm(x, mod_l[5] * ffn_l, ln2_g[l], ln2_b[l])
    return x
```

```python
import functools

import numpy as np
import jax
import jax.numpy as jnp
from jax import lax
from jax.experimental import pallas as pl
from jax.experimental.pallas import tpu as pltpu

F32 = jnp.float32
BF16 = jnp.bfloat16

D_MODEL = 2048
FOURIER_W = 512
FOURIER_GROUP = 128
HGRN_W = 512
HGRN_HEADS = 4
HGRN_DIM = 128
MLA_HEADS = 8
MLA_NOPE = 128
MLA_ROPE = 64
MLA_V = 128
Q_LORA = 512
KV_LORA = 256
MLA_SCALE = (MLA_NOPE + MLA_ROPE) ** -0.5
N_EXPERTS = 32
TOP_K = 4
EXPERT_FF = 768
SWIGLU_ALPHA = 1.702
SWIGLU_LIMIT = 7.0
N_MOD = 6
GRID_W = 64
ROPE_BASE = 10000.0
EPS = 1e-6

LANES = 128
TM = 256
HEAD_SLOT = 2 * LANES
IN_W = 4096
HG_OFF = FOURIER_W
MLA_OFF = HG_OFF + 5 * HGRN_W
MLA_IN_W = IN_W - MLA_OFF
N_LEVELS = 9
ADA_TN = 1024
NEG_BIG = -1e30
VMEM_LIMIT = 56 * 1024 * 1024


def _cparams(sem, vmem=VMEM_LIMIT):
    return pltpu.CompilerParams(dimension_semantics=sem, vmem_limit_bytes=vmem)


def _dot(a, b):
    return jnp.dot(a, b, preferred_element_type=F32)


def _dot_nt(a, b):
    return lax.dot_general(a, b, (((1,), (1,)), ((), ())), preferred_element_type=F32)


def _dot_tn(a, b):
    return lax.dot_general(a, b, (((0,), (0,)), ((), ())), preferred_element_type=F32)


def _layer_norm(x):
    mu = jnp.mean(x, axis=-1, keepdims=True)
    xc = x - mu
    var = jnp.mean(xc * xc, axis=-1, keepdims=True)
    return xc * lax.rsqrt(var + EPS)


def _rms(x):
    return x * lax.rsqrt(jnp.mean(x * x, axis=-1, keepdims=True) + EPS)


def _sigmoid(x):
    return 1.0 / (1.0 + jnp.exp(-x))


def _mod_row(i, tpb, nct, nb):
    return jnp.where(i % tpb < nct, nb, i // tpb)


def _mod(mod_ref, row, k):
    return mod_ref[pl.ds(row, 1), k * D_MODEL:(k + 1) * D_MODEL]


def _ada_kernel(c_ref, w_ref, b_ref, o_ref):
    c = c_ref[...]
    s = c * _sigmoid(c)
    hi = s.astype(BF16)
    lo = (s - hi.astype(F32)).astype(BF16)
    w = w_ref[...].astype(BF16)
    o_ref[...] = _dot(hi, w) + _dot(lo, w) + b_ref[...]


def _ada(cc, w_ada, b_ada):
    depth, d, n = w_ada.shape
    rows = cc.shape[0]
    return pl.pallas_call(
        _ada_kernel,
        grid=(depth, n // ADA_TN),
        in_specs=[
            pl.BlockSpec((rows, d), lambda l, j: (0, 0)),
            pl.BlockSpec((None, d, ADA_TN), lambda l, j: (l, 0, j)),
            pl.BlockSpec((None, 1, ADA_TN), lambda l, j: (l, 0, j)),
        ],
        out_specs=pl.BlockSpec((None, rows, ADA_TN), lambda l, j: (l, 0, j)),
        out_shape=jax.ShapeDtypeStruct((depth, rows, n), F32),
        compiler_params=_cparams(("arbitrary", "arbitrary")),
        name="ada",
    )(cc, w_ada, b_ada.reshape(depth, 1, n))


def _pre_kernel(x_ref, mod_ref, w_ref, dft_ref, z_ref, hg_ref, mla_ref, *, tpb, nct, nb):
    row = _mod_row(pl.program_id(0), tpb, nct, nb)
    h = _layer_norm(x_ref[...]) * (1.0 + _mod(mod_ref, row, 1)) + _mod(mod_ref, row, 0)
    r = _dot(h.astype(BF16), w_ref[...])
    z_ref[...] = _dot(r[:, 0:FOURIER_W].astype(BF16), dft_ref[...]).astype(BF16)
    hg_ref[...] = r[:, HG_OFF:MLA_OFF]
    mla_ref[...] = r[:, MLA_OFF:IN_W]


def _pre(x, mod, w_in, dft_c, *, tpb, nct, nb):
    n = x.shape[0]
    kern = functools.partial(_pre_kernel, tpb=tpb, nct=nct, nb=nb)
    return pl.pallas_call(
        kern,
        grid=(n // TM,),
        in_specs=[
            pl.BlockSpec((TM, D_MODEL), lambda i: (i, 0)),
            pl.BlockSpec(mod.shape, lambda i: (0, 0)),
            pl.BlockSpec((D_MODEL, IN_W), lambda i: (0, 0), pipeline_mode=pl.Buffered(1)),
            pl.BlockSpec(dft_c.shape, lambda i: (0, 0)),
        ],
        out_specs=[
            pl.BlockSpec((TM, 2 * FOURIER_W), lambda i: (i, 0)),
            pl.BlockSpec((TM, 5 * HGRN_W), lambda i: (i, 0)),
            pl.BlockSpec((TM, MLA_IN_W), lambda i: (i, 0)),
        ],
        out_shape=[
            jax.ShapeDtypeStruct((n, 2 * FOURIER_W), BF16),
            jax.ShapeDtypeStruct((n, 5 * HGRN_W), F32),
            jax.ShapeDtypeStruct((n, MLA_IN_W), F32),
        ],
        compiler_params=_cparams(("arbitrary",)),
        name="pre",
    )(x, mod, w_in, dft_c)


def _four_kernel(z_ref, cc_ref, sc_ref, cl_ref, sl_ref, y_ref, *, nct, ctx_len):
    j = pl.program_id(1)

    def emit(cm, sm, z):
        y = _dot(cm, z[:, 0:FOURIER_W]) - _dot(sm, z[:, FOURIER_W:])
        y_ref[...] = y.astype(BF16)

    @pl.when(j < nct)
    def _():
        emit(cc_ref[...], sc_ref[...], z_ref[0:ctx_len, :])

    @pl.when(j >= nct)
    def _():
        emit(cl_ref[...], sl_ref[...], z_ref[ctx_len:, :])


def _fourier(z, dft_ctx, dft_lat, *, nb, tpb, nct, ctx_len, seq):
    n = z.shape[0]
    r = ctx_len + seq
    cc, sc = dft_ctx
    cl, sl = dft_lat
    kern = functools.partial(_four_kernel, nct=nct, ctx_len=ctx_len)
    cmap = lambda b, j: (jnp.minimum(j, nct - 1), 0)
    lmap = lambda b, j: (jnp.maximum(j - nct, 0), 0)
    return pl.pallas_call(
        kern,
        grid=(nb, tpb),
        in_specs=[
            pl.BlockSpec((None, r, 2 * FOURIER_W), lambda b, j: (b, 0, 0)),
            pl.BlockSpec((TM, ctx_len), cmap),
            pl.BlockSpec((TM, ctx_len), cmap),
            pl.BlockSpec((TM, seq), lmap),
            pl.BlockSpec((TM, seq), lmap),
        ],
        out_specs=pl.BlockSpec((TM, FOURIER_W), lambda b, j: (b * tpb + j, 0)),
        out_shape=jax.ShapeDtypeStruct((n, FOURIER_W), BF16),
        compiler_params=_cparams(("arbitrary", "arbitrary")),
        name="fourier",
    )(z.reshape(nb, r, 2 * FOURIER_W), cc, sc, cl, sl)


def _level_matrices(reverse):
    p = np.arange(TM)
    mats = [(p[None, :] <= p[:, None])]
    for lev in range(1, N_LEVELS):
        r = TM >> lev
        same = (p[None, :] // r) == (p[:, None] // r)
        is_q = ((p // r) % 2 == 1)[:, None]
        mats.append(same & np.where(is_q, p[None, :] <= p[:, None], p[None, :] > p[:, None]))
    g = np.stack(mats).astype(np.float32)
    if reverse:
        g = g[:, ::-1, ::-1]
    return jnp.asarray(g.reshape(N_LEVELS * TM, TM), dtype=BF16)


def _hgrn_kernel(q_ref, v_ref, z_ref, lb_ref, g_ref, o_ref, st_ref, *, reverse):
    @pl.when(pl.program_id(1) == 0)
    def _():
        st_ref[...] = jnp.zeros_like(st_ref)

    z = z_ref[...]
    log_lb = lb_ref[0:1, :]
    log_1m_lb = lb_ref[1:2, :]
    one_m_lb = lb_ref[2:3, :]
    log_sig = jnp.minimum(z, 0.0) - jnp.log1p(jnp.exp(-jnp.abs(z)))
    a2 = log_1m_lb + log_sig
    log_f = jnp.maximum(log_lb, a2) + jnp.log1p(jnp.exp(-jnp.abs(log_lb - a2)))
    kk = one_m_lb * _sigmoid(-z)

    hi = log_f.astype(BF16)
    lo = (log_f - hi.astype(F32)).astype(BF16)
    g = g_ref[...]
    wall = _dot(g, hi) + _dot(g, lo)
    a = wall[0:TM]
    a_end = a[0:1] if reverse else a[TM - 1:TM]
    q = q_ref[...]
    v = v_ref[...].astype(BF16)
    qa = (q * jnp.exp(a)).astype(BF16)
    kd = (kk * jnp.exp(a_end - a)).astype(BF16)
    carry = jnp.exp(a_end)

    row = lax.broadcasted_iota(jnp.int32, (TM, TM), 0)
    col = lax.broadcasted_iota(jnp.int32, (TM, TM), 1)
    diff = row ^ col
    pos = lax.broadcasted_iota(jnp.int32, (TM, HGRN_DIM), 0)
    if reverse:
        pos = TM - 1 - pos

    for h in range(HGRN_HEADS):
        sl = slice(h * HGRN_DIM, (h + 1) * HGRN_DIM)
        st = st_ref[h]
        qh, kh, vh = q[:, sl], kk[:, sl], v[:, sl]
        p = jnp.where(diff == 0, _dot_nt(qh.astype(BF16), kh.astype(BF16)), 0.0)
        for lev in range(1, N_LEVELS):
            shift = (TM >> lev).bit_length() - 1
            fac = jnp.exp(wall[lev * TM:(lev + 1) * TM, sl])
            is_q = ((pos >> shift) & 1) == 1
            qt = jnp.where(is_q, qh * fac, 0.0).astype(BF16)
            kt = jnp.where(is_q, 0.0, kh * fac).astype(BF16)
            p = p + jnp.where((diff >> (shift + 1)) == 0, _dot_nt(qt, kt), 0.0)
        o_ref[:, sl] = _dot_nt(qa[:, sl], st.astype(BF16)) + _dot(p.astype(BF16), vh)
        st_ref[h] = st * carry[:, sl] + _dot_tn(vh, kd[:, sl])


def _hgrn(hg, lb_rows, *, z_block, reverse, nb, tpb, nct):
    n = hg.shape[0]

    def chunk(b, s):
        if not reverse:
            return b * tpb + s
        return b * tpb + jnp.where(s < nct, nct - 1 - s, tpb - 1 - (s - nct))

    kern = functools.partial(_hgrn_kernel, reverse=reverse)
    gmat = _level_matrices(reverse)
    return pl.pallas_call(
        kern,
        grid=(nb, tpb),
        in_specs=[
            pl.BlockSpec((TM, HGRN_W), lambda b, s: (chunk(b, s), 0)),
            pl.BlockSpec((TM, HGRN_W), lambda b, s: (chunk(b, s), 1)),
            pl.BlockSpec((TM, HGRN_W), lambda b, s: (chunk(b, s), z_block)),
            pl.BlockSpec((8, HGRN_W), lambda b, s: (0, 0)),
            pl.BlockSpec(gmat.shape, lambda b, s: (0, 0)),
        ],
        out_specs=pl.BlockSpec((TM, HGRN_W), lambda b, s: (chunk(b, s), 0)),
        out_shape=jax.ShapeDtypeStruct((n, HGRN_W), F32),
        scratch_shapes=[pltpu.VMEM((HGRN_HEADS, HGRN_DIM, HGRN_DIM), F32)],
        compiler_params=_cparams(("arbitrary", "arbitrary")),
        name="hgrn_bwd" if reverse else "hgrn_fwd",
    )(hg, hg, hg, lb_rows, gmat)


def _mla_kernel(m_ref, qg_ref, kvg_ref, wq_ref, wkv_ref, cos_ref, sin_ref, q_ref, k_ref, v_ref):
    cq = m_ref[:, 0:Q_LORA]
    ckv = m_ref[:, Q_LORA:Q_LORA + KV_LORA]
    kr = m_ref[:, Q_LORA + KV_LORA:Q_LORA + KV_LORA + LANES]
    kr_sw = m_ref[:, Q_LORA + KV_LORA + LANES:MLA_IN_W]
    cos = cos_ref[...]
    sin = sin_ref[...]
    qall = _dot((_rms(cq) * qg_ref[...]).astype(BF16), wq_ref[...]) * MLA_SCALE
    kv = _dot((_rms(ckv) * kvg_ref[...]).astype(BF16), wkv_ref[...])
    k_rot = (kr * cos + kr_sw * sin).astype(BF16)
    nw = MLA_HEADS * LANES
    for h in range(MLA_HEADS):
        a, b = h * LANES, (h + 1) * LANES
        s = h * HEAD_SLOT
        q_ref[:, s:s + LANES] = qall[:, a:b].astype(BF16)
        q_ref[:, s + LANES:s + HEAD_SLOT] = (qall[:, nw + a:nw + b] * cos + qall[:, 2 * nw + a:2 * nw + b] * sin).astype(BF16)
        k_ref[:, s:s + LANES] = kv[:, a:b].astype(BF16)
        k_ref[:, s + LANES:s + HEAD_SLOT] = k_rot
    v_ref[...] = kv[:, nw:].astype(BF16)


def _mla_up(mla_in, q_gain, kv_gain, wq, wkv, cos_t, sin_t):
    n = mla_in.shape[0]
    qk_w = MLA_HEADS * HEAD_SLOT
    v_w = MLA_HEADS * MLA_V
    full = lambda a: pl.BlockSpec(a.shape, lambda i: (0, 0))
    return pl.pallas_call(
        _mla_kernel,
        grid=(n // TM,),
        in_specs=[
            pl.BlockSpec((TM, MLA_IN_W), lambda i: (i, 0)),
            full(q_gain), full(kv_gain), full(wq), full(wkv),
            pl.BlockSpec((TM, LANES), lambda i: (i, 0)),
            pl.BlockSpec((TM, LANES), lambda i: (i, 0)),
        ],
        out_specs=[
            pl.BlockSpec((TM, qk_w), lambda i: (i, 0)),
            pl.BlockSpec((TM, qk_w), lambda i: (i, 0)),
            pl.BlockSpec((TM, v_w), lambda i: (i, 0)),
        ],
        out_shape=[
            jax.ShapeDtypeStruct((n, qk_w), BF16),
            jax.ShapeDtypeStruct((n, qk_w), BF16),
            jax.ShapeDtypeStruct((n, v_w), BF16),
        ],
        compiler_params=_cparams(("arbitrary",)),
        name="mla_up",
    )(mla_in, q_gain, kv_gain, wq, wkv, cos_t, sin_t)


HEADS_PER_STEP = 2


def _attn_kernel(q_ref, k_ref, v_ref, o_ref, *, nct, ctx_len):
    j = pl.program_id(2)

    def run(n_keys):
        for hh in range(HEADS_PER_STEP):
            q = q_ref[:, hh * HEAD_SLOT:(hh + 1) * HEAD_SLOT]
            k = k_ref[0:n_keys, hh * HEAD_SLOT:(hh + 1) * HEAD_SLOT]
            v = v_ref[0:n_keys, hh * MLA_V:(hh + 1) * MLA_V]
            s = _dot_nt(q, k)
            p = jnp.exp(s - jnp.max(s, axis=-1, keepdims=True))
            denom = jnp.sum(p, axis=-1, keepdims=True)
            o = _dot(p.astype(BF16), v) / denom
            o_ref[:, hh * MLA_V:(hh + 1) * MLA_V] = o.astype(BF16)

    @pl.when(j < nct)
    def _():
        run(ctx_len)

    @pl.when(j >= nct)
    def _():
        run(k_ref.shape[0])


def _attention(q, k, v, *, nb, tpb, nct, ctx_len):
    n = q.shape[0]
    r = n // nb
    hp = MLA_HEADS // HEADS_PER_STEP
    kern = functools.partial(_attn_kernel, nct=nct, ctx_len=ctx_len)
    return pl.pallas_call(
        kern,
        grid=(nb, hp, tpb),
        in_specs=[
            pl.BlockSpec((TM, HEADS_PER_STEP * HEAD_SLOT), lambda b, h, j: (b * tpb + j, h)),
            pl.BlockSpec((None, r, HEADS_PER_STEP * HEAD_SLOT), lambda b, h, j: (b, 0, h)),
            pl.BlockSpec((None, r, HEADS_PER_STEP * MLA_V), lambda b, h, j: (b, 0, h)),
        ],
        out_specs=pl.BlockSpec((TM, HEADS_PER_STEP * MLA_V), lambda b, h, j: (b * tpb + j, h)),
        out_shape=jax.ShapeDtypeStruct((n, MLA_HEADS * MLA_V), BF16),
        compiler_params=_cparams(("arbitrary", "arbitrary", "arbitrary")),
        name="attention",
    )(q, k.reshape(nb, r, -1), v.reshape(nb, r, -1))


def _pack_rows(x):
    w = x.shape[1] // 2
    lo = pltpu.bitcast(x[:, :w].astype(BF16).astype(F32), jnp.uint32)
    hi = pltpu.bitcast(x[:, w:].astype(BF16).astype(F32), jnp.uint32)
    return (lo >> 16) | (hi & jnp.uint32(0xFFFF0000))


def _unpack_rows(w):
    lo = pltpu.bitcast(w << 16, F32)
    hi = pltpu.bitcast(w & jnp.uint32(0xFFFF0000), F32)
    return lo, hi


def _post_kernel(yf_ref, of_ref, ob_ref, g_ref, att_ref, x_ref, mod_ref, hn_ref, wo_ref, lng_ref, lnb_ref,
                 rwh_ref, rwl_ref, rb_ref, x1_ref, h2_ref, eid_ref, gate_ref, *, tpb, nct, nb, alpha):
    row = _mod_row(pl.program_id(0), tpb, nct, nb)
    o = of_ref[...] + ob_ref[...]
    hn = hn_ref[...]
    gate_act = g_ref[...]
    gate_act = gate_act * _sigmoid(gate_act)
    mix = _dot(yf_ref[...], wo_ref[0:FOURIER_W, :])
    for h in range(HGRN_HEADS):
        sl = slice(h * HGRN_DIM, (h + 1) * HGRN_DIM)
        r = (_rms(o[:, sl]) * hn[:, sl] * gate_act[:, sl]).astype(BF16)
        mix = mix + _dot(r, wo_ref[FOURIER_W + h * HGRN_DIM:FOURIER_W + (h + 1) * HGRN_DIM, :])
    mix = mix + _dot(att_ref[...], wo_ref[FOURIER_W + HGRN_W:, :])

    x1 = _layer_norm(alpha * x_ref[...] + _mod(mod_ref, row, 2) * mix) * lng_ref[...] + lnb_ref[...]
    x1_ref[...] = x1
    h2 = _layer_norm(x1) * (1.0 + _mod(mod_ref, row, 4)) + _mod(mod_ref, row, 3)
    h2_ref[...] = _pack_rows(h2)

    hi = h2.astype(BF16)
    lo = (h2 - hi.astype(F32)).astype(BF16)
    logits = _dot(hi, rwh_ref[...]) + _dot(lo, rwh_ref[...]) + _dot(hi, rwl_ref[...]) + rb_ref[...]
    lane = lax.broadcasted_iota(jnp.int32, logits.shape, 1)
    cur = logits
    vals, ids = [], []
    for _ in range(TOP_K):
        m = jnp.max(cur, axis=-1, keepdims=True)
        idx = jnp.min(jnp.where(cur == m, lane, LANES), axis=-1, keepdims=True)
        vals.append(m)
        ids.append(idx)
        cur = jnp.where(lane == idx, -3e38, cur)
    ex = [jnp.exp(vk - vals[0]) for vk in vals]
    den = ex[0] + ex[1] + ex[2] + ex[3]
    eid = jnp.zeros(logits.shape, jnp.int32)
    gate = jnp.zeros(logits.shape, F32)
    for k in range(TOP_K):
        eid = jnp.where(lane == k, ids[k], eid)
        gate = jnp.where(lane == k, ex[k] / den, gate)
    eid_ref[...] = eid
    gate_ref[...] = gate


def _post(yf, o_f, o_b, hg, att, x, mod, hnorm, w_out, ln_g, ln_b, rw_hi, rw_lo, rb, *, tpb, nct, nb, alpha):
    n = x.shape[0]
    kern = functools.partial(_post_kernel, tpb=tpb, nct=nct, nb=nb, alpha=alpha)
    full = lambda a: pl.BlockSpec(a.shape, lambda i: (0, 0))
    return pl.pallas_call(
        kern,
        grid=(n // TM,),
        in_specs=[
            pl.BlockSpec((TM, FOURIER_W), lambda i: (i, 0)),
            pl.BlockSpec((TM, HGRN_W), lambda i: (i, 0)),
            pl.BlockSpec((TM, HGRN_W), lambda i: (i, 0)),
            pl.BlockSpec((TM, HGRN_W), lambda i: (i, 2)),
            pl.BlockSpec((TM, MLA_HEADS * MLA_V), lambda i: (i, 0)),
            pl.BlockSpec((TM, D_MODEL), lambda i: (i, 0)),
            full(mod), full(hnorm),
            pl.BlockSpec((D_MODEL, D_MODEL), lambda i: (0, 0), pipeline_mode=pl.Buffered(1)),
            full(ln_g), full(ln_b), full(rw_hi), full(rw_lo), full(rb),
        ],
        out_specs=[
            pl.BlockSpec((TM, D_MODEL), lambda i: (i, 0)),
            pl.BlockSpec((TM, D_MODEL // 2), lambda i: (i, 0)),
            pl.BlockSpec((TM, LANES), lambda i: (i, 0)),
            pl.BlockSpec((TM, LANES), lambda i: (i, 0)),
        ],
        out_shape=[
            jax.ShapeDtypeStruct((n, D_MODEL), F32),
            jax.ShapeDtypeStruct((n, D_MODEL // 2), jnp.uint32),
            jax.ShapeDtypeStruct((n, LANES), jnp.int32),
            jax.ShapeDtypeStruct((n, LANES), F32),
        ],
        compiler_params=_cparams(("arbitrary",)),
        name="post",
    )(yf, o_f, o_b, hg, att, x, mod, hnorm, w_out, ln_g, ln_b, rw_hi, rw_lo, rb)


def _row_copy(src_hbm, dst_vmem, src_row, dst_row, sem):
    return pltpu.make_async_copy(src_hbm.at[pl.ds(src_row, 1), :], dst_vmem.at[pl.ds(dst_row, 1), :], sem)


def _gather_kernel(tok_ref, nu_ref, h_hbm, o_ref, sem):
    i = pl.program_id(0)

    @pl.when(i < nu_ref[0])
    def _():
        def start(r, c):
            _row_copy(h_hbm, o_ref, tok_ref[i * TM + r], r, sem).start()
            return c

        def wait(r, c):
            _row_copy(h_hbm, o_ref, 0, r, sem).wait()
            return c

        lax.fori_loop(0, TM, start, 0)
        lax.fori_loop(0, TM, wait, 0)

    @pl.when(i >= nu_ref[0])
    def _():
        o_ref[...] = jnp.zeros_like(o_ref)


def _gather_rows(tok_sorted, n_used, h2p, n_tiles):
    w = h2p.shape[1]
    return pl.pallas_call(
        _gather_kernel,
        grid_spec=pltpu.PrefetchScalarGridSpec(
            num_scalar_prefetch=2,
            grid=(n_tiles,),
            in_specs=[pl.BlockSpec(memory_space=pl.ANY)],
            out_specs=pl.BlockSpec((TM, w), lambda i, tok, nu: (i, 0)),
            scratch_shapes=[pltpu.SemaphoreType.DMA(())],
        ),
        out_shape=jax.ShapeDtypeStruct((n_tiles * TM, w), jnp.uint32),
        compiler_params=_cparams(("arbitrary",)),
        name="moe_gather",
    )(tok_sorted, n_used, h2p)


def _moe_kernel(te_ref, nu_ref, x_ref, w1g_ref, w1l_ref, b1g_ref, b1l_ref, w2_ref, b2_ref, y_ref):
    i = pl.program_id(0)

    @pl.when(i < nu_ref[0])
    def _():
        half = D_MODEL // 2
        lo, hi = _unpack_rows(x_ref[...])
        lo = lo.astype(BF16)
        hi = hi.astype(BF16)
        ug = _dot(lo, w1g_ref[0:half, :]) + _dot(hi, w1g_ref[half:, :]) + b1g_ref[...]
        ul = _dot(lo, w1l_ref[0:half, :]) + _dot(hi, w1l_ref[half:, :]) + b1l_ref[...]
        xg = jnp.minimum(ug, SWIGLU_LIMIT)
        xl = jnp.clip(ul, -SWIGLU_LIMIT, SWIGLU_LIMIT)
        act = xg * _sigmoid(SWIGLU_ALPHA * xg) * (xl + 1.0)
        y = _dot(act.astype(BF16), w2_ref[...]) + b2_ref[...]
        y_ref[...] = _pack_rows(y)

    @pl.when(i >= nu_ref[0])
    def _():
        y_ref[...] = jnp.zeros_like(y_ref)


def _moe_ffn(tile_e, n_used, xs, w1g, w1l, b1g, b1l, w2, b2):
    m = xs.shape[0]
    emap = lambda i, te, nu: (te[i], 0, 0)
    return pl.pallas_call(
        _moe_kernel,
        grid_spec=pltpu.PrefetchScalarGridSpec(
            num_scalar_prefetch=2,
            grid=(m // TM,),
            in_specs=[
                pl.BlockSpec((TM, D_MODEL // 2), lambda i, te, nu: (i, 0)),
                pl.BlockSpec((None, D_MODEL, EXPERT_FF), emap),
                pl.BlockSpec((None, D_MODEL, EXPERT_FF), emap),
                pl.BlockSpec((None, 1, EXPERT_FF), emap),
                pl.BlockSpec((None, 1, EXPERT_FF), emap),
                pl.BlockSpec((None, EXPERT_FF, D_MODEL), emap),
                pl.BlockSpec((None, 1, D_MODEL), emap),
            ],
            out_specs=pl.BlockSpec((TM, D_MODEL // 2), lambda i, te, nu: (i, 0)),
        ),
        out_shape=jax.ShapeDtypeStruct((m, D_MODEL // 2), jnp.uint32),
        compiler_params=_cparams(("arbitrary",)),
        name="moe_ffn",
    )(tile_e, n_used, xs, w1g, w1l, b1g, b1l, w2, b2)


def _combine_kernel(pos_ref, y_hbm, gate_ref, x_ref, mod_ref, lng_ref, lnb_ref, o_ref, buf, sem, *, tpb, nct, nb, alpha):
    i = pl.program_id(0)
    row = _mod_row(i, tpb, nct, nb)

    def start(r, c):
        for k in range(TOP_K):
            _row_copy(y_hbm, buf.at[k], pos_ref[(i * TM + r) * TOP_K + k], r, sem).start()
        return c

    def wait(r, c):
        for k in range(TOP_K):
            _row_copy(y_hbm, buf.at[k], 0, r, sem).wait()
        return c

    lax.fori_loop(0, TM, start, 0)
    lax.fori_loop(0, TM, wait, 0)

    gates = gate_ref[...]
    half = D_MODEL // 2
    acc_lo = jnp.zeros((TM, half), F32)
    acc_hi = jnp.zeros((TM, half), F32)
    for k in range(TOP_K):
        lo, hi = _unpack_rows(buf[k])
        gk = gates[:, k:k + 1]
        acc_lo = acc_lo + gk * lo
        acc_hi = acc_hi + gk * hi
    ffn = jnp.concatenate([acc_lo, acc_hi], axis=-1)
    x2 = _layer_norm(alpha * x_ref[...] + _mod(mod_ref, row, 5) * ffn) * lng_ref[...] + lnb_ref[...]
    o_ref[...] = x2


def _combine(pos, ys, gates, x, mod, ln_g, ln_b, *, tpb, nct, nb, alpha):
    n = x.shape[0]
    kern = functools.partial(_combine_kernel, tpb=tpb, nct=nct, nb=nb, alpha=alpha)
    full = lambda a: pl.BlockSpec(a.shape, lambda i, p: (0, 0))
    return pl.pallas_call(
        kern,
        grid_spec=pltpu.PrefetchScalarGridSpec(
            num_scalar_prefetch=1,
            grid=(n // TM,),
            in_specs=[
                pl.BlockSpec(memory_space=pl.ANY),
                pl.BlockSpec((TM, LANES), lambda i, p: (i, 0)),
                pl.BlockSpec((TM, D_MODEL), lambda i, p: (i, 0)),
                full(mod), full(ln_g), full(ln_b),
            ],
            out_specs=pl.BlockSpec((TM, D_MODEL), lambda i, p: (i, 0)),
            scratch_shapes=[pltpu.VMEM((TOP_K, TM, D_MODEL // 2), jnp.uint32), pltpu.SemaphoreType.DMA(())],
        ),
        out_shape=jax.ShapeDtypeStruct((n, D_MODEL), F32),
        compiler_params=_cparams(("arbitrary",)),
        name="moe_combine",
    )(pos, ys, gates, x, mod, ln_g, ln_b)


def _route(eid):
    n = eid.shape[0]
    n_slots = n * TOP_K
    n_tiles = n_slots // TM + N_EXPERTS
    flat_e = eid.reshape(-1)
    onehot = (flat_e[:, None] == jnp.arange(N_EXPERTS, dtype=jnp.int32)[None, :]).astype(jnp.int32)
    csum = jnp.cumsum(onehot, axis=0)
    rank = jnp.sum(csum * onehot, axis=1) - 1
    counts = csum[-1]
    padded = (counts + TM - 1) // TM * TM
    pad_end = jnp.cumsum(padded)
    pad_start = pad_end - padded
    pos = (pad_start[flat_e] + rank).astype(jnp.int32)
    tok_sorted = jnp.zeros((n_tiles * TM,), jnp.int32).at[pos].set(jnp.arange(n_slots, dtype=jnp.int32) // TOP_K)
    tile_start = jnp.arange(n_tiles, dtype=jnp.int32) * TM
    tile_e = jnp.minimum(jnp.searchsorted(pad_end, tile_start, side="right"), N_EXPERTS - 1).astype(jnp.int32)
    n_used = (pad_end[-1] // TM).astype(jnp.int32).reshape(1)
    return pos, tok_sorted, tile_e, n_used, n_tiles


def _dft_pair(t_len):
    t = jnp.arange(t_len, dtype=jnp.int32)
    ang = ((t[:, None] * t[None, :]) % t_len).astype(F32) * (2.0 * np.pi / t_len)
    s = 1.0 / np.sqrt(t_len)
    return (jnp.cos(ang) * s).astype(BF16), (jnp.sin(ang) * s).astype(BF16)


def _channel_dft():
    c = np.arange(FOURIER_GROUP)
    ang = 2.0 * np.pi * ((c[:, None] * c[None, :]) % FOURIER_GROUP) / FOURIER_GROUP
    s = 1.0 / np.sqrt(FOURIER_GROUP)
    groups = FOURIER_W // FOURIER_GROUP
    m = np.zeros((FOURIER_W, 2 * FOURIER_W), np.float32)
    for g in range(groups):
        a, b = g * FOURIER_GROUP, (g + 1) * FOURIER_GROUP
        m[a:b, a:b] = np.cos(ang) * s
        m[a:b, FOURIER_W + a:FOURIER_W + b] = np.sin(ang) * s
    return jnp.asarray(m, dtype=BF16)


def _pair_swap(w):
    pairs = w.reshape(*w.shape[:-1], w.shape[-1] // 2, 2)
    return jnp.stack([-pairs[..., 1], pairs[..., 0]], axis=-1).reshape(w.shape)


def _pad_lanes(w, width):
    return jnp.pad(w, [(0, 0)] * (w.ndim - 1) + [(0, width - w.shape[-1])])


def _rope_tables(nb, ctx_len, seq):
    t = jnp.arange(seq)
    n_freq = MLA_ROPE // 4
    inv = ROPE_BASE ** (-jnp.arange(n_freq, dtype=F32) / n_freq)
    ang = jnp.concatenate([(t // GRID_W)[:, None] * inv, (t % GRID_W)[:, None] * inv], axis=-1)
    ang = jnp.repeat(ang, 2, axis=-1)
    cos = jnp.concatenate([jnp.ones((ctx_len, MLA_ROPE), F32), jnp.cos(ang)], axis=0)
    sin = jnp.concatenate([jnp.zeros((ctx_len, MLA_ROPE), F32), jnp.sin(ang)], axis=0)
    cos = jnp.tile(_pad_lanes(cos, LANES), (nb, 1))
    sin = jnp.tile(_pad_lanes(sin, LANES), (nb, 1))
    return cos, sin


def _prep_in_proj(w_in):
    f, q, i, g, zf, zb, cq, ckv, kr = jnp.split(w_in, np.cumsum([512, 512, 512, 512, 512, 512, 512, 256]).tolist(), axis=-1)
    cols = [f, q, i, g, zf, zb, cq, ckv, _pad_lanes(kr, LANES), _pad_lanes(_pair_swap(kr), LANES)]
    return jnp.concatenate(cols, axis=-1).astype(BF16)


def _prep_wq(w_uq):
    w = w_uq.reshape(Q_LORA, MLA_HEADS, MLA_NOPE + MLA_ROPE)
    nope = w[:, :, :MLA_NOPE].reshape(Q_LORA, -1)
    rope = w[:, :, MLA_NOPE:]
    rope_p = _pad_lanes(rope, LANES).reshape(Q_LORA, -1)
    rope_s = _pad_lanes(_pair_swap(rope), LANES).reshape(Q_LORA, -1)
    return jnp.concatenate([nope, rope_p, rope_s], axis=-1).astype(BF16)


def _prep_wkv(w_ukv):
    w = w_ukv.reshape(KV_LORA, MLA_HEADS, MLA_NOPE + MLA_V)
    return jnp.concatenate([w[:, :, :MLA_NOPE].reshape(KV_LORA, -1), w[:, :, MLA_NOPE:].reshape(KV_LORA, -1)], axis=-1).astype(BF16)


def kernel(x, c, ctx, c_ctx, w_ada, b_ada, w_in, mla_q_norm, w_uq, mla_kv_norm, w_ukv, hgrn_lb_logits, hgrn_norm,
           w_out, ln1_g, ln1_b, router_w, router_b, w1, b1, w2, b2, ln2_g, ln2_b):
    nb, seq, d = x.shape
    ctx_len = ctx.shape[1]
    depth = w_ada.shape[0]
    assert d == D_MODEL and seq % TM == 0 and ctx_len % TM == 0 and seq % GRID_W == 0
    r = ctx_len + seq
    tpb, nct = r // TM, ctx_len // TM
    n = nb * r
    alpha = float((2 * depth) ** 0.25)
    geo = dict(tpb=tpb, nct=nct, nb=nb)

    xs = jnp.concatenate([ctx, x], axis=1).reshape(n, d)

    mod_rows = -(-(nb + 1) // 8) * 8
    cc = jnp.zeros((mod_rows, d), F32).at[:nb].set(c).at[nb].set(c_ctx)
    mod_all = _ada(cc, w_ada, b_ada)

    lb = jnp.cumsum(jax.nn.softmax(hgrn_lb_logits.astype(F32), axis=0), axis=0)
    lb = lb - lb[:1]
    lb_rows = jnp.stack([jnp.log(lb), jnp.log1p(-lb), 1.0 - lb] + [jnp.zeros_like(lb)] * 5, axis=2)

    dft_c = _channel_dft()
    dft_ctx = _dft_pair(ctx_len)
    dft_lat = _dft_pair(seq)
    cos_t, sin_t = _rope_tables(nb, ctx_len, seq)

    for l in range(depth):
        mod = mod_all[l]
        z, hg, mla_in = _pre(xs, mod, _prep_in_proj(w_in[l]), dft_c, **geo)
        yf = _fourier(z, dft_ctx, dft_lat, nb=nb, tpb=tpb, nct=nct, ctx_len=ctx_len, seq=seq)
        o_f = _hgrn(hg, lb_rows[l, 0], z_block=3, reverse=False, **geo)
        o_b = _hgrn(hg, lb_rows[l, 1], z_block=4, reverse=True, **geo)
        q, k, v = _mla_up(mla_in, mla_q_norm[l].reshape(1, -1), mla_kv_norm[l].reshape(1, -1),
                          _prep_wq(w_uq[l]), _prep_wkv(w_ukv[l]), cos_t, sin_t)
        att = _attention(q, k, v, nb=nb, tpb=tpb, nct=nct, ctx_len=ctx_len)

        rw = _pad_lanes(router_w[l], LANES)
        rw_hi = rw.astype(BF16)
        rw_lo = (rw - rw_hi.astype(F32)).astype(BF16)
        rb = jnp.full((1, LANES), NEG_BIG, F32).at[0, :N_EXPERTS].set(router_b[l])
        x1, h2p, eid, gates = _post(yf, o_f, o_b, hg, att, xs, mod, hgrn_norm[l].reshape(1, -1), w_out[l].astype(BF16),
                                    ln1_g[l].reshape(1, -1), ln1_b[l].reshape(1, -1), rw_hi, rw_lo, rb, alpha=alpha, **geo)

        pos, tok_sorted, tile_e, n_used, n_tiles = _route(eid[:, :TOP_K])
        xg = _gather_rows(tok_sorted, n_used, h2p, n_tiles)
        w1g = w1[l, :, :, 0::2].astype(BF16)
        w1l = w1[l, :, :, 1::2].astype(BF16)
        b1g = b1[l, :, 0::2].reshape(N_EXPERTS, 1, EXPERT_FF)
        b1l = b1[l, :, 1::2].reshape(N_EXPERTS, 1, EXPERT_FF)
        ys = _moe_ffn(tile_e, n_used, xg, w1g, w1l, b1g, b1l, w2[l].astype(BF16), b2[l].reshape(N_EXPERTS, 1, d))
        xs = _combine(pos, ys, gates, x1, mod, ln2_g[l].reshape(1, -1), ln2_b[l].reshape(1, -1), alpha=alpha, **geo)

    return xs.reshape(nb, r, d)[:, ctx_len:, :]
```

```python
import functools

import numpy as np
import jax
import jax.numpy as jnp
from jax import lax
from jax.experimental import pallas as pl
from jax.experimental.pallas import tpu as pltpu

F32 = jnp.float32
BF16 = jnp.bfloat16
U32 = jnp.uint32
I32 = jnp.int32

D_MODEL = 2048
FOURIER_W = 512
FOURIER_GROUP = 128
HGRN_W = 512
HGRN_HEADS = 4
HGRN_DIM = 128
MLA_HEADS = 8
MLA_NOPE = 128
MLA_ROPE = 64
MLA_V = 128
Q_LORA = 512
KV_LORA = 256
MLA_SCALE = (MLA_NOPE + MLA_ROPE) ** -0.5
N_EXPERTS = 32
TOP_K = 4
EXPERT_FF = 768
SWIGLU_ALPHA = 1.702
SWIGLU_LIMIT = 7.0
N_MOD = 6
GRID_W = 64
ROPE_BASE = 10000.0
EPS = 1e-6
LOG2_E = 1.4426950408889634

LANES = 128
TM = 256
TQ = 2 * TM
HEAD_SLOT = 2 * LANES
HEADS_PER_STEP = 2
KEY_BLOCK = 256
IN_W = 4096
HG_OFF = FOURIER_W
MLA_OFF = HG_OFF + 5 * HGRN_W
MLA_IN_W = IN_W - MLA_OFF
N_LEVELS = 9
ADA_TN = 1024
W1_TN = 256
HALF = D_MODEL // 2
NEG_BIG = -1e30
ISSUE_UNROLL = 8
VMEM_LIMIT = 56 * 1024 * 1024


def _cparams(sem, vmem=VMEM_LIMIT):
    return pltpu.CompilerParams(dimension_semantics=sem, vmem_limit_bytes=vmem)


def _dot(a, b):
    return jnp.dot(a, b, preferred_element_type=F32)


def _dot_nt(a, b):
    return lax.dot_general(a, b, (((1,), (1,)), ((), ())), preferred_element_type=F32)


def _dot_tn(a, b):
    return lax.dot_general(a, b, (((0,), (0,)), ((), ())), preferred_element_type=F32)


def _split_dot(x, m):
    hi = x.astype(BF16)
    lo = (x - hi.astype(F32)).astype(BF16)
    return _dot(hi, m) + _dot(lo, m)


def _layer_norm(x):
    mu = jnp.mean(x, axis=-1, keepdims=True)
    xc = x - mu
    var = jnp.mean(xc * xc, axis=-1, keepdims=True)
    return xc * lax.rsqrt(var + EPS)


def _rms(x):
    return x * lax.rsqrt(jnp.mean(x * x, axis=-1, keepdims=True) + EPS)


def _sigmoid(x):
    return 1.0 / (1.0 + jnp.exp(-x))


def _mod_row(i, tpb, nlt, nb):
    return jnp.where(i % tpb >= nlt, nb, i // tpb)


def _mod(mod_ref, row, k):
    return mod_ref[pl.ds(row, 1), k * D_MODEL:(k + 1) * D_MODEL]


def _ada_kernel(c_ref, w_ref, b_ref, o_ref):
    c = c_ref[...]
    s = c * _sigmoid(c)
    hi = s.astype(BF16)
    lo = (s - hi.astype(F32)).astype(BF16)
    w = w_ref[...].astype(BF16)
    o_ref[...] = _dot(hi, w) + _dot(lo, w) + b_ref[...]


def _ada(cc, w_ada, b_ada):
    depth, d, n = w_ada.shape
    rows = cc.shape[0]
    return pl.pallas_call(
        _ada_kernel,
        grid=(depth, n // ADA_TN),
        in_specs=[
            pl.BlockSpec((rows, d), lambda l, j: (0, 0)),
            pl.BlockSpec((None, d, ADA_TN), lambda l, j: (l, 0, j)),
            pl.BlockSpec((None, 1, ADA_TN), lambda l, j: (l, 0, j)),
        ],
        out_specs=pl.BlockSpec((None, rows, ADA_TN), lambda l, j: (l, 0, j)),
        out_shape=jax.ShapeDtypeStruct((depth, rows, n), F32),
        compiler_params=_cparams(("arbitrary", "arbitrary")),
        name="ada",
    )(cc, w_ada, b_ada.reshape(depth, 1, n))


def _pre_kernel(x_ref, mod_ref, w_ref, dft_ref, z_ref, hg_ref, mla_ref, *, tpb, nlt, nb):
    row = _mod_row(pl.program_id(0), tpb, nlt, nb)
    h = _layer_norm(x_ref[...]) * (1.0 + _mod(mod_ref, row, 1)) + _mod(mod_ref, row, 0)
    r = _dot(h.astype(BF16), w_ref[...])
    z_ref[...] = _dot(r[:, 0:FOURIER_W].astype(BF16), dft_ref[...]).astype(BF16)
    hg_ref[...] = r[:, HG_OFF:MLA_OFF]
    mla_ref[...] = r[:, MLA_OFF:IN_W]


def _pre(x, mod, w_in, dft_c, *, tpb, nlt, nb):
    n = x.shape[0]
    kern = functools.partial(_pre_kernel, tpb=tpb, nlt=nlt, nb=nb)
    return pl.pallas_call(
        kern,
        grid=(n // TM,),
        in_specs=[
            pl.BlockSpec((TM, D_MODEL), lambda i: (i, 0)),
            pl.BlockSpec(mod.shape, lambda i: (0, 0)),
            pl.BlockSpec((D_MODEL, IN_W), lambda i: (0, 0), pipeline_mode=pl.Buffered(1)),
            pl.BlockSpec(dft_c.shape, lambda i: (0, 0)),
        ],
        out_specs=[
            pl.BlockSpec((TM, 2 * FOURIER_W), lambda i: (i, 0)),
            pl.BlockSpec((TM, 5 * HGRN_W), lambda i: (i, 0)),
            pl.BlockSpec((TM, MLA_IN_W), lambda i: (i, 0)),
        ],
        out_shape=[
            jax.ShapeDtypeStruct((n, 2 * FOURIER_W), BF16),
            jax.ShapeDtypeStruct((n, 5 * HGRN_W), F32),
            jax.ShapeDtypeStruct((n, MLA_IN_W), F32),
        ],
        compiler_params=_cparams(("arbitrary",)),
        name="pre",
    )(x, mod, w_in, dft_c)


def _four_kernel(z_ref, cl_ref, sl_ref, cc_ref, sc_ref, y_ref, *, nlt, seq):
    j = pl.program_id(1)

    def emit(cm, sm, z):
        y = _dot(cm, z[:, 0:FOURIER_W]) - _dot(sm, z[:, FOURIER_W:])
        y_ref[...] = y.astype(BF16)

    @pl.when(j < nlt)
    def _():
        emit(cl_ref[...], sl_ref[...], z_ref[0:seq, :])

    @pl.when(j >= nlt)
    def _():
        emit(cc_ref[...], sc_ref[...], z_ref[seq:, :])


def _fourier(z, dft_lat, dft_ctx, *, nb, tpb, nlt, seq, ctx_len):
    n = z.shape[0]
    r = seq + ctx_len
    cl, sl = dft_lat
    cc, sc = dft_ctx
    kern = functools.partial(_four_kernel, nlt=nlt, seq=seq)
    lmap = lambda b, j: (jnp.minimum(j, nlt - 1), 0)
    cmap = lambda b, j: (jnp.maximum(j - nlt, 0), 0)
    return pl.pallas_call(
        kern,
        grid=(nb, tpb),
        in_specs=[
            pl.BlockSpec((None, r, 2 * FOURIER_W), lambda b, j: (b, 0, 0)),
            pl.BlockSpec((TM, seq), lmap),
            pl.BlockSpec((TM, seq), lmap),
            pl.BlockSpec((TM, ctx_len), cmap),
            pl.BlockSpec((TM, ctx_len), cmap),
        ],
        out_specs=pl.BlockSpec((TM, FOURIER_W), lambda b, j: (b * tpb + j, 0)),
        out_shape=jax.ShapeDtypeStruct((n, FOURIER_W), BF16),
        compiler_params=_cparams(("arbitrary", "arbitrary")),
        name="fourier",
    )(z.reshape(nb, r, 2 * FOURIER_W), cl, sl, cc, sc)


def _level_matrices(reverse):
    p = np.arange(TM)
    mats = [(p[None, :] <= p[:, None])]
    for lev in range(1, N_LEVELS):
        r = TM >> lev
        same = (p[None, :] // r) == (p[:, None] // r)
        is_q = ((p // r) % 2 == 1)[:, None]
        mats.append(same & np.where(is_q, p[None, :] <= p[:, None], p[None, :] > p[:, None]))
    g = np.stack(mats).astype(np.float32)
    if reverse:
        g = g[:, ::-1, ::-1]
    return jnp.asarray(g.reshape(N_LEVELS * TM, TM), dtype=BF16)


def _hgrn_kernel(q_ref, v_ref, z_ref, lb_ref, g_ref, o_ref, st_ref, *, reverse):
    @pl.when(pl.program_id(1) == 0)
    def _():
        st_ref[...] = jnp.zeros_like(st_ref)

    z = z_ref[...]
    log_lb = lb_ref[0:1, :]
    log_1m_lb = lb_ref[1:2, :]
    one_m_lb = lb_ref[2:3, :]
    log_sig = jnp.minimum(z, 0.0) - jnp.log1p(jnp.exp(-jnp.abs(z)))
    a2 = log_1m_lb + log_sig
    log_f = jnp.maximum(log_lb, a2) + jnp.log1p(jnp.exp(-jnp.abs(log_lb - a2)))
    kk = one_m_lb * _sigmoid(-z)

    wall = _split_dot_left(g_ref[...], log_f)
    a = wall[0:TM]
    a_end = a[0:1] if reverse else a[TM - 1:TM]
    q = q_ref[...]
    v = v_ref[...].astype(BF16)
    qa = (q * jnp.exp(a)).astype(BF16)
    kd = (kk * jnp.exp(a_end - a)).astype(BF16)
    carry = jnp.exp(a_end)

    row = lax.broadcasted_iota(I32, (TM, TM), 0)
    col = lax.broadcasted_iota(I32, (TM, TM), 1)
    diff = row ^ col
    pos = lax.broadcasted_iota(I32, (TM, HGRN_DIM), 0)
    if reverse:
        pos = TM - 1 - pos

    for h in range(HGRN_HEADS):
        sl = slice(h * HGRN_DIM, (h + 1) * HGRN_DIM)
        st = st_ref[h]
        qh, kh, vh = q[:, sl], kk[:, sl], v[:, sl]
        p = jnp.where(diff == 0, _dot_nt(qh.astype(BF16), kh.astype(BF16)), 0.0)
        for lev in range(1, N_LEVELS):
            shift = (TM >> lev).bit_length() - 1
            fac = jnp.exp(wall[lev * TM:(lev + 1) * TM, sl])
            is_q = ((pos >> shift) & 1) == 1
            qt = jnp.where(is_q, qh * fac, 0.0).astype(BF16)
            kt = jnp.where(is_q, 0.0, kh * fac).astype(BF16)
            p = p + jnp.where((diff >> (shift + 1)) == 0, _dot_nt(qt, kt), 0.0)
        o_ref[:, sl] = _dot_nt(qa[:, sl], st.astype(BF16)) + _dot(p.astype(BF16), vh)
        st_ref[h] = st * carry[:, sl] + _dot_tn(vh, kd[:, sl])


def _split_dot_left(m, x):
    hi = x.astype(BF16)
    lo = (x - hi.astype(F32)).astype(BF16)
    return _dot(m, hi) + _dot(m, lo)


def _hgrn(hg, lb_rows, *, z_block, reverse, nb, tpb, nlt):
    n = hg.shape[0]
    nct = tpb - nlt

    def chunk(b, s):
        if reverse:
            j = jnp.where(s < nct, tpb - 1 - s, nlt - 1 - (s - nct))
        else:
            j = jnp.where(s < nct, nlt + s, s - nct)
        return b * tpb + j

    kern = functools.partial(_hgrn_kernel, reverse=reverse)
    gmat = _level_matrices(reverse)
    return pl.pallas_call(
        kern,
        grid=(nb, tpb),
        in_specs=[
            pl.BlockSpec((TM, HGRN_W), lambda b, s: (chunk(b, s), 0)),
            pl.BlockSpec((TM, HGRN_W), lambda b, s: (chunk(b, s), 1)),
            pl.BlockSpec((TM, HGRN_W), lambda b, s: (chunk(b, s), z_block)),
            pl.BlockSpec((8, HGRN_W), lambda b, s: (0, 0)),
            pl.BlockSpec(gmat.shape, lambda b, s: (0, 0)),
        ],
        out_specs=pl.BlockSpec((TM, HGRN_W), lambda b, s: (chunk(b, s), 0)),
        out_shape=jax.ShapeDtypeStruct((n, HGRN_W), F32),
        scratch_shapes=[pltpu.VMEM((HGRN_HEADS, HGRN_DIM, HGRN_DIM), F32)],
        compiler_params=_cparams(("arbitrary", "arbitrary")),
        name="hgrn_bwd" if reverse else "hgrn_fwd",
    )(hg, hg, hg, lb_rows, gmat)


def _mla_kernel(m_ref, qg_ref, kvg_ref, wq_ref, wkv_ref, cos_ref, sin_ref, q_ref, k_ref, v_ref):
    cq = m_ref[:, 0:Q_LORA]
    ckv = m_ref[:, Q_LORA:Q_LORA + KV_LORA]
    kr = m_ref[:, Q_LORA + KV_LORA:Q_LORA + KV_LORA + LANES]
    kr_sw = m_ref[:, Q_LORA + KV_LORA + LANES:MLA_IN_W]
    cos = cos_ref[...]
    sin = sin_ref[...]
    qall = _dot((_rms(cq) * qg_ref[...]).astype(BF16), wq_ref[...]) * (MLA_SCALE * LOG2_E)
    kv = _dot((_rms(ckv) * kvg_ref[...]).astype(BF16), wkv_ref[...])
    k_rot = (kr * cos + kr_sw * sin).astype(BF16)
    ones = jnp.ones((TM, LANES), BF16)
    nw = MLA_HEADS * LANES
    for h in range(MLA_HEADS):
        a, b = h * LANES, (h + 1) * LANES
        s = h * HEAD_SLOT
        q_ref[:, s:s + LANES] = qall[:, a:b].astype(BF16)
        q_ref[:, s + LANES:s + HEAD_SLOT] = (qall[:, nw + a:nw + b] * cos + qall[:, 2 * nw + a:2 * nw + b] * sin).astype(BF16)
        k_ref[:, s:s + LANES] = kv[:, a:b].astype(BF16)
        k_ref[:, s + LANES:s + HEAD_SLOT] = k_rot
        v_ref[:, s:s + LANES] = kv[:, nw + a:nw + b].astype(BF16)
        v_ref[:, s + LANES:s + HEAD_SLOT] = ones


def _mla_up(mla_in, q_gain, kv_gain, wq, wkv, cos_t, sin_t):
    n = mla_in.shape[0]
    w = MLA_HEADS * HEAD_SLOT
    full = lambda a: pl.BlockSpec(a.shape, lambda i: (0, 0))
    return pl.pallas_call(
        _mla_kernel,
        grid=(n // TM,),
        in_specs=[
            pl.BlockSpec((TM, MLA_IN_W), lambda i: (i, 0)),
            full(q_gain), full(kv_gain), full(wq), full(wkv),
            pl.BlockSpec((TM, LANES), lambda i: (i, 0)),
            pl.BlockSpec((TM, LANES), lambda i: (i, 0)),
        ],
        out_specs=[pl.BlockSpec((TM, w), lambda i: (i, 0))] * 3,
        out_shape=[jax.ShapeDtypeStruct((n, w), BF16)] * 3,
        compiler_params=_cparams(("arbitrary",)),
        name="mla_up",
    )(mla_in, q_gain, kv_gain, wq, wkv, cos_t, sin_t)


def _attn_kernel(q_ref, k_ref, v_ref, o_ref, s_ref, *, n_lat_steps, seq, ctx_len):
    j = pl.program_id(2)

    def run(rows, k0, k1):
        blocks = [(kb, slice(k0 + kb * KEY_BLOCK, k0 + (kb + 1) * KEY_BLOCK), slice(kb * KEY_BLOCK, (kb + 1) * KEY_BLOCK))
                  for kb in range((k1 - k0) // KEY_BLOCK)]
        for hh in range(HEADS_PER_STEP):
            hs = slice(hh * HEAD_SLOT, (hh + 1) * HEAD_SLOT)
            q = q_ref[0:rows, hs]
            m_run = jnp.full((rows, LANES), -jnp.inf, F32)
            for kb, ks, ss in blocks:
                s = _dot_nt(q, k_ref[ks, hs])
                s_ref[0:rows, ss] = s
                for c in range(KEY_BLOCK // LANES):
                    m_run = jnp.maximum(m_run, s[:, c * LANES:(c + 1) * LANES])
            m = jnp.max(m_run, axis=-1, keepdims=True)
            o = jnp.zeros((rows, HEAD_SLOT), F32)
            for kb, ks, ss in blocks:
                d = (s_ref[0:rows, ss] - m).astype(BF16)
                o = o + _dot(jnp.exp2(d), v_ref[ks, hs])
            o_ref[0:rows, hh * MLA_V:(hh + 1) * MLA_V] = (o[:, 0:MLA_V] / o[:, MLA_V:]).astype(BF16)

    @pl.when(j < n_lat_steps)
    def _():
        run(TQ, 0, seq + ctx_len)

    @pl.when(j >= n_lat_steps)
    def _():
        run(ctx_len, seq, seq + ctx_len)


def _attention(q, k, v, *, nb, seq, ctx_len):
    n = q.shape[0]
    r = seq + ctx_len
    assert seq % TQ == 0 and ctx_len <= TQ
    n_lat_steps = seq // TQ
    hp = MLA_HEADS // HEADS_PER_STEP
    wq = HEADS_PER_STEP * HEAD_SLOT
    wo = HEADS_PER_STEP * MLA_V
    kern = functools.partial(_attn_kernel, n_lat_steps=n_lat_steps, seq=seq, ctx_len=ctx_len)
    out = pl.pallas_call(
        kern,
        grid=(nb, hp, n_lat_steps + 1),
        in_specs=[
            pl.BlockSpec((None, TQ, wq), lambda b, h, j: (b, j, h)),
            pl.BlockSpec((None, r, wq), lambda b, h, j: (b, 0, h)),
            pl.BlockSpec((None, r, wq), lambda b, h, j: (b, 0, h)),
        ],
        out_specs=pl.BlockSpec((None, TQ, wo), lambda b, h, j: (b, j, h)),
        out_shape=jax.ShapeDtypeStruct((nb, r, MLA_HEADS * MLA_V), BF16),
        scratch_shapes=[pltpu.VMEM((TQ, r), F32)],
        compiler_params=_cparams(("arbitrary", "arbitrary", "arbitrary")),
        name="attention",
    )(q.reshape(nb, r, -1), k.reshape(nb, r, -1), v.reshape(nb, r, -1))
    return out.reshape(n, MLA_HEADS * MLA_V)


def _pack_rows(x):
    w = x.shape[1] // 2
    lo = pltpu.bitcast(x[:, :w].astype(BF16).astype(F32), U32)
    hi = pltpu.bitcast(x[:, w:].astype(BF16).astype(F32), U32)
    return (lo >> 16) | (hi & jnp.uint32(0xFFFF0000))


def _unpack_rows(w):
    lo = pltpu.bitcast(w << 16, F32)
    hi = pltpu.bitcast(w & jnp.uint32(0xFFFF0000), F32)
    return lo, hi


def _post_kernel(yf_ref, of_ref, ob_ref, g_ref, att_ref, x_ref, mod_ref, hn_ref, wo_ref, lng_ref, lnb_ref,
                 rwh_ref, rwl_ref, rb_ref, x1_ref, h2_ref, eid_ref, gate_ref, *, tpb, nlt, nb, alpha):
    row = _mod_row(pl.program_id(0), tpb, nlt, nb)
    o = of_ref[...] + ob_ref[...]
    hn = hn_ref[...]
    gate_act = g_ref[...]
    gate_act = gate_act * _sigmoid(gate_act)
    mix = _dot(yf_ref[...], wo_ref[0:FOURIER_W, :])
    for h in range(HGRN_HEADS):
        sl = slice(h * HGRN_DIM, (h + 1) * HGRN_DIM)
        r = (_rms(o[:, sl]) * hn[:, sl] * gate_act[:, sl]).astype(BF16)
        mix = mix + _dot(r, wo_ref[FOURIER_W + h * HGRN_DIM:FOURIER_W + (h + 1) * HGRN_DIM, :])
    mix = mix + _dot(att_ref[...], wo_ref[FOURIER_W + HGRN_W:, :])

    x1 = _layer_norm(alpha * x_ref[...] + _mod(mod_ref, row, 2) * mix) * lng_ref[...] + lnb_ref[...]
    x1_ref[...] = x1
    h2 = _layer_norm(x1) * (1.0 + _mod(mod_ref, row, 4)) + _mod(mod_ref, row, 3)
    h2_ref[...] = _pack_rows(h2)

    hi = h2.astype(BF16)
    lo = (h2 - hi.astype(F32)).astype(BF16)
    logits = _dot(hi, rwh_ref[...]) + _dot(lo, rwh_ref[...]) + _dot(hi, rwl_ref[...]) + rb_ref[...]
    lane = lax.broadcasted_iota(I32, logits.shape, 1)
    cur = logits
    vals, ids = [], []
    for _ in range(TOP_K):
        m = jnp.max(cur, axis=-1, keepdims=True)
        idx = jnp.min(jnp.where(cur == m, lane, LANES), axis=-1, keepdims=True)
        vals.append(m)
        ids.append(idx)
        cur = jnp.where(lane == idx, -3e38, cur)
    ex = [jnp.exp(vk - vals[0]) for vk in vals]
    den = ex[0] + ex[1] + ex[2] + ex[3]
    eid = jnp.zeros(logits.shape, I32)
    gate = jnp.zeros(logits.shape, F32)
    for k in range(TOP_K):
        eid = jnp.where(lane == k, ids[k], eid)
        gate = jnp.where(lane == k, ex[k] / den, gate)
    eid_ref[...] = eid
    gate_ref[...] = gate


def _post(yf, o_f, o_b, hg, att, x, mod, hnorm, w_out, ln_g, ln_b, rw_hi, rw_lo, rb, *, tpb, nlt, nb, alpha):
    n = x.shape[0]
    kern = functools.partial(_post_kernel, tpb=tpb, nlt=nlt, nb=nb, alpha=alpha)
    full = lambda a: pl.BlockSpec(a.shape, lambda i: (0, 0))
    return pl.pallas_call(
        kern,
        grid=(n // TM,),
        in_specs=[
            pl.BlockSpec((TM, FOURIER_W), lambda i: (i, 0)),
            pl.BlockSpec((TM, HGRN_W), lambda i: (i, 0)),
            pl.BlockSpec((TM, HGRN_W), lambda i: (i, 0)),
            pl.BlockSpec((TM, HGRN_W), lambda i: (i, 2)),
            pl.BlockSpec((TM, MLA_HEADS * MLA_V), lambda i: (i, 0)),
            pl.BlockSpec((TM, D_MODEL), lambda i: (i, 0)),
            full(mod), full(hnorm),
            pl.BlockSpec((D_MODEL, D_MODEL), lambda i: (0, 0), pipeline_mode=pl.Buffered(1)),
            full(ln_g), full(ln_b), full(rw_hi), full(rw_lo), full(rb),
        ],
        out_specs=[
            pl.BlockSpec((TM, D_MODEL), lambda i: (i, 0)),
            pl.BlockSpec((TM, HALF), lambda i: (i, 0)),
            pl.BlockSpec((TM, LANES), lambda i: (i, 0)),
            pl.BlockSpec((TM, LANES), lambda i: (i, 0)),
        ],
        out_shape=[
            jax.ShapeDtypeStruct((n, D_MODEL), F32),
            jax.ShapeDtypeStruct((n, HALF), U32),
            jax.ShapeDtypeStruct((n, LANES), I32),
            jax.ShapeDtypeStruct((n, LANES), F32),
        ],
        compiler_params=_cparams(("arbitrary",)),
        name="post",
    )(yf, o_f, o_b, hg, att, x, mod, hnorm, w_out, ln_g, ln_b, rw_hi, rw_lo, rb)


def _route_kernel(eid_ref, pos_ref, te_ref, meta_ref, base_ref, cnt_ref, *, n_tiles_pad):
    ph = pl.program_id(0)
    i = pl.program_id(1)
    lane = lax.broadcasted_iota(I32, (TM, LANES), 1)
    eid = eid_ref[...]
    onehot = [(lane == eid[:, k:k + 1]).astype(F32) for k in range(TOP_K)]
    colsum = [jnp.sum(oh, axis=0, keepdims=True) for oh in onehot]

    @pl.when(ph == 0)
    def _():
        @pl.when(i == 0)
        def _():
            cnt_ref[...] = jnp.zeros_like(cnt_ref)

        base_ref[pl.ds(i, 1), :] = cnt_ref[0:1, :]
        cnt_ref[0:1, :] = cnt_ref[0:1, :] + colsum[0] + colsum[1] + colsum[2] + colsum[3]

    @pl.when(ph == 1)
    def _():
        cnt = cnt_ref[0:1, :]
        tiles = jnp.floor((cnt + (TM - 1)) * (1.0 / TM))
        r128 = lax.broadcasted_iota(I32, (LANES, LANES), 0)
        c128 = lax.broadcasted_iota(I32, (LANES, LANES), 1)
        incl = (r128 <= c128).astype(BF16)
        pad_end = _split_dot(jnp.broadcast_to(tiles, (8, LANES)), incl)[0:1, :] * TM
        pad_start = pad_end - tiles * TM

        rr = lax.broadcasted_iota(I32, (TM, TM), 0)
        cc = lax.broadcasted_iota(I32, (TM, TM), 1)
        strict = (cc < rr).astype(BF16)
        run = pad_start + base_ref[pl.ds(i, 1), :]
        pos = jnp.zeros((TM, LANES), I32)
        for k in range(TOP_K):
            before = _dot(strict, onehot[k].astype(BF16))
            val = jnp.sum(onehot[k] * (before + run), axis=1, keepdims=True)
            pos = jnp.where(lane == k, val.astype(I32), pos)
            run = run + colsum[k]
        pos_ref[...] = pos

        @pl.when(i == 0)
        def _():
            tstart = (lax.broadcasted_iota(I32, (n_tiles_pad, LANES), 0) * TM).astype(F32)
            lane_t = lax.broadcasted_iota(I32, (n_tiles_pad, LANES), 1)
            done = jnp.where((pad_end <= tstart) & (lane_t < N_EXPERTS), 1.0, 0.0)
            te = jnp.minimum(jnp.sum(done, axis=1, keepdims=True), N_EXPERTS - 1.0)
            te_ref[...] = jnp.broadcast_to(te, (n_tiles_pad, LANES)).astype(I32)
            sub = lax.broadcasted_iota(I32, (8, LANES), 0)
            meta = jnp.where(sub == 0, pad_start + cnt, jnp.where(sub == 1, pad_end - pad_start - cnt, pad_end))
            meta_ref[...] = meta.astype(I32)


def _route(eid, n_tiles):
    n = eid.shape[0]
    nt = n // TM
    n_tiles_pad = -(-n_tiles // 8) * 8
    kern = functools.partial(_route_kernel, n_tiles_pad=n_tiles_pad)
    pos, te, meta = pl.pallas_call(
        kern,
        grid=(2, nt),
        in_specs=[pl.BlockSpec((TM, LANES), lambda ph, i: (i, 0))],
        out_specs=[
            pl.BlockSpec((TM, LANES), lambda ph, i: (ph * i, 0)),
            pl.BlockSpec((n_tiles_pad, LANES), lambda ph, i: (0, 0)),
            pl.BlockSpec((8, LANES), lambda ph, i: (0, 0)),
        ],
        out_shape=[
            jax.ShapeDtypeStruct((n, LANES), I32),
            jax.ShapeDtypeStruct((n_tiles_pad, LANES), I32),
            jax.ShapeDtypeStruct((8, LANES), I32),
        ],
        scratch_shapes=[pltpu.VMEM((-(-nt // 8) * 8, LANES), F32), pltpu.VMEM((8, LANES), F32)],
        compiler_params=_cparams(("arbitrary", "arbitrary")),
        name="moe_route",
    )(eid)
    pos_flat = pos[:, :TOP_K].reshape(-1)
    tile_e = te[:n_tiles, 0]
    fill_start = meta[0, :N_EXPERTS]
    fill_n = meta[1, :N_EXPERTS]
    n_used = meta[2, N_EXPERTS - 1:N_EXPERTS] // TM
    return pos_flat, tile_e, n_used, fill_start, fill_n


def _dispatch_kernel(pos_ref, fs_ref, fn_ref, nu_ref, h_ref, o_hbm, zero_ref, sem, fill_sem, *, n_tiles):
    i = pl.program_id(0)

    def start(r, c):
        for k in range(TOP_K):
            p = pos_ref[(i * TM + r) * TOP_K + k]
            pltpu.make_async_copy(h_ref.at[pl.ds(r, 1), :], o_hbm.at[pl.ds(p, 1), :], sem).start()
        return c

    lax.fori_loop(0, TM, start, 0, unroll=ISSUE_UNROLL)

    @pl.when(i == 0)
    def _():
        zero_ref[...] = jnp.zeros_like(zero_ref)

        def fill_row(row):
            return pltpu.make_async_copy(zero_ref.at[pl.ds(0, 1), :], o_hbm.at[pl.ds(row, 1), :], fill_sem)

        def fill_tile(t):
            return pltpu.make_async_copy(zero_ref, o_hbm.at[pl.ds(pl.multiple_of(t * TM, TM), TM), :], fill_sem)

        def each(fn):
            for e in range(N_EXPERTS):
                lax.fori_loop(0, fn_ref[e], lambda t, c, e=e: (fn(fill_row(fs_ref[e] + t)), c)[1], 0)
            lax.fori_loop(nu_ref[0], n_tiles, lambda t, c: (fn(fill_tile(t)), c)[1], 0)

        each(lambda cp: cp.start())
        each(lambda cp: cp.wait())

    for _ in range(TOP_K):
        pltpu.make_async_copy(h_ref, o_hbm.at[pl.ds(0, TM), :], sem).wait()


def _dispatch(pos_flat, fill_start, fill_n, n_used, h2p, n_tiles):
    n = h2p.shape[0]
    return pl.pallas_call(
        functools.partial(_dispatch_kernel, n_tiles=n_tiles),
        grid_spec=pltpu.PrefetchScalarGridSpec(
            num_scalar_prefetch=4,
            grid=(n // TM,),
            in_specs=[pl.BlockSpec((TM, HALF), lambda i, p, fs, fn, nu: (i, 0))],
            out_specs=pl.BlockSpec(memory_space=pl.ANY),
            scratch_shapes=[pltpu.VMEM((TM, HALF), U32), pltpu.SemaphoreType.DMA(()), pltpu.SemaphoreType.DMA(())],
        ),
        out_shape=jax.ShapeDtypeStruct((n_tiles * TM, HALF), U32),
        compiler_params=_cparams(("arbitrary",)),
        name="moe_dispatch",
    )(pos_flat, fill_start, fill_n, n_used, h2p)


def _w1_kernel(w_ref, g_ref, l_ref, t_ref):
    for c in range(D_MODEL // LANES):
        cs = slice(c * LANES, (c + 1) * LANES)
        t_ref[c] = w_ref[cs, :].T
        g_ref[:, cs] = t_ref[c, pl.ds(0, W1_TN // 2, stride=2), :].astype(BF16)
        l_ref[:, cs] = t_ref[c, pl.ds(1, W1_TN // 2, stride=2), :].astype(BF16)


def _prep_w1(w1):
    depth, ne, d, n2 = w1.shape
    out = jax.ShapeDtypeStruct((depth * ne, n2 // 2, d), BF16)
    g, l = pl.pallas_call(
        _w1_kernel,
        grid=(depth * ne, n2 // W1_TN),
        in_specs=[pl.BlockSpec((None, d, W1_TN), lambda e, j: (e, 0, j))],
        out_specs=[pl.BlockSpec((None, W1_TN // 2, d), lambda e, j: (e, j, 0))] * 2,
        out_shape=[out, out],
        scratch_shapes=[pltpu.VMEM((d // LANES, W1_TN, LANES), F32)],
        compiler_params=_cparams(("arbitrary", "arbitrary")),
        name="moe_w1_prep",
    )(w1.reshape(depth * ne, d, n2))
    return g.reshape(depth, ne, n2 // 2, d), l.reshape(depth, ne, n2 // 2, d)


def _moe_kernel(te_ref, nu_ref, x_ref, w1g_ref, w1l_ref, b1g_ref, b1l_ref, w2_ref, b2_ref, y_ref, w2b_ref):
    i = pl.program_id(0)

    @pl.when(i < nu_ref[0])
    def _():
        @pl.when((i == 0) | (te_ref[i] != te_ref[jnp.maximum(i - 1, 0)]))
        def _():
            w2b_ref[...] = w2_ref[...].astype(BF16)

        lo, hi = _unpack_rows(x_ref[...])
        lo = lo.astype(BF16)
        hi = hi.astype(BF16)
        ug = _dot_nt(lo, w1g_ref[:, 0:HALF]) + _dot_nt(hi, w1g_ref[:, HALF:]) + b1g_ref[...]
        ul = _dot_nt(lo, w1l_ref[:, 0:HALF]) + _dot_nt(hi, w1l_ref[:, HALF:]) + b1l_ref[...]
        xg = jnp.minimum(ug, SWIGLU_LIMIT)
        xl = jnp.clip(ul, -SWIGLU_LIMIT, SWIGLU_LIMIT)
        act = xg * _sigmoid(SWIGLU_ALPHA * xg) * (xl + 1.0)
        y = _dot(act.astype(BF16), w2b_ref[...]) + b2_ref[...]
        y_ref[...] = _pack_rows(y)

    @pl.when(i >= nu_ref[0])
    def _():
        y_ref[...] = jnp.zeros_like(y_ref)


def _moe_ffn(tile_e, n_used, xs, w1g, w1l, b1g, b1l, w2, b2):
    m = xs.shape[0]
    emap = lambda i, te, nu: (te[i], 0, 0)
    return pl.pallas_call(
        _moe_kernel,
        grid_spec=pltpu.PrefetchScalarGridSpec(
            num_scalar_prefetch=2,
            grid=(m // TM,),
            in_specs=[
                pl.BlockSpec((TM, HALF), lambda i, te, nu: (i, 0)),
                pl.BlockSpec((None, EXPERT_FF, D_MODEL), emap),
                pl.BlockSpec((None, EXPERT_FF, D_MODEL), emap),
                pl.BlockSpec((None, 1, EXPERT_FF), emap),
                pl.BlockSpec((None, 1, EXPERT_FF), emap),
                pl.BlockSpec((None, EXPERT_FF, D_MODEL), emap),
                pl.BlockSpec((None, 1, D_MODEL), emap),
            ],
            out_specs=pl.BlockSpec((TM, HALF), lambda i, te, nu: (i, 0)),
            scratch_shapes=[pltpu.VMEM((EXPERT_FF, D_MODEL), BF16)],
        ),
        out_shape=jax.ShapeDtypeStruct((m, HALF), U32),
        compiler_params=_cparams(("arbitrary",)),
        name="moe_ffn",
    )(tile_e, n_used, xs, w1g, w1l, b1g, b1l, w2, b2)


def _combine_kernel(pos_ref, y_hbm, gate_ref, x_ref, mod_ref, lng_ref, lnb_ref, o_ref, buf, sem, *, tpb, nlt, nb, alpha):
    i = pl.program_id(0)
    n_steps = pl.num_programs(0)
    row = _mod_row(i, tpb, nlt, nb)

    def issue(tile, slot):
        def start(r, c):
            for k in range(TOP_K):
                p = pos_ref[(tile * TM + r) * TOP_K + k]
                pltpu.make_async_copy(y_hbm.at[pl.ds(p, 1), :], buf.at[slot, k, pl.ds(r, 1), :], sem.at[slot]).start()
            return c

        lax.fori_loop(0, TM, start, 0, unroll=ISSUE_UNROLL)

    @pl.when(i == 0)
    def _():
        issue(0, 0)

    @pl.when(i + 1 < n_steps)
    def _():
        issue(i + 1, (i + 1) % 2)

    slot = i % 2
    for k in range(TOP_K):
        pltpu.make_async_copy(y_hbm.at[pl.ds(0, TM), :], buf.at[slot, k], sem.at[slot]).wait()

    gates = gate_ref[...]
    acc_lo = jnp.zeros((TM, HALF), F32)
    acc_hi = jnp.zeros((TM, HALF), F32)
    for k in range(TOP_K):
        lo, hi = _unpack_rows(buf[slot, k])
        gk = gates[:, k:k + 1]
        acc_lo = acc_lo + gk * lo
        acc_hi = acc_hi + gk * hi
    ffn = jnp.concatenate([acc_lo, acc_hi], axis=-1)
    x2 = _layer_norm(alpha * x_ref[...] + _mod(mod_ref, row, 5) * ffn) * lng_ref[...] + lnb_ref[...]
    o_ref[...] = x2


def _combine(pos, ys, gates, x, mod, ln_g, ln_b, *, tpb, nlt, nb, alpha):
    n = x.shape[0]
    kern = functools.partial(_combine_kernel, tpb=tpb, nlt=nlt, nb=nb, alpha=alpha)
    full = lambda a: pl.BlockSpec(a.shape, lambda i, p: (0, 0))
    return pl.pallas_call(
        kern,
        grid_spec=pltpu.PrefetchScalarGridSpec(
            num_scalar_prefetch=1,
            grid=(n // TM,),
            in_specs=[
                pl.BlockSpec(memory_space=pl.ANY),
                pl.BlockSpec((TM, LANES), lambda i, p: (i, 0)),
                pl.BlockSpec((TM, D_MODEL), lambda i, p: (i, 0)),
                full(mod), full(ln_g), full(ln_b),
            ],
            out_specs=pl.BlockSpec((TM, D_MODEL), lambda i, p: (i, 0)),
            scratch_shapes=[pltpu.VMEM((2, TOP_K, TM, HALF), U32), pltpu.SemaphoreType.DMA((2,))],
        ),
        out_shape=jax.ShapeDtypeStruct((n, D_MODEL), F32),
        compiler_params=_cparams(("arbitrary",)),
        name="moe_combine",
    )(pos, ys, gates, x, mod, ln_g, ln_b)


def _dft_pair(t_len):
    t = jnp.arange(t_len, dtype=I32)
    ang = ((t[:, None] * t[None, :]) % t_len).astype(F32) * (2.0 * np.pi / t_len)
    s = 1.0 / np.sqrt(t_len)
    return (jnp.cos(ang) * s).astype(BF16), (jnp.sin(ang) * s).astype(BF16)


def _channel_dft():
    c = np.arange(FOURIER_GROUP)
    ang = 2.0 * np.pi * ((c[:, None] * c[None, :]) % FOURIER_GROUP) / FOURIER_GROUP
    s = 1.0 / np.sqrt(FOURIER_GROUP)
    groups = FOURIER_W // FOURIER_GROUP
    m = np.zeros((FOURIER_W, 2 * FOURIER_W), np.float32)
    for g in range(groups):
        a, b = g * FOURIER_GROUP, (g + 1) * FOURIER_GROUP
        m[a:b, a:b] = np.cos(ang) * s
        m[a:b, FOURIER_W + a:FOURIER_W + b] = np.sin(ang) * s
    return jnp.asarray(m, dtype=BF16)


def _pair_swap(w):
    pairs = w.reshape(*w.shape[:-1], w.shape[-1] // 2, 2)
    return jnp.stack([-pairs[..., 1], pairs[..., 0]], axis=-1).reshape(w.shape)


def _pad_lanes(w, width):
    return jnp.pad(w, [(0, 0)] * (w.ndim - 1) + [(0, width - w.shape[-1])])


def _rope_tables(nb, seq, ctx_len):
    t = jnp.arange(seq)
    n_freq = MLA_ROPE // 4
    inv = ROPE_BASE ** (-jnp.arange(n_freq, dtype=F32) / n_freq)
    ang = jnp.concatenate([(t // GRID_W)[:, None] * inv, (t % GRID_W)[:, None] * inv], axis=-1)
    ang = jnp.repeat(ang, 2, axis=-1)
    cos = jnp.concatenate([jnp.cos(ang), jnp.ones((ctx_len, MLA_ROPE), F32)], axis=0)
    sin = jnp.concatenate([jnp.sin(ang), jnp.zeros((ctx_len, MLA_ROPE), F32)], axis=0)
    cos = jnp.tile(_pad_lanes(cos, LANES), (nb, 1))
    sin = jnp.tile(_pad_lanes(sin, LANES), (nb, 1))
    return cos, sin


def _prep_in_proj(w_in):
    f, q, i, g, zf, zb, cq, ckv, kr = jnp.split(w_in, np.cumsum([512, 512, 512, 512, 512, 512, 512, 256]).tolist(), axis=-1)
    cols = [f, q, i, g, zf, zb, cq, ckv, _pad_lanes(kr, LANES), _pad_lanes(_pair_swap(kr), LANES)]
    return jnp.concatenate(cols, axis=-1).astype(BF16)


def _prep_wq(w_uq):
    w = w_uq.reshape(Q_LORA, MLA_HEADS, MLA_NOPE + MLA_ROPE)
    nope = w[:, :, :MLA_NOPE].reshape(Q_LORA, -1)
    rope = w[:, :, MLA_NOPE:]
    rope_p = _pad_lanes(rope, LANES).reshape(Q_LORA, -1)
    rope_s = _pad_lanes(_pair_swap(rope), LANES).reshape(Q_LORA, -1)
    return jnp.concatenate([nope, rope_p, rope_s], axis=-1).astype(BF16)


def _prep_wkv(w_ukv):
    w = w_ukv.reshape(KV_LORA, MLA_HEADS, MLA_NOPE + MLA_V)
    return jnp.concatenate([w[:, :, :MLA_NOPE].reshape(KV_LORA, -1), w[:, :, MLA_NOPE:].reshape(KV_LORA, -1)], axis=-1).astype(BF16)


def kernel(x, c, ctx, c_ctx, w_ada, b_ada, w_in, mla_q_norm, w_uq, mla_kv_norm, w_ukv, hgrn_lb_logits, hgrn_norm,
           w_out, ln1_g, ln1_b, router_w, router_b, w1, b1, w2, b2, ln2_g, ln2_b):
    nb, seq, d = x.shape
    ctx_len = ctx.shape[1]
    depth = w_ada.shape[0]
    assert d == D_MODEL and seq % TM == 0 and ctx_len % TM == 0 and seq % GRID_W == 0
    r = seq + ctx_len
    tpb, nlt = r // TM, seq // TM
    n = nb * r
    n_tiles = n * TOP_K // TM + N_EXPERTS
    alpha = float((2 * depth) ** 0.25)
    geo = dict(tpb=tpb, nlt=nlt, nb=nb)

    xs = jnp.concatenate([x, ctx], axis=1).reshape(n, d)

    mod_rows = -(-(nb + 1) // 8) * 8
    cc = jnp.zeros((mod_rows, d), F32).at[:nb].set(c).at[nb].set(c_ctx)
    mod_all = _ada(cc, w_ada, b_ada)

    lb = jnp.cumsum(jax.nn.softmax(hgrn_lb_logits.astype(F32), axis=0), axis=0)
    lb = lb - lb[:1]
    lb_rows = jnp.stack([jnp.log(lb), jnp.log1p(-lb), 1.0 - lb] + [jnp.zeros_like(lb)] * 5, axis=2)

    dft_c = _channel_dft()
    dft_lat = _dft_pair(seq)
    dft_ctx = _dft_pair(ctx_len)
    cos_t, sin_t = _rope_tables(nb, seq, ctx_len)
    w1g_all, w1l_all = _prep_w1(w1)

    for l in range(depth):
        mod = mod_all[l]
        z, hg, mla_in = _pre(xs, mod, _prep_in_proj(w_in[l]), dft_c, **geo)
        yf = _fourier(z, dft_lat, dft_ctx, nb=nb, tpb=tpb, nlt=nlt, seq=seq, ctx_len=ctx_len)
        o_f = _hgrn(hg, lb_rows[l, 0], z_block=3, reverse=False, **geo)
        o_b = _hgrn(hg, lb_rows[l, 1], z_block=4, reverse=True, **geo)
        q, k, v = _mla_up(mla_in, mla_q_norm[l].reshape(1, -1), mla_kv_norm[l].reshape(1, -1),
                          _prep_wq(w_uq[l]), _prep_wkv(w_ukv[l]), cos_t, sin_t)
        att = _attention(q, k, v, nb=nb, seq=seq, ctx_len=ctx_len)

        rw = _pad_lanes(router_w[l], LANES)
        rw_hi = rw.astype(BF16)
        rw_lo = (rw - rw_hi.astype(F32)).astype(BF16)
        rb = jnp.full((1, LANES), NEG_BIG, F32).at[0, :N_EXPERTS].set(router_b[l])
        x1, h2p, eid, gates = _post(yf, o_f, o_b, hg, att, xs, mod, hgrn_norm[l].reshape(1, -1), w_out[l].astype(BF16),
                                    ln1_g[l].reshape(1, -1), ln1_b[l].reshape(1, -1), rw_hi, rw_lo, rb, alpha=alpha, **geo)

        pos, tile_e, n_used, fill_start, fill_n = _route(eid, n_tiles)
        xg = _dispatch(pos, fill_start, fill_n, n_used, h2p, n_tiles)
        b1g = b1[l, :, 0::2].reshape(N_EXPERTS, 1, EXPERT_FF)
        b1l = b1[l, :, 1::2].reshape(N_EXPERTS, 1, EXPERT_FF)
        ys = _moe_ffn(tile_e, n_used, xg, w1g_all[l], w1l_all[l], b1g, b1l, w2[l], b2[l].reshape(N_EXPERTS, 1, d))
        xs = _combine(pos, ys, gates, x1, mod, ln2_g[l].reshape(1, -1), ln2_b[l].reshape(1, -1), alpha=alpha, **geo)

    return xs.reshape(nb, r, d)[:, :seq, :]
```

```python
import functools

import numpy as np
import jax
import jax.numpy as jnp
from jax import lax
from jax.experimental import pallas as pl
from jax.experimental.pallas import tpu as pltpu

F32 = jnp.float32
BF16 = jnp.bfloat16
U32 = jnp.uint32
I32 = jnp.int32

D_MODEL = 2048
FOURIER_W = 512
FOURIER_GROUP = 128
HGRN_W = 512
HGRN_HEADS = 4
HGRN_DIM = 128
MLA_HEADS = 8
MLA_NOPE = 128
MLA_ROPE = 64
MLA_V = 128
Q_LORA = 512
KV_LORA = 256
MLA_SCALE = (MLA_NOPE + MLA_ROPE) ** -0.5
N_EXPERTS = 32
TOP_K = 4
EXPERT_FF = 768
SWIGLU_ALPHA = 1.702
SWIGLU_LIMIT = 7.0
N_MOD = 6
GRID_W = 64
ROPE_BASE = 10000.0
EPS = 1e-6
LOG2_E = 1.4426950408889634

LANES = 128
TM = 256
TQ = 2 * TM
HEAD_SLOT = 2 * LANES
HEADS_PER_STEP = 4
KEY_BLOCK = 256
IN_W = 4096
HG_OFF = FOURIER_W
MLA_OFF = HG_OFF + 5 * HGRN_W
MLA_IN_W = IN_W - MLA_OFF
N_LEVELS = 9
ADA_TN = 1024
W1_TN = 768
HALF = D_MODEL // 2
NEG_BIG = -1e30
ISSUE_UNROLL = 8
VMEM_LIMIT = 56 * 1024 * 1024


def _cparams(sem, vmem=VMEM_LIMIT):
    return pltpu.CompilerParams(dimension_semantics=sem, vmem_limit_bytes=vmem)


def _dot(a, b):
    return jnp.dot(a, b, preferred_element_type=F32)


def _dot_nt(a, b):
    return lax.dot_general(a, b, (((1,), (1,)), ((), ())), preferred_element_type=F32)


def _dot_tn(a, b):
    return lax.dot_general(a, b, (((0,), (0,)), ((), ())), preferred_element_type=F32)


def _split_dot(x, m):
    hi = x.astype(BF16)
    lo = (x - hi.astype(F32)).astype(BF16)
    return _dot(hi, m) + _dot(lo, m)


def _layer_norm(x):
    mu = jnp.mean(x, axis=-1, keepdims=True)
    xc = x - mu
    var = jnp.mean(xc * xc, axis=-1, keepdims=True)
    return xc * lax.rsqrt(var + EPS)


def _rms(x):
    return x * lax.rsqrt(jnp.mean(x * x, axis=-1, keepdims=True) + EPS)


def _sigmoid(x):
    return 1.0 / (1.0 + jnp.exp(-x))


def _mod_row(i, tpb, nlt, nb):
    return jnp.where(i % tpb >= nlt, nb, i // tpb)


def _mod(mod_ref, row, k):
    return mod_ref[pl.ds(row, 1), k * D_MODEL:(k + 1) * D_MODEL]


def _ada_kernel(c_ref, w_ref, b_ref, o_ref):
    c = c_ref[...]
    s = c * _sigmoid(c)
    hi = s.astype(BF16)
    lo = (s - hi.astype(F32)).astype(BF16)
    w = w_ref[...].astype(BF16)
    o_ref[...] = _dot(hi, w) + _dot(lo, w) + b_ref[...]


def _ada(cc, w_ada, b_ada):
    depth, d, n = w_ada.shape
    rows = cc.shape[0]
    return pl.pallas_call(
        _ada_kernel,
        grid=(depth, n // ADA_TN),
        in_specs=[
            pl.BlockSpec((rows, d), lambda l, j: (0, 0)),
            pl.BlockSpec((None, d, ADA_TN), lambda l, j: (l, 0, j)),
            pl.BlockSpec((None, 1, ADA_TN), lambda l, j: (l, 0, j)),
        ],
        out_specs=pl.BlockSpec((None, rows, ADA_TN), lambda l, j: (l, 0, j)),
        out_shape=jax.ShapeDtypeStruct((depth, rows, n), F32),
        compiler_params=_cparams(("arbitrary", "arbitrary")),
        name="ada",
    )(cc, w_ada, b_ada.reshape(depth, 1, n))


def _pre_kernel(x_ref, mod_ref, w_ref, dft_ref, z_ref, hg_ref, mla_ref, *, tpb, nlt, nb):
    row = _mod_row(pl.program_id(0), tpb, nlt, nb)
    h = _layer_norm(x_ref[...]) * (1.0 + _mod(mod_ref, row, 1)) + _mod(mod_ref, row, 0)
    r = _dot(h.astype(BF16), w_ref[...])
    z_ref[...] = _dot(r[:, 0:FOURIER_W].astype(BF16), dft_ref[...]).astype(BF16)
    hg_ref[...] = r[:, HG_OFF:MLA_OFF]
    mla_ref[...] = r[:, MLA_OFF:IN_W]


def _layer_block(a, layer, **kw):
    return pl.BlockSpec((None,) + a.shape[1:], lambda *_: (layer, 0, 0), **kw)


def _pre(x, mod_all, w_in_all, dft_c, layer, *, tpb, nlt, nb):
    n = x.shape[0]
    kern = functools.partial(_pre_kernel, tpb=tpb, nlt=nlt, nb=nb)
    return pl.pallas_call(
        kern,
        grid=(n // TM,),
        in_specs=[
            pl.BlockSpec((TM, D_MODEL), lambda i: (i, 0)),
            _layer_block(mod_all, layer),
            _layer_block(w_in_all, layer, pipeline_mode=pl.Buffered(1)),
            pl.BlockSpec(dft_c.shape, lambda i: (0, 0)),
        ],
        out_specs=[
            pl.BlockSpec((TM, 2 * FOURIER_W), lambda i: (i, 0)),
            pl.BlockSpec((TM, 5 * HGRN_W), lambda i: (i, 0)),
            pl.BlockSpec((TM, MLA_IN_W), lambda i: (i, 0)),
        ],
        out_shape=[
            jax.ShapeDtypeStruct((n, 2 * FOURIER_W), BF16),
            jax.ShapeDtypeStruct((n, 5 * HGRN_W), F32),
            jax.ShapeDtypeStruct((n, MLA_IN_W), F32),
        ],
        compiler_params=_cparams(("arbitrary",)),
        name="pre",
    )(x, mod_all, w_in_all, dft_c)


def _four_kernel(z_ref, cl_ref, sl_ref, cc_ref, sc_ref, y_ref, *, nlt, seq):
    j = pl.program_id(1)

    def emit(cm, sm, z):
        y = _dot(cm, z[:, 0:FOURIER_W]) - _dot(sm, z[:, FOURIER_W:])
        y_ref[...] = y.astype(BF16)

    @pl.when(j < nlt)
    def _():
        emit(cl_ref[...], sl_ref[...], z_ref[0:seq, :])

    @pl.when(j >= nlt)
    def _():
        emit(cc_ref[...], sc_ref[...], z_ref[seq:, :])


def _fourier(z, dft_lat, dft_ctx, *, nb, tpb, nlt, seq, ctx_len):
    n = z.shape[0]
    r = seq + ctx_len
    cl, sl = dft_lat
    cc, sc = dft_ctx
    kern = functools.partial(_four_kernel, nlt=nlt, seq=seq)
    lmap = lambda b, j: (jnp.minimum(j, nlt - 1), 0)
    cmap = lambda b, j: (jnp.maximum(j - nlt, 0), 0)
    return pl.pallas_call(
        kern,
        grid=(nb, tpb),
        in_specs=[
            pl.BlockSpec((None, r, 2 * FOURIER_W), lambda b, j: (b, 0, 0)),
            pl.BlockSpec((TM, seq), lmap),
            pl.BlockSpec((TM, seq), lmap),
            pl.BlockSpec((TM, ctx_len), cmap),
            pl.BlockSpec((TM, ctx_len), cmap),
        ],
        out_specs=pl.BlockSpec((TM, FOURIER_W), lambda b, j: (b * tpb + j, 0)),
        out_shape=jax.ShapeDtypeStruct((n, FOURIER_W), BF16),
        compiler_params=_cparams(("arbitrary", "arbitrary")),
        name="fourier",
    )(z.reshape(nb, r, 2 * FOURIER_W), cl, sl, cc, sc)


def _level_matrices(reverse):
    p = np.arange(TM)
    mats = [(p[None, :] <= p[:, None])]
    for lev in range(1, N_LEVELS):
        r = TM >> lev
        same = (p[None, :] // r) == (p[:, None] // r)
        is_q = ((p // r) % 2 == 1)[:, None]
        mats.append(same & np.where(is_q, p[None, :] <= p[:, None], p[None, :] > p[:, None]))
    g = np.stack(mats).astype(np.float32)
    if reverse:
        g = g[:, ::-1, ::-1]
    return jnp.asarray(g.reshape(N_LEVELS * TM, TM), dtype=BF16)


def _hgrn_kernel(q_ref, v_ref, z_ref, lb_ref, g_ref, o_ref, st_ref, *, reverse):
    @pl.when(pl.program_id(1) == 0)
    def _():
        st_ref[...] = jnp.zeros_like(st_ref)

    z = z_ref[...]
    log_lb = lb_ref[0:1, :]
    log_1m_lb = lb_ref[1:2, :]
    one_m_lb = lb_ref[2:3, :]
    log_sig = jnp.minimum(z, 0.0) - jnp.log1p(jnp.exp(-jnp.abs(z)))
    a2 = log_1m_lb + log_sig
    log_f = jnp.maximum(log_lb, a2) + jnp.log1p(jnp.exp(-jnp.abs(log_lb - a2)))
    kk = one_m_lb * _sigmoid(-z)

    wall = _split_dot_left(g_ref[...], log_f)
    a = wall[0:TM]
    a_end = a[0:1] if reverse else a[TM - 1:TM]
    q = q_ref[...]
    v = v_ref[...].astype(BF16)
    qa = (q * jnp.exp(a)).astype(BF16)
    kd = (kk * jnp.exp(a_end - a)).astype(BF16)
    carry = jnp.exp(a_end)

    row = lax.broadcasted_iota(I32, (TM, TM), 0)
    col = lax.broadcasted_iota(I32, (TM, TM), 1)
    diff = row ^ col
    pos = lax.broadcasted_iota(I32, (TM, HGRN_DIM), 0)
    if reverse:
        pos = TM - 1 - pos

    for h in range(HGRN_HEADS):
        sl = slice(h * HGRN_DIM, (h + 1) * HGRN_DIM)
        st = st_ref[h]
        qh, kh, vh = q[:, sl], kk[:, sl], v[:, sl]
        p = jnp.where(diff == 0, _dot_nt(qh.astype(BF16), kh.astype(BF16)), 0.0)
        for lev in range(1, N_LEVELS):
            shift = (TM >> lev).bit_length() - 1
            fac = jnp.exp(wall[lev * TM:(lev + 1) * TM, sl])
            is_q = ((pos >> shift) & 1) == 1
            qt = jnp.where(is_q, qh * fac, 0.0).astype(BF16)
            kt = jnp.where(is_q, 0.0, kh * fac).astype(BF16)
            p = p + jnp.where((diff >> (shift + 1)) == 0, _dot_nt(qt, kt), 0.0)
        o_ref[:, sl] = _dot_nt(qa[:, sl], st.astype(BF16)) + _dot(p.astype(BF16), vh)
        st_ref[h] = st * carry[:, sl] + _dot_tn(vh, kd[:, sl])


def _split_dot_left(m, x):
    hi = x.astype(BF16)
    lo = (x - hi.astype(F32)).astype(BF16)
    return _dot(m, hi) + _dot(m, lo)


def _hgrn(hg, lb_rows, *, z_block, reverse, nb, tpb, nlt):
    n = hg.shape[0]
    nct = tpb - nlt

    def chunk(b, s):
        if reverse:
            j = jnp.where(s < nct, tpb - 1 - s, nlt - 1 - (s - nct))
        else:
            j = jnp.where(s < nct, nlt + s, s - nct)
        return b * tpb + j

    kern = functools.partial(_hgrn_kernel, reverse=reverse)
    gmat = _level_matrices(reverse)
    return pl.pallas_call(
        kern,
        grid=(nb, tpb),
        in_specs=[
            pl.BlockSpec((TM, HGRN_W), lambda b, s: (chunk(b, s), 0)),
            pl.BlockSpec((TM, HGRN_W), lambda b, s: (chunk(b, s), 1)),
            pl.BlockSpec((TM, HGRN_W), lambda b, s: (chunk(b, s), z_block)),
            pl.BlockSpec((8, HGRN_W), lambda b, s: (0, 0)),
            pl.BlockSpec(gmat.shape, lambda b, s: (0, 0)),
        ],
        out_specs=pl.BlockSpec((TM, HGRN_W), lambda b, s: (chunk(b, s), 0)),
        out_shape=jax.ShapeDtypeStruct((n, HGRN_W), F32),
        scratch_shapes=[pltpu.VMEM((HGRN_HEADS, HGRN_DIM, HGRN_DIM), F32)],
        compiler_params=_cparams(("arbitrary", "arbitrary")),
        name="hgrn_bwd" if reverse else "hgrn_fwd",
    )(hg, hg, hg, lb_rows, gmat)


def _mla_kernel(m_ref, qg_ref, kvg_ref, wq_ref, wkv_ref, cos_ref, sin_ref, q_ref, k_ref, v_ref):
    cq = m_ref[:, 0:Q_LORA]
    ckv = m_ref[:, Q_LORA:Q_LORA + KV_LORA]
    kr = m_ref[:, Q_LORA + KV_LORA:Q_LORA + KV_LORA + LANES]
    kr_sw = m_ref[:, Q_LORA + KV_LORA + LANES:MLA_IN_W]
    cos = cos_ref[...]
    sin = sin_ref[...]
    qall = _dot((_rms(cq) * qg_ref[...]).astype(BF16), wq_ref[...]) * (MLA_SCALE * LOG2_E)
    kv = _dot((_rms(ckv) * kvg_ref[...]).astype(BF16), wkv_ref[...])
    k_rot = (kr * cos + kr_sw * sin).astype(BF16)
    ones = jnp.ones((TM, LANES), BF16)
    nw = MLA_HEADS * LANES
    for h in range(MLA_HEADS):
        a, b = h * LANES, (h + 1) * LANES
        s = h * HEAD_SLOT
        q_ref[:, s:s + LANES] = qall[:, a:b].astype(BF16)
        q_ref[:, s + LANES:s + HEAD_SLOT] = (qall[:, nw + a:nw + b] * cos + qall[:, 2 * nw + a:2 * nw + b] * sin).astype(BF16)
        k_ref[:, s:s + LANES] = kv[:, a:b].astype(BF16)
        k_ref[:, s + LANES:s + HEAD_SLOT] = k_rot
        v_ref[:, s:s + LANES] = kv[:, nw + a:nw + b].astype(BF16)
        v_ref[:, s + LANES:s + HEAD_SLOT] = ones


def _mla_up(mla_in, q_gain, kv_gain, wq_all, wkv_all, cos_t, sin_t, layer):
    n = mla_in.shape[0]
    w = MLA_HEADS * HEAD_SLOT
    full = lambda a: pl.BlockSpec(a.shape, lambda i: (0, 0))
    return pl.pallas_call(
        _mla_kernel,
        grid=(n // TM,),
        in_specs=[
            pl.BlockSpec((TM, MLA_IN_W), lambda i: (i, 0)),
            full(q_gain), full(kv_gain), _layer_block(wq_all, layer), _layer_block(wkv_all, layer),
            pl.BlockSpec((TM, LANES), lambda i: (i, 0)),
            pl.BlockSpec((TM, LANES), lambda i: (i, 0)),
        ],
        out_specs=[pl.BlockSpec((TM, w), lambda i: (i, 0))] * 3,
        out_shape=[jax.ShapeDtypeStruct((n, w), BF16)] * 3,
        compiler_params=_cparams(("arbitrary",)),
        name="mla_up",
    )(mla_in, q_gain, kv_gain, wq_all, wkv_all, cos_t, sin_t)


def _attn_kernel(q_ref, k_ref, v_ref, o_ref, s_ref, *, n_lat_steps, seq, ctx_len):
    j = pl.program_id(2)

    def run(rows, k0, k1):
        blocks = [(kb, slice(k0 + kb * KEY_BLOCK, k0 + (kb + 1) * KEY_BLOCK), slice(kb * KEY_BLOCK, (kb + 1) * KEY_BLOCK))
                  for kb in range((k1 - k0) // KEY_BLOCK)]
        hslot = lambda hh: slice(hh * HEAD_SLOT, (hh + 1) * HEAD_SLOT)

        def scores(hh, blk, m_run):
            _, ks, ss = blk
            s = _dot_nt(q_ref[0:rows, hslot(hh)], k_ref[ks, hslot(hh)])
            s_ref[hh % 2, 0:rows, ss] = s
            for c in range(KEY_BLOCK // LANES):
                m_run = jnp.maximum(m_run, s[:, c * LANES:(c + 1) * LANES])
            return m_run

        def values(hh, blk, m, o):
            _, ks, ss = blk
            d = (s_ref[hh % 2, 0:rows, ss] - m).astype(BF16)
            return o + _dot(jnp.exp2(d), v_ref[ks, hslot(hh)])

        def emit(hh, o):
            o_ref[0:rows, hh * MLA_V:(hh + 1) * MLA_V] = (o[:, 0:MLA_V] / o[:, MLA_V:]).astype(BF16)

        neg = jnp.full((rows, LANES), -jnp.inf, F32)
        zero = jnp.zeros((rows, HEAD_SLOT), F32)
        m_run = neg
        for blk in blocks:
            m_run = scores(0, blk, m_run)
        m_prev = jnp.max(m_run, axis=-1, keepdims=True)
        for hh in range(1, HEADS_PER_STEP):
            m_run, o = neg, zero
            for blk in blocks:
                m_run = scores(hh, blk, m_run)
                o = values(hh - 1, blk, m_prev, o)
            emit(hh - 1, o)
            m_prev = jnp.max(m_run, axis=-1, keepdims=True)
        o = zero
        for blk in blocks:
            o = values(HEADS_PER_STEP - 1, blk, m_prev, o)
        emit(HEADS_PER_STEP - 1, o)

    @pl.when(j < n_lat_steps)
    def _():
        run(TQ, 0, seq + ctx_len)

    @pl.when(j >= n_lat_steps)
    def _():
        run(ctx_len, seq, seq + ctx_len)


def _attention(q, k, v, *, nb, seq, ctx_len):
    n = q.shape[0]
    r = seq + ctx_len
    assert seq % TQ == 0 and ctx_len <= TQ
    n_lat_steps = seq // TQ
    hp = MLA_HEADS // HEADS_PER_STEP
    wq = HEADS_PER_STEP * HEAD_SLOT
    wo = HEADS_PER_STEP * MLA_V
    kern = functools.partial(_attn_kernel, n_lat_steps=n_lat_steps, seq=seq, ctx_len=ctx_len)
    out = pl.pallas_call(
        kern,
        grid=(nb, hp, n_lat_steps + 1),
        in_specs=[
            pl.BlockSpec((None, TQ, wq), lambda b, h, j: (b, j, h)),
            pl.BlockSpec((None, r, wq), lambda b, h, j: (b, 0, h), pipeline_mode=pl.Buffered(1)),
            pl.BlockSpec((None, r, wq), lambda b, h, j: (b, 0, h), pipeline_mode=pl.Buffered(1)),
        ],
        out_specs=pl.BlockSpec((None, TQ, wo), lambda b, h, j: (b, j, h)),
        out_shape=jax.ShapeDtypeStruct((nb, r, MLA_HEADS * MLA_V), BF16),
        scratch_shapes=[pltpu.VMEM((2, TQ, r), F32)],
        compiler_params=_cparams(("arbitrary", "arbitrary", "arbitrary")),
        name="attention",
    )(q.reshape(nb, r, -1), k.reshape(nb, r, -1), v.reshape(nb, r, -1))
    return out.reshape(n, MLA_HEADS * MLA_V)


def _pack_rows(x):
    w = x.shape[1] // 2
    lo = pltpu.bitcast(x[:, :w].astype(BF16).astype(F32), U32)
    hi = pltpu.bitcast(x[:, w:].astype(BF16).astype(F32), U32)
    return (lo >> 16) | (hi & jnp.uint32(0xFFFF0000))


def _unpack_rows(w):
    lo = pltpu.bitcast(w << 16, F32)
    hi = pltpu.bitcast(w & jnp.uint32(0xFFFF0000), F32)
    return lo, hi


def _post_kernel(yf_ref, of_ref, ob_ref, g_ref, att_ref, x_ref, mod_ref, hn_ref, wo_ref, lng_ref, lnb_ref,
                 rwh_ref, rwl_ref, rb_ref, x1_ref, h2_ref, eid_ref, gate_ref, *, tpb, nlt, nb, alpha):
    row = _mod_row(pl.program_id(0), tpb, nlt, nb)
    o = of_ref[...] + ob_ref[...]
    hn = hn_ref[...]
    gate_act = g_ref[...]
    gate_act = gate_act * _sigmoid(gate_act)
    mix = _dot(yf_ref[...], wo_ref[0:FOURIER_W, :])
    for h in range(HGRN_HEADS):
        sl = slice(h * HGRN_DIM, (h + 1) * HGRN_DIM)
        r = (_rms(o[:, sl]) * hn[:, sl] * gate_act[:, sl]).astype(BF16)
        mix = mix + _dot(r, wo_ref[FOURIER_W + h * HGRN_DIM:FOURIER_W + (h + 1) * HGRN_DIM, :])
    mix = mix + _dot(att_ref[...], wo_ref[FOURIER_W + HGRN_W:, :])

    x1 = _layer_norm(alpha * x_ref[...] + _mod(mod_ref, row, 2) * mix) * lng_ref[...] + lnb_ref[...]
    x1_ref[...] = x1
    h2 = _layer_norm(x1) * (1.0 + _mod(mod_ref, row, 4)) + _mod(mod_ref, row, 3)
    h2_ref[...] = _pack_rows(h2)

    hi = h2.astype(BF16)
    lo = (h2 - hi.astype(F32)).astype(BF16)
    logits = _dot(hi, rwh_ref[...]) + _dot(lo, rwh_ref[...]) + _dot(hi, rwl_ref[...]) + rb_ref[...]
    lane = lax.broadcasted_iota(I32, logits.shape, 1)
    cur = logits
    vals, ids = [], []
    for _ in range(TOP_K):
        m = jnp.max(cur, axis=-1, keepdims=True)
        idx = jnp.min(jnp.where(cur == m, lane, LANES), axis=-1, keepdims=True)
        vals.append(m)
        ids.append(idx)
        cur = jnp.where(lane == idx, -3e38, cur)
    ex = [jnp.exp(vk - vals[0]) for vk in vals]
    den = ex[0] + ex[1] + ex[2] + ex[3]
    eid = jnp.zeros(logits.shape, I32)
    gate = jnp.zeros(logits.shape, F32)
    for k in range(TOP_K):
        eid = jnp.where(lane == k, ids[k], eid)
        gate = jnp.where(lane == k, ex[k] / den, gate)
    eid_ref[...] = eid
    gate_ref[...] = gate


def _post(yf, o_f, o_b, hg, att, x, mod_all, hnorm, w_out_all, ln_g, ln_b, rw_hi, rw_lo, rb, layer, *, tpb, nlt, nb, alpha):
    n = x.shape[0]
    kern = functools.partial(_post_kernel, tpb=tpb, nlt=nlt, nb=nb, alpha=alpha)
    full = lambda a: pl.BlockSpec(a.shape, lambda i: (0, 0))
    return pl.pallas_call(
        kern,
        grid=(n // TM,),
        in_specs=[
            pl.BlockSpec((TM, FOURIER_W), lambda i: (i, 0)),
            pl.BlockSpec((TM, HGRN_W), lambda i: (i, 0)),
            pl.BlockSpec((TM, HGRN_W), lambda i: (i, 0)),
            pl.BlockSpec((TM, HGRN_W), lambda i: (i, 2)),
            pl.BlockSpec((TM, MLA_HEADS * MLA_V), lambda i: (i, 0)),
            pl.BlockSpec((TM, D_MODEL), lambda i: (i, 0)),
            _layer_block(mod_all, layer), full(hnorm),
            _layer_block(w_out_all, layer, pipeline_mode=pl.Buffered(1)),
            full(ln_g), full(ln_b), full(rw_hi), full(rw_lo), full(rb),
        ],
        out_specs=[
            pl.BlockSpec((TM, D_MODEL), lambda i: (i, 0)),
            pl.BlockSpec((TM, HALF), lambda i: (i, 0)),
            pl.BlockSpec((TM, LANES), lambda i: (i, 0)),
            pl.BlockSpec((TM, LANES), lambda i: (i, 0)),
        ],
        out_shape=[
            jax.ShapeDtypeStruct((n, D_MODEL), F32),
            jax.ShapeDtypeStruct((n, HALF), U32),
            jax.ShapeDtypeStruct((n, LANES), I32),
            jax.ShapeDtypeStruct((n, LANES), F32),
        ],
        compiler_params=_cparams(("arbitrary",)),
        name="post",
    )(yf, o_f, o_b, hg, att, x, mod_all, hnorm, w_out_all, ln_g, ln_b, rw_hi, rw_lo, rb)


def _route_kernel(eid_ref, pos_ref, te_ref, meta_ref, base_ref, cnt_ref, *, n_tiles_pad):
    ph = pl.program_id(0)
    i = pl.program_id(1)
    lane = lax.broadcasted_iota(I32, (TM, LANES), 1)
    eid = eid_ref[...]
    onehot = [(lane == eid[:, k:k + 1]).astype(F32) for k in range(TOP_K)]
    colsum = [jnp.sum(oh, axis=0, keepdims=True) for oh in onehot]

    @pl.when(ph == 0)
    def _():
        @pl.when(i == 0)
        def _():
            cnt_ref[...] = jnp.zeros_like(cnt_ref)

        base_ref[pl.ds(i, 1), :] = cnt_ref[0:1, :]
        cnt_ref[0:1, :] = cnt_ref[0:1, :] + colsum[0] + colsum[1] + colsum[2] + colsum[3]

    @pl.when(ph == 1)
    def _():
        cnt = cnt_ref[0:1, :]
        tiles = jnp.floor((cnt + (TM - 1)) * (1.0 / TM))
        r128 = lax.broadcasted_iota(I32, (LANES, LANES), 0)
        c128 = lax.broadcasted_iota(I32, (LANES, LANES), 1)
        incl = (r128 <= c128).astype(BF16)
        pad_end = _split_dot(jnp.broadcast_to(tiles, (8, LANES)), incl)[0:1, :] * TM
        pad_start = pad_end - tiles * TM

        rr = lax.broadcasted_iota(I32, (TM, TM), 0)
        cc = lax.broadcasted_iota(I32, (TM, TM), 1)
        strict = (cc < rr).astype(BF16)
        run = pad_start + base_ref[pl.ds(i, 1), :]
        pos = jnp.zeros((TM, LANES), I32)
        for k in range(TOP_K):
            before = _dot(strict, onehot[k].astype(BF16))
            val = jnp.sum(onehot[k] * (before + run), axis=1, keepdims=True)
            pos = jnp.where(lane == k, val.astype(I32), pos)
            run = run + colsum[k]
        pos_ref[...] = pos

        @pl.when(i == 0)
        def _():
            tstart = (lax.broadcasted_iota(I32, (n_tiles_pad, LANES), 0) * TM).astype(F32)
            lane_t = lax.broadcasted_iota(I32, (n_tiles_pad, LANES), 1)
            done = jnp.where((pad_end <= tstart) & (lane_t < N_EXPERTS), 1.0, 0.0)
            te = jnp.minimum(jnp.sum(done, axis=1, keepdims=True), N_EXPERTS - 1.0)
            te_ref[...] = jnp.broadcast_to(te, (n_tiles_pad, LANES)).astype(I32)
            sub = lax.broadcasted_iota(I32, (8, LANES), 0)
            meta = jnp.where(sub == 0, pad_start + cnt, jnp.where(sub == 1, pad_end - pad_start - cnt, pad_end))
            meta_ref[...] = meta.astype(I32)


def _route(eid, n_tiles):
    n = eid.shape[0]
    nt = n // TM
    n_tiles_pad = -(-n_tiles // 8) * 8
    kern = functools.partial(_route_kernel, n_tiles_pad=n_tiles_pad)
    pos, te, meta = pl.pallas_call(
        kern,
        grid=(2, nt),
        in_specs=[pl.BlockSpec((TM, LANES), lambda ph, i: (i, 0))],
        out_specs=[
            pl.BlockSpec((TM, LANES), lambda ph, i: (ph * i, 0)),
            pl.BlockSpec((n_tiles_pad, LANES), lambda ph, i: (0, 0)),
            pl.BlockSpec((8, LANES), lambda ph, i: (0, 0)),
        ],
        out_shape=[
            jax.ShapeDtypeStruct((n, LANES), I32),
            jax.ShapeDtypeStruct((n_tiles_pad, LANES), I32),
            jax.ShapeDtypeStruct((8, LANES), I32),
        ],
        scratch_shapes=[pltpu.VMEM((-(-nt // 8) * 8, LANES), F32), pltpu.VMEM((8, LANES), F32)],
        compiler_params=_cparams(("arbitrary", "arbitrary")),
        name="moe_route",
    )(eid)
    pos_flat = pos[:, :TOP_K].reshape(-1)
    tile_e = te[:n_tiles, 0]
    fill_start = meta[0, :N_EXPERTS]
    fill_n = meta[1, :N_EXPERTS]
    n_used = meta[2, N_EXPERTS - 1:N_EXPERTS] // TM
    return pos_flat, tile_e, n_used, fill_start, fill_n


def _dispatch_kernel(pos_ref, fs_ref, fn_ref, nu_ref, h_ref, o_hbm, zero_ref, sem, fill_sem, *, n_tiles):
    i = pl.program_id(0)

    def start(r, c):
        for k in range(TOP_K):
            p = pos_ref[(i * TM + r) * TOP_K + k]
            pltpu.make_async_copy(h_ref.at[pl.ds(r, 1), :], o_hbm.at[pl.ds(p, 1), :], sem).start()
        return c

    lax.fori_loop(0, TM, start, 0, unroll=ISSUE_UNROLL)

    @pl.when(i == 0)
    def _():
        zero_ref[...] = jnp.zeros_like(zero_ref)

        def fill_row(row):
            return pltpu.make_async_copy(zero_ref.at[pl.ds(0, 1), :], o_hbm.at[pl.ds(row, 1), :], fill_sem)

        def fill_tile(t):
            return pltpu.make_async_copy(zero_ref, o_hbm.at[pl.ds(pl.multiple_of(t * TM, TM), TM), :], fill_sem)

        def each(fn):
            for e in range(N_EXPERTS):
                lax.fori_loop(0, fn_ref[e], lambda t, c, e=e: (fn(fill_row(fs_ref[e] + t)), c)[1], 0)
            lax.fori_loop(nu_ref[0], n_tiles, lambda t, c: (fn(fill_tile(t)), c)[1], 0)

        each(lambda cp: cp.start())
        each(lambda cp: cp.wait())

    for _ in range(TOP_K):
        pltpu.make_async_copy(h_ref, o_hbm.at[pl.ds(0, TM), :], sem).wait()


def _dispatch(pos_flat, fill_start, fill_n, n_used, h2p, n_tiles):
    n = h2p.shape[0]
    return pl.pallas_call(
        functools.partial(_dispatch_kernel, n_tiles=n_tiles),
        grid_spec=pltpu.PrefetchScalarGridSpec(
            num_scalar_prefetch=4,
            grid=(n // TM,),
            in_specs=[pl.BlockSpec((TM, HALF), lambda i, p, fs, fn, nu: (i, 0))],
            out_specs=pl.BlockSpec(memory_space=pl.ANY),
            scratch_shapes=[pltpu.VMEM((TM, HALF), U32), pltpu.SemaphoreType.DMA(()), pltpu.SemaphoreType.DMA(())],
        ),
        out_shape=jax.ShapeDtypeStruct((n_tiles * TM, HALF), U32),
        compiler_params=_cparams(("arbitrary",)),
        name="moe_dispatch",
    )(pos_flat, fill_start, fill_n, n_used, h2p)


def _w1_kernel(w_ref, g_ref, l_ref, t_ref):
    for c in range(D_MODEL // LANES):
        cs = slice(c * LANES, (c + 1) * LANES)
        t_ref[c] = w_ref[cs, :].T
        g_ref[:, cs] = t_ref[c, pl.ds(0, W1_TN // 2, stride=2), :].astype(BF16)
        l_ref[:, cs] = t_ref[c, pl.ds(1, W1_TN // 2, stride=2), :].astype(BF16)


def _prep_w1(w1):
    depth, ne, d, n2 = w1.shape
    out = jax.ShapeDtypeStruct((depth * ne, n2 // 2, d), BF16)
    g, l = pl.pallas_call(
        _w1_kernel,
        grid=(depth * ne, n2 // W1_TN),
        in_specs=[pl.BlockSpec((None, d, W1_TN), lambda e, j: (e, 0, j))],
        out_specs=[pl.BlockSpec((None, W1_TN // 2, d), lambda e, j: (e, j, 0))] * 2,
        out_shape=[out, out],
        scratch_shapes=[pltpu.VMEM((d // LANES, W1_TN, LANES), F32)],
        compiler_params=_cparams(("arbitrary", "arbitrary")),
        name="moe_w1_prep",
    )(w1.reshape(depth * ne, d, n2))
    return g, l


def _moe_kernel(te_ref, nu_ref, x_ref, w1g_ref, w1l_ref, b1g_ref, b1l_ref, w2_ref, b2_ref, y_ref, w2b_ref):
    i = pl.program_id(0)

    @pl.when(i < nu_ref[0])
    def _():
        @pl.when((i == 0) | (te_ref[i] != te_ref[jnp.maximum(i - 1, 0)]))
        def _():
            w2b_ref[...] = w2_ref[...].astype(BF16)

        lo, hi = _unpack_rows(x_ref[...])
        lo = lo.astype(BF16)
        hi = hi.astype(BF16)
        ug = _dot_nt(lo, w1g_ref[:, 0:HALF]) + _dot_nt(hi, w1g_ref[:, HALF:]) + b1g_ref[...]
        ul = _dot_nt(lo, w1l_ref[:, 0:HALF]) + _dot_nt(hi, w1l_ref[:, HALF:]) + b1l_ref[...]
        xg = jnp.minimum(ug, SWIGLU_LIMIT)
        xl = jnp.clip(ul, -SWIGLU_LIMIT, SWIGLU_LIMIT)
        act = xg * _sigmoid(SWIGLU_ALPHA * xg) * (xl + 1.0)
        y = _dot(act.astype(BF16), w2b_ref[...]) + b2_ref[...]
        y_ref[...] = _pack_rows(y)

    @pl.when(i >= nu_ref[0])
    def _():
        y_ref[...] = jnp.zeros_like(y_ref)


def _moe_ffn(tile_e, n_used, xs, w1g, w1l, b1g, b1l, w2, b2, layer):
    m = xs.shape[0]
    emap = lambda i, te, nu: (layer * N_EXPERTS + te[i], 0, 0)
    return pl.pallas_call(
        _moe_kernel,
        grid_spec=pltpu.PrefetchScalarGridSpec(
            num_scalar_prefetch=2,
            grid=(m // TM,),
            in_specs=[
                pl.BlockSpec((TM, HALF), lambda i, te, nu: (i, 0)),
                pl.BlockSpec((None, EXPERT_FF, D_MODEL), emap),
                pl.BlockSpec((None, EXPERT_FF, D_MODEL), emap),
                pl.BlockSpec((None, 1, EXPERT_FF), emap),
                pl.BlockSpec((None, 1, EXPERT_FF), emap),
                pl.BlockSpec((None, EXPERT_FF, D_MODEL), emap),
                pl.BlockSpec((None, 1, D_MODEL), emap),
            ],
            out_specs=pl.BlockSpec((TM, HALF), lambda i, te, nu: (i, 0)),
            scratch_shapes=[pltpu.VMEM((EXPERT_FF, D_MODEL), BF16)],
        ),
        out_shape=jax.ShapeDtypeStruct((m, HALF), U32),
        compiler_params=_cparams(("arbitrary",)),
        name="moe_ffn",
    )(tile_e, n_used, xs, w1g, w1l, b1g, b1l, w2, b2)


def _combine_kernel(pos_ref, y_hbm, gate_ref, x_ref, mod_ref, lng_ref, lnb_ref, o_ref, buf, sem, *, tpb, nlt, nb, alpha,
                    latent_only):
    i = pl.program_id(0)
    n_steps = pl.num_programs(0)
    row = _mod_row(i, tpb, nlt, nb)

    def issue(tile, slot):
        def start(r, c):
            for k in range(TOP_K):
                p = pos_ref[(tile * TM + r) * TOP_K + k]
                pltpu.make_async_copy(y_hbm.at[pl.ds(p, 1), :], buf.at[slot, k, pl.ds(r, 1), :], sem.at[slot]).start()
            return c

        lax.fori_loop(0, TM, start, 0, unroll=ISSUE_UNROLL)

    @pl.when(i == 0)
    def _():
        issue(0, 0)

    @pl.when(i + 1 < n_steps)
    def _():
        issue(i + 1, (i + 1) % 2)

    slot = i % 2
    for k in range(TOP_K):
        pltpu.make_async_copy(y_hbm.at[pl.ds(0, TM), :], buf.at[slot, k], sem.at[slot]).wait()

    gates = gate_ref[...]
    acc_lo = jnp.zeros((TM, HALF), F32)
    acc_hi = jnp.zeros((TM, HALF), F32)
    for k in range(TOP_K):
        lo, hi = _unpack_rows(buf[slot, k])
        gk = gates[:, k:k + 1]
        acc_lo = acc_lo + gk * lo
        acc_hi = acc_hi + gk * hi
    ffn = jnp.concatenate([acc_lo, acc_hi], axis=-1)
    x2 = _layer_norm(alpha * x_ref[...] + _mod(mod_ref, row, 5) * ffn) * lng_ref[...] + lnb_ref[...]
    if latent_only:
        @pl.when(i % tpb < nlt)
        def _():
            o_ref[...] = x2
    else:
        o_ref[...] = x2


def _combine(pos, ys, gates, x, mod_all, ln_g, ln_b, layer, *, latent_only, tpb, nlt, nb, alpha):
    n = x.shape[0]
    kern = functools.partial(_combine_kernel, tpb=tpb, nlt=nlt, nb=nb, alpha=alpha, latent_only=latent_only)
    full = lambda a: pl.BlockSpec(a.shape, lambda i, p: (0, 0))
    if latent_only:
        omap = lambda i, p: ((i // tpb) * nlt + jnp.minimum(i % tpb, nlt - 1), 0)
        n_out = nb * nlt * TM
    else:
        omap = lambda i, p: (i, 0)
        n_out = n
    return pl.pallas_call(
        kern,
        grid_spec=pltpu.PrefetchScalarGridSpec(
            num_scalar_prefetch=1,
            grid=(n // TM,),
            in_specs=[
                pl.BlockSpec(memory_space=pl.ANY),
                pl.BlockSpec((TM, LANES), lambda i, p: (i, 0)),
                pl.BlockSpec((TM, D_MODEL), lambda i, p: (i, 0)),
                _layer_block(mod_all, layer), full(ln_g), full(ln_b),
            ],
            out_specs=pl.BlockSpec((TM, D_MODEL), omap),
            scratch_shapes=[pltpu.VMEM((2, TOP_K, TM, HALF), U32), pltpu.SemaphoreType.DMA((2,))],
        ),
        out_shape=jax.ShapeDtypeStruct((n_out, D_MODEL), F32),
        compiler_params=_cparams(("arbitrary",)),
        name="moe_combine",
    )(pos, ys, gates, x, mod_all, ln_g, ln_b)


def _dft_pair(t_len):
    t = jnp.arange(t_len, dtype=I32)
    ang = ((t[:, None] * t[None, :]) % t_len).astype(F32) * (2.0 * np.pi / t_len)
    s = 1.0 / np.sqrt(t_len)
    return (jnp.cos(ang) * s).astype(BF16), (jnp.sin(ang) * s).astype(BF16)


def _channel_dft():
    c = np.arange(FOURIER_GROUP)
    ang = 2.0 * np.pi * ((c[:, None] * c[None, :]) % FOURIER_GROUP) / FOURIER_GROUP
    s = 1.0 / np.sqrt(FOURIER_GROUP)
    groups = FOURIER_W // FOURIER_GROUP
    m = np.zeros((FOURIER_W, 2 * FOURIER_W), np.float32)
    for g in range(groups):
        a, b = g * FOURIER_GROUP, (g + 1) * FOURIER_GROUP
        m[a:b, a:b] = np.cos(ang) * s
        m[a:b, FOURIER_W + a:FOURIER_W + b] = np.sin(ang) * s
    return jnp.asarray(m, dtype=BF16)


def _pair_swap(w):
    pairs = w.reshape(*w.shape[:-1], w.shape[-1] // 2, 2)
    return jnp.stack([-pairs[..., 1], pairs[..., 0]], axis=-1).reshape(w.shape)


def _pad_lanes(w, width):
    return jnp.pad(w, [(0, 0)] * (w.ndim - 1) + [(0, width - w.shape[-1])])


def _rope_tables(nb, seq, ctx_len):
    t = jnp.arange(seq)
    n_freq = MLA_ROPE // 4
    inv = ROPE_BASE ** (-jnp.arange(n_freq, dtype=F32) / n_freq)
    ang = jnp.concatenate([(t // GRID_W)[:, None] * inv, (t % GRID_W)[:, None] * inv], axis=-1)
    ang = jnp.repeat(ang, 2, axis=-1)
    cos = jnp.concatenate([jnp.cos(ang), jnp.ones((ctx_len, MLA_ROPE), F32)], axis=0)
    sin = jnp.concatenate([jnp.sin(ang), jnp.zeros((ctx_len, MLA_ROPE), F32)], axis=0)
    cos = jnp.tile(_pad_lanes(cos, LANES), (nb, 1))
    sin = jnp.tile(_pad_lanes(sin, LANES), (nb, 1))
    return cos, sin


def _prep_in_proj(w_in):
    f, q, i, g, zf, zb, cq, ckv, kr = jnp.split(w_in, np.cumsum([512, 512, 512, 512, 512, 512, 512, 256]).tolist(), axis=-1)
    cols = [f, q, i, g, zf, zb, cq, ckv, _pad_lanes(kr, LANES), _pad_lanes(_pair_swap(kr), LANES)]
    return jnp.concatenate(cols, axis=-1).astype(BF16)


def _prep_wq(w_uq):
    w = w_uq.reshape(Q_LORA, MLA_HEADS, MLA_NOPE + MLA_ROPE)
    nope = w[:, :, :MLA_NOPE].reshape(Q_LORA, -1)
    rope = w[:, :, MLA_NOPE:]
    rope_p = _pad_lanes(rope, LANES).reshape(Q_LORA, -1)
    rope_s = _pad_lanes(_pair_swap(rope), LANES).reshape(Q_LORA, -1)
    return jnp.concatenate([nope, rope_p, rope_s], axis=-1).astype(BF16)


def _prep_wkv(w_ukv):
    w = w_ukv.reshape(KV_LORA, MLA_HEADS, MLA_NOPE + MLA_V)
    return jnp.concatenate([w[:, :, :MLA_NOPE].reshape(KV_LORA, -1), w[:, :, MLA_NOPE:].reshape(KV_LORA, -1)], axis=-1).astype(BF16)


def kernel(x, c, ctx, c_ctx, w_ada, b_ada, w_in, mla_q_norm, w_uq, mla_kv_norm, w_ukv, hgrn_lb_logits, hgrn_norm,
           w_out, ln1_g, ln1_b, router_w, router_b, w1, b1, w2, b2, ln2_g, ln2_b):
    nb, seq, d = x.shape
    ctx_len = ctx.shape[1]
    depth = w_ada.shape[0]
    assert d == D_MODEL and seq % TM == 0 and ctx_len % TM == 0 and seq % GRID_W == 0
    r = seq + ctx_len
    tpb, nlt = r // TM, seq // TM
    n = nb * r
    n_tiles = n * TOP_K // TM + N_EXPERTS
    alpha = float((2 * depth) ** 0.25)
    geo = dict(tpb=tpb, nlt=nlt, nb=nb)

    xs = jnp.concatenate([x, ctx], axis=1).reshape(n, d)

    mod_rows = -(-(nb + 1) // 8) * 8
    cc = jnp.zeros((mod_rows, d), F32).at[:nb].set(c).at[nb].set(c_ctx)
    mod_all = _ada(cc, w_ada, b_ada)

    lb = jnp.cumsum(jax.nn.softmax(hgrn_lb_logits.astype(F32), axis=0), axis=0)
    lb = lb - lb[:1]
    lb_rows = jnp.stack([jnp.log(lb), jnp.log1p(-lb), 1.0 - lb] + [jnp.zeros_like(lb)] * 5, axis=2)

    dft_c = _channel_dft()
    dft_lat = _dft_pair(seq)
    dft_ctx = _dft_pair(ctx_len)
    cos_t, sin_t = _rope_tables(nb, seq, ctx_len)

    w_in_all = _prep_in_proj(w_in)
    wq_all = jax.vmap(_prep_wq)(w_uq)
    wkv_all = jax.vmap(_prep_wkv)(w_ukv)
    w_out_all = w_out.astype(BF16)
    w1g_all, w1l_all = _prep_w1(w1)
    le = depth * N_EXPERTS
    w2_all = w2.reshape(le, EXPERT_FF, d)
    b1g_all = b1[:, :, 0::2].reshape(le, 1, EXPERT_FF)
    b1l_all = b1[:, :, 1::2].reshape(le, 1, EXPERT_FF)
    b2_all = b2.reshape(le, 1, d)
    rw_all = _pad_lanes(router_w, LANES)
    rw_hi_all = rw_all.astype(BF16)
    rw_lo_all = (rw_all - rw_hi_all.astype(F32)).astype(BF16)

    for l in range(depth):
        row = lambda a: a[l].reshape(1, -1)
        z, hg, mla_in = _pre(xs, mod_all, w_in_all, dft_c, l, **geo)
        yf = _fourier(z, dft_lat, dft_ctx, nb=nb, tpb=tpb, nlt=nlt, seq=seq, ctx_len=ctx_len)
        o_f = _hgrn(hg, lb_rows[l, 0], z_block=3, reverse=False, **geo)
        o_b = _hgrn(hg, lb_rows[l, 1], z_block=4, reverse=True, **geo)
        q, k, v = _mla_up(mla_in, row(mla_q_norm), row(mla_kv_norm), wq_all, wkv_all, cos_t, sin_t, l)
        att = _attention(q, k, v, nb=nb, seq=seq, ctx_len=ctx_len)

        rb = jnp.full((1, LANES), NEG_BIG, F32).at[0, :N_EXPERTS].set(router_b[l])
        x1, h2p, eid, gates = _post(yf, o_f, o_b, hg, att, xs, mod_all, row(hgrn_norm), w_out_all, row(ln1_g), row(ln1_b),
                                    rw_hi_all[l], rw_lo_all[l], rb, l, alpha=alpha, **geo)

        pos, tile_e, n_used, fill_start, fill_n = _route(eid, n_tiles)
        xg = _dispatch(pos, fill_start, fill_n, n_used, h2p, n_tiles)
        ys = _moe_ffn(tile_e, n_used, xg, w1g_all, w1l_all, b1g_all, b1l_all, w2_all, b2_all, l)
        xs = _combine(pos, ys, gates, x1, mod_all, row(ln2_g), row(ln2_b), l, latent_only=l == depth - 1, alpha=alpha, **geo)

    return xs.reshape(nb, seq, d)
```

```python
import functools

import numpy as np
import jax
import jax.numpy as jnp
from jax import lax
from jax.experimental import pallas as pl
from jax.experimental.pallas import tpu as pltpu

F32 = jnp.float32
BF16 = jnp.bfloat16
U32 = jnp.uint32
I32 = jnp.int32

D_MODEL = 2048
FOURIER_W = 512
FOURIER_GROUP = 128
HGRN_W = 512
HGRN_HEADS = 4
HGRN_DIM = 128
MLA_HEADS = 8
MLA_NOPE = 128
MLA_ROPE = 64
MLA_V = 128
Q_LORA = 512
KV_LORA = 256
MLA_SCALE = (MLA_NOPE + MLA_ROPE) ** -0.5
N_EXPERTS = 32
TOP_K = 4
EXPERT_FF = 768
SWIGLU_ALPHA = 1.702
SWIGLU_LIMIT = 7.0
N_MOD = 6
GRID_W = 64
ROPE_BASE = 10000.0
EPS = 1e-6
LOG2_E = 1.4426950408889634

LANES = 128
SUBLANES = 8
TM = 256
TMM = 512
TQ = 2 * TM
HEAD_SLOT = 2 * LANES
HEADS_PER_STEP = 4
KEY_BLOCK = 256
IN_W = 4096
HG_OFF = FOURIER_W
MLA_OFF = HG_OFF + 5 * HGRN_W
MLA_IN_W = IN_W - MLA_OFF
N_LEVELS = 9
ADA_TN = 1024
W1_TN = 768
HALF = D_MODEL // 2
NEG_BIG = -1e30
ISSUE_UNROLL = 8
VMEM_LIMIT = 56 * 1024 * 1024


def _cparams(sem, vmem=VMEM_LIMIT):
    return pltpu.CompilerParams(dimension_semantics=sem, vmem_limit_bytes=vmem)


def _dot(a, b):
    return jnp.dot(a, b, preferred_element_type=F32)


def _dot_nt(a, b):
    return lax.dot_general(a, b, (((1,), (1,)), ((), ())), preferred_element_type=F32)


def _dot_tn(a, b):
    return lax.dot_general(a, b, (((0,), (0,)), ((), ())), preferred_element_type=F32)


def _split_dot(x, m):
    hi = x.astype(BF16)
    lo = (x - hi.astype(F32)).astype(BF16)
    return _dot(hi, m) + _dot(lo, m)


def _layer_norm(x):
    mu = jnp.mean(x, axis=-1, keepdims=True)
    xc = x - mu
    var = jnp.mean(xc * xc, axis=-1, keepdims=True)
    return xc * lax.rsqrt(var + EPS)


def _rms(x):
    return x * lax.rsqrt(jnp.mean(x * x, axis=-1, keepdims=True) + EPS)


def _sigmoid(x):
    return 1.0 / (1.0 + jnp.exp(-x))


def _mod_row(i, tpb, nlt, nb):
    return jnp.where(i % tpb >= nlt, nb, i // tpb)


def _mod(mod_ref, row, k):
    return mod_ref[pl.ds(row, 1), k * D_MODEL:(k + 1) * D_MODEL]


def _ada_kernel(c_ref, w_ref, b_ref, o_ref):
    c = c_ref[...]
    s = c * _sigmoid(c)
    hi = s.astype(BF16)
    lo = (s - hi.astype(F32)).astype(BF16)
    w = w_ref[...].astype(BF16)
    o_ref[...] = _dot(hi, w) + _dot(lo, w) + b_ref[...]


def _ada(cc, w_ada, b_ada):
    depth, d, n = w_ada.shape
    rows = cc.shape[0]
    return pl.pallas_call(
        _ada_kernel,
        grid=(depth, n // ADA_TN),
        in_specs=[
            pl.BlockSpec((rows, d), lambda l, j: (0, 0)),
            pl.BlockSpec((None, d, ADA_TN), lambda l, j: (l, 0, j)),
            pl.BlockSpec((None, 1, ADA_TN), lambda l, j: (l, 0, j)),
        ],
        out_specs=pl.BlockSpec((None, rows, ADA_TN), lambda l, j: (l, 0, j)),
        out_shape=jax.ShapeDtypeStruct((depth, rows, n), F32),
        compiler_params=_cparams(("arbitrary", "arbitrary")),
        name="ada",
    )(cc, w_ada, b_ada.reshape(depth, 1, n))


def _pre_kernel(x_ref, mod_ref, w_ref, dft_ref, z_ref, hg_ref, mla_ref, *, tpb, nlt, nb):
    row = _mod_row(pl.program_id(0), tpb, nlt, nb)
    h = _layer_norm(x_ref[...]) * (1.0 + _mod(mod_ref, row, 1)) + _mod(mod_ref, row, 0)
    r = _dot(h.astype(BF16), w_ref[...])
    z_ref[...] = _dot(r[:, 0:FOURIER_W].astype(BF16), dft_ref[...]).astype(BF16)
    hg_ref[...] = r[:, HG_OFF:MLA_OFF]
    mla_ref[...] = r[:, MLA_OFF:IN_W]


def _layer_block(a, layer, **kw):
    return pl.BlockSpec((None,) + a.shape[1:], lambda *_: (layer, 0, 0), **kw)


def _pre(x, mod_all, w_in_all, dft_c, layer, *, tpb, nlt, nb):
    n = x.shape[0]
    kern = functools.partial(_pre_kernel, tpb=tpb, nlt=nlt, nb=nb)
    return pl.pallas_call(
        kern,
        grid=(n // TM,),
        in_specs=[
            pl.BlockSpec((TM, D_MODEL), lambda i: (i, 0)),
            _layer_block(mod_all, layer),
            _layer_block(w_in_all, layer, pipeline_mode=pl.Buffered(1)),
            pl.BlockSpec(dft_c.shape, lambda i: (0, 0)),
        ],
        out_specs=[
            pl.BlockSpec((TM, 2 * FOURIER_W), lambda i: (i, 0)),
            pl.BlockSpec((TM, 5 * HGRN_W), lambda i: (i, 0)),
            pl.BlockSpec((TM, MLA_IN_W), lambda i: (i, 0)),
        ],
        out_shape=[
            jax.ShapeDtypeStruct((n, 2 * FOURIER_W), BF16),
            jax.ShapeDtypeStruct((n, 5 * HGRN_W), F32),
            jax.ShapeDtypeStruct((n, MLA_IN_W), F32),
        ],
        compiler_params=_cparams(("arbitrary",)),
        name="pre",
    )(x, mod_all, w_in_all, dft_c)


def _four_kernel(z_ref, cl_ref, sl_ref, cc_ref, sc_ref, y_ref, *, nlt, seq):
    j = pl.program_id(1)

    def emit(cm, sm, z):
        y = _dot(cm, z[:, 0:FOURIER_W]) - _dot(sm, z[:, FOURIER_W:])
        y_ref[...] = y.astype(BF16)

    @pl.when(j < nlt)
    def _():
        emit(cl_ref[...], sl_ref[...], z_ref[0:seq, :])

    @pl.when(j >= nlt)
    def _():
        emit(cc_ref[...], sc_ref[...], z_ref[seq:, :])


def _fourier(z, dft_lat, dft_ctx, *, nb, tpb, nlt, seq, ctx_len):
    n = z.shape[0]
    r = seq + ctx_len
    cl, sl = dft_lat
    cc, sc = dft_ctx
    kern = functools.partial(_four_kernel, nlt=nlt, seq=seq)
    lmap = lambda b, j: (jnp.minimum(j, nlt - 1), 0)
    cmap = lambda b, j: (jnp.maximum(j - nlt, 0), 0)
    return pl.pallas_call(
        kern,
        grid=(nb, tpb),
        in_specs=[
            pl.BlockSpec((None, r, 2 * FOURIER_W), lambda b, j: (b, 0, 0)),
            pl.BlockSpec((TM, seq), lmap),
            pl.BlockSpec((TM, seq), lmap),
            pl.BlockSpec((TM, ctx_len), cmap),
            pl.BlockSpec((TM, ctx_len), cmap),
        ],
        out_specs=pl.BlockSpec((TM, FOURIER_W), lambda b, j: (b * tpb + j, 0)),
        out_shape=jax.ShapeDtypeStruct((n, FOURIER_W), BF16),
        compiler_params=_cparams(("arbitrary", "arbitrary")),
        name="fourier",
    )(z.reshape(nb, r, 2 * FOURIER_W), cl, sl, cc, sc)


def _level_matrices(reverse):
    p = np.arange(TM)
    mats = [(p[None, :] <= p[:, None])]
    for lev in range(1, N_LEVELS):
        r = TM >> lev
        same = (p[None, :] // r) == (p[:, None] // r)
        is_q = ((p // r) % 2 == 1)[:, None]
        mats.append(same & np.where(is_q, p[None, :] <= p[:, None], p[None, :] > p[:, None]))
    g = np.stack(mats).astype(np.float32)
    if reverse:
        g = g[:, ::-1, ::-1]
    return jnp.asarray(g.reshape(N_LEVELS * TM, TM), dtype=BF16)


def _hgrn_kernel(q_ref, v_ref, z_ref, lb_ref, g_ref, o_ref, st_ref, *, reverse):
    @pl.when(pl.program_id(1) == 0)
    def _():
        st_ref[...] = jnp.zeros_like(st_ref)

    z = z_ref[...]
    log_lb = lb_ref[0:1, :]
    log_1m_lb = lb_ref[1:2, :]
    one_m_lb = lb_ref[2:3, :]
    log_sig = jnp.minimum(z, 0.0) - jnp.log1p(jnp.exp(-jnp.abs(z)))
    a2 = log_1m_lb + log_sig
    log_f = jnp.maximum(log_lb, a2) + jnp.log1p(jnp.exp(-jnp.abs(log_lb - a2)))
    kk = one_m_lb * _sigmoid(-z)

    wall = _split_dot_left(g_ref[...], log_f * LOG2_E)
    a = wall[0:TM]
    a_end = a[0:1] if reverse else a[TM - 1:TM]
    q = q_ref[...]
    v = v_ref[...].astype(BF16)
    qa = (q * jnp.exp2(a)).astype(BF16)
    kd = (kk * jnp.exp2(a_end - a)).astype(BF16)
    carry = jnp.exp2(a_end)

    half = TM // 2
    tiles = (slice(0, half), slice(half, TM))
    q_tile, k_tile = (0, 1) if reverse else (1, 0)
    row = lax.broadcasted_iota(I32, (half, half), 0)
    col = lax.broadcasted_iota(I32, (half, half), 1)
    diff = row ^ col
    pos = lax.broadcasted_iota(I32, (TM, HGRN_W), 0)
    if reverse:
        pos = TM - 1 - pos

    def is_q_piece(idx):
        return idx % 2 == (0 if reverse else 1)

    fac, fac_q, fac_k = {}, {}, {}
    for lev in range(1, N_LEVELS):
        r = TM >> lev
        fac[lev] = jnp.exp2(wall[lev * TM:(lev + 1) * TM])
        if r < SUBLANES:
            is_q = ((pos >> (r.bit_length() - 1)) & 1) == 1
            fac_q[lev] = jnp.where(is_q, fac[lev], 0.0)
            fac_k[lev] = jnp.where(is_q, 0.0, fac[lev])

    def level_operands(lev, qh, kh, sl, rows):
        r = TM >> lev
        if r < SUBLANES:
            return (qh[rows] * fac_q[lev][rows, sl]).astype(BF16), (kh[rows] * fac_k[lev][rows, sl]).astype(BF16)
        qs, ks = [], []
        for start in range(rows.start, rows.stop, r):
            piece = slice(start, start + r)
            zero = jnp.zeros((r, HGRN_DIM), F32)
            if is_q_piece(start // r):
                qs.append(qh[piece] * fac[lev][piece, sl])
                ks.append(zero)
            else:
                qs.append(zero)
                ks.append(kh[piece] * fac[lev][piece, sl])
        cat = lambda xs: (xs[0] if len(xs) == 1 else jnp.concatenate(xs, axis=0)).astype(BF16)
        return cat(qs), cat(ks)

    heads = [slice(h * HGRN_DIM, (h + 1) * HGRN_DIM) for h in range(HGRN_HEADS)]
    p = {}
    for h, sl in enumerate(heads):
        for d, rows in enumerate(tiles):
            p[h, d] = jnp.where(diff == 0, _dot_nt(q[rows, sl].astype(BF16), kk[rows, sl].astype(BF16)), 0.0)
    for lev in range(2, N_LEVELS):
        block = 2 * (TM >> lev)
        for h, sl in enumerate(heads):
            for d, rows in enumerate(tiles):
                qt, kt = level_operands(lev, q[:, sl], kk[:, sl], sl, rows)
                sc = _dot_nt(qt, kt)
                p[h, d] = p[h, d] + (sc if block == half else jnp.where((diff >> (block.bit_length() - 1)) == 0, sc, 0.0))

    for h, sl in enumerate(heads):
        st = st_ref[h]
        qh, kh, vh = q[:, sl], kk[:, sl], v[:, sl]
        p_off = _dot_nt((qh[tiles[q_tile]] * fac[1][tiles[q_tile], sl]).astype(BF16),
                        (kh[tiles[k_tile]] * fac[1][tiles[k_tile], sl]).astype(BF16))
        inter = _dot_nt(qa[:, sl], st.astype(BF16))
        for d, rows in enumerate(tiles):
            o = inter[rows] + _dot(p[h, d].astype(BF16), vh[rows])
            if d == q_tile:
                o = o + _dot(p_off.astype(BF16), vh[tiles[k_tile]])
            o_ref[rows, sl] = o
        st_ref[h] = st * carry[:, sl] + _dot_tn(vh, kd[:, sl])


def _split_dot_left(m, x):
    hi = x.astype(BF16)
    lo = (x - hi.astype(F32)).astype(BF16)
    return _dot(m, hi) + _dot(m, lo)


def _hgrn(hg, lb_rows, *, z_block, reverse, nb, tpb, nlt):
    n = hg.shape[0]
    nct = tpb - nlt

    def chunk(b, s):
        if reverse:
            j = jnp.where(s < nct, tpb - 1 - s, nlt - 1 - (s - nct))
        else:
            j = jnp.where(s < nct, nlt + s, s - nct)
        return b * tpb + j

    kern = functools.partial(_hgrn_kernel, reverse=reverse)
    gmat = _level_matrices(reverse)
    return pl.pallas_call(
        kern,
        grid=(nb, tpb),
        in_specs=[
            pl.BlockSpec((TM, HGRN_W), lambda b, s: (chunk(b, s), 0)),
            pl.BlockSpec((TM, HGRN_W), lambda b, s: (chunk(b, s), 1)),
            pl.BlockSpec((TM, HGRN_W), lambda b, s: (chunk(b, s), z_block)),
            pl.BlockSpec((8, HGRN_W), lambda b, s: (0, 0)),
            pl.BlockSpec(gmat.shape, lambda b, s: (0, 0)),
        ],
        out_specs=pl.BlockSpec((TM, HGRN_W), lambda b, s: (chunk(b, s), 0)),
        out_shape=jax.ShapeDtypeStruct((n, HGRN_W), F32),
        scratch_shapes=[pltpu.VMEM((HGRN_HEADS, HGRN_DIM, HGRN_DIM), F32)],
        compiler_params=_cparams(("arbitrary", "arbitrary")),
        name="hgrn_bwd" if reverse else "hgrn_fwd",
    )(hg, hg, hg, lb_rows, gmat)


def _mla_kernel(m_ref, qg_ref, kvg_ref, wq_ref, wkv_ref, cos_ref, sin_ref, q_ref, k_ref, v_ref):
    cq = m_ref[:, 0:Q_LORA]
    ckv = m_ref[:, Q_LORA:Q_LORA + KV_LORA]
    kr = m_ref[:, Q_LORA + KV_LORA:Q_LORA + KV_LORA + LANES]
    kr_sw = m_ref[:, Q_LORA + KV_LORA + LANES:MLA_IN_W]
    cos = cos_ref[...]
    sin = sin_ref[...]
    qall = _dot((_rms(cq) * qg_ref[...]).astype(BF16), wq_ref[...]) * (MLA_SCALE * LOG2_E)
    kv = _dot((_rms(ckv) * kvg_ref[...]).astype(BF16), wkv_ref[...])
    k_rot = (kr * cos + kr_sw * sin).astype(BF16)
    ones = jnp.ones((TM, LANES), BF16)
    nw = MLA_HEADS * LANES
    for h in range(MLA_HEADS):
        a, b = h * LANES, (h + 1) * LANES
        s = h * HEAD_SLOT
        q_ref[:, s:s + LANES] = qall[:, a:b].astype(BF16)
        q_ref[:, s + LANES:s + HEAD_SLOT] = (qall[:, nw + a:nw + b] * cos + qall[:, 2 * nw + a:2 * nw + b] * sin).astype(BF16)
        k_ref[:, s:s + LANES] = kv[:, a:b].astype(BF16)
        k_ref[:, s + LANES:s + HEAD_SLOT] = k_rot
        v_ref[:, s:s + LANES] = kv[:, nw + a:nw + b].astype(BF16)
        v_ref[:, s + LANES:s + HEAD_SLOT] = ones


def _mla_up(mla_in, q_gain, kv_gain, wq_all, wkv_all, cos_t, sin_t, layer):
    n = mla_in.shape[0]
    w = MLA_HEADS * HEAD_SLOT
    full = lambda a: pl.BlockSpec(a.shape, lambda i: (0, 0))
    return pl.pallas_call(
        _mla_kernel,
        grid=(n // TM,),
        in_specs=[
            pl.BlockSpec((TM, MLA_IN_W), lambda i: (i, 0)),
            full(q_gain), full(kv_gain), _layer_block(wq_all, layer), _layer_block(wkv_all, layer),
            pl.BlockSpec((TM, LANES), lambda i: (i, 0)),
            pl.BlockSpec((TM, LANES), lambda i: (i, 0)),
        ],
        out_specs=[pl.BlockSpec((TM, w), lambda i: (i, 0))] * 3,
        out_shape=[jax.ShapeDtypeStruct((n, w), BF16)] * 3,
        compiler_params=_cparams(("arbitrary",)),
        name="mla_up",
    )(mla_in, q_gain, kv_gain, wq_all, wkv_all, cos_t, sin_t)


def _attn_kernel(q_ref, k_ref, v_ref, o_ref, s_ref, *, n_lat_steps, seq, ctx_len):
    j = pl.program_id(2)

    def run(rows, k0, k1):
        blocks = [(kb, slice(k0 + kb * KEY_BLOCK, k0 + (kb + 1) * KEY_BLOCK), slice(kb * KEY_BLOCK, (kb + 1) * KEY_BLOCK))
                  for kb in range((k1 - k0) // KEY_BLOCK)]
        hslot = lambda hh: slice(hh * HEAD_SLOT, (hh + 1) * HEAD_SLOT)

        def scores(hh, blk, m_run):
            _, ks, ss = blk
            s = _dot_nt(q_ref[0:rows, hslot(hh)], k_ref[ks, hslot(hh)])
            s_ref[hh % 2, 0:rows, ss] = s
            for c in range(KEY_BLOCK // LANES):
                m_run = jnp.maximum(m_run, s[:, c * LANES:(c + 1) * LANES])
            return m_run

        def values(hh, blk, m, o):
            _, ks, ss = blk
            d = (s_ref[hh % 2, 0:rows, ss] - m).astype(BF16)
            return o + _dot(jnp.exp2(d), v_ref[ks, hslot(hh)])

        def emit(hh, o):
            o_ref[0:rows, hh * MLA_V:(hh + 1) * MLA_V] = (o[:, 0:MLA_V] / o[:, MLA_V:]).astype(BF16)

        neg = jnp.full((rows, LANES), -jnp.inf, F32)
        zero = jnp.zeros((rows, HEAD_SLOT), F32)
        m_run = neg
        for blk in blocks:
            m_run = scores(0, blk, m_run)
        m_prev = jnp.max(m_run, axis=-1, keepdims=True)
        for hh in range(1, HEADS_PER_STEP):
            m_run, o = neg, zero
            for blk in blocks:
                m_run = scores(hh, blk, m_run)
                o = values(hh - 1, blk, m_prev, o)
            emit(hh - 1, o)
            m_prev = jnp.max(m_run, axis=-1, keepdims=True)
        o = zero
        for blk in blocks:
            o = values(HEADS_PER_STEP - 1, blk, m_prev, o)
        emit(HEADS_PER_STEP - 1, o)

    @pl.when(j < n_lat_steps)
    def _():
        run(TQ, 0, seq + ctx_len)

    @pl.when(j >= n_lat_steps)
    def _():
        run(ctx_len, seq, seq + ctx_len)


def _attention(q, k, v, *, nb, seq, ctx_len):
    n = q.shape[0]
    r = seq + ctx_len
    assert seq % TQ == 0 and ctx_len <= TQ
    n_lat_steps = seq // TQ
    hp = MLA_HEADS // HEADS_PER_STEP
    wq = HEADS_PER_STEP * HEAD_SLOT
    wo = HEADS_PER_STEP * MLA_V
    kern = functools.partial(_attn_kernel, n_lat_steps=n_lat_steps, seq=seq, ctx_len=ctx_len)
    out = pl.pallas_call(
        kern,
        grid=(nb, hp, n_lat_steps + 1),
        in_specs=[
            pl.BlockSpec((None, TQ, wq), lambda b, h, j: (b, j, h)),
            pl.BlockSpec((None, r, wq), lambda b, h, j: (b, 0, h), pipeline_mode=pl.Buffered(1)),
            pl.BlockSpec((None, r, wq), lambda b, h, j: (b, 0, h), pipeline_mode=pl.Buffered(1)),
        ],
        out_specs=pl.BlockSpec((None, TQ, wo), lambda b, h, j: (b, j, h)),
        out_shape=jax.ShapeDtypeStruct((nb, r, MLA_HEADS * MLA_V), BF16),
        scratch_shapes=[pltpu.VMEM((2, TQ, r), F32)],
        compiler_params=_cparams(("arbitrary", "arbitrary", "arbitrary")),
        name="attention",
    )(q.reshape(nb, r, -1), k.reshape(nb, r, -1), v.reshape(nb, r, -1))
    return out.reshape(n, MLA_HEADS * MLA_V)


def _pack_rows(x):
    w = x.shape[1] // 2
    lo = pltpu.bitcast(x[:, :w].astype(BF16).astype(F32), U32)
    hi = pltpu.bitcast(x[:, w:].astype(BF16).astype(F32), U32)
    return (lo >> 16) | (hi & jnp.uint32(0xFFFF0000))


def _unpack_rows(w):
    lo = pltpu.bitcast(w << 16, F32)
    hi = pltpu.bitcast(w & jnp.uint32(0xFFFF0000), F32)
    return lo, hi


def _post_kernel(yf_ref, of_ref, ob_ref, g_ref, att_ref, x_ref, mod_ref, hn_ref, wo_ref, lng_ref, lnb_ref,
                 rw_ref, rb_ref, x1_ref, h2_ref, eid_ref, gate_ref, *, tpb, nlt, nb, alpha):
    row = _mod_row(pl.program_id(0), tpb, nlt, nb)
    o = of_ref[...] + ob_ref[...]
    hn = hn_ref[...]
    gate_act = g_ref[...]
    gate_act = gate_act * _sigmoid(gate_act)
    mix = _dot(yf_ref[...], wo_ref[0:FOURIER_W, :])
    for h in range(HGRN_HEADS):
        sl = slice(h * HGRN_DIM, (h + 1) * HGRN_DIM)
        r = (_rms(o[:, sl]) * hn[:, sl] * gate_act[:, sl]).astype(BF16)
        mix = mix + _dot(r, wo_ref[FOURIER_W + h * HGRN_DIM:FOURIER_W + (h + 1) * HGRN_DIM, :])
    mix = mix + _dot(att_ref[...], wo_ref[FOURIER_W + HGRN_W:, :])

    x1 = _layer_norm(alpha * x_ref[...] + _mod(mod_ref, row, 2) * mix) * lng_ref[...] + lnb_ref[...]
    x1_ref[...] = x1
    h2 = _layer_norm(x1) * (1.0 + _mod(mod_ref, row, 4)) + _mod(mod_ref, row, 3)
    h2_ref[...] = _pack_rows(h2)

    hi = h2.astype(BF16)
    lo = (h2 - hi.astype(F32)).astype(BF16)
    rr = _dot(jnp.concatenate([hi, lo], axis=0), rw_ref[...])
    logits = rr[0:TM, 0:LANES] + rr[0:TM, LANES:] + rr[TM:, 0:LANES] + rr[TM:, LANES:] + rb_ref[...]
    lane = lax.broadcasted_iota(I32, logits.shape, 1)
    cur = logits
    vals, ids = [], []
    for _ in range(TOP_K):
        m = jnp.max(cur, axis=-1, keepdims=True)
        idx = jnp.min(jnp.where(cur == m, lane, LANES), axis=-1, keepdims=True)
        vals.append(m)
        ids.append(idx)
        cur = jnp.where(lane == idx, -3e38, cur)
    ex = [jnp.exp(vk - vals[0]) for vk in vals]
    den = ex[0] + ex[1] + ex[2] + ex[3]
    eid = jnp.zeros(logits.shape, I32)
    gate = jnp.zeros(logits.shape, F32)
    for k in range(TOP_K):
        eid = jnp.where(lane == k, ids[k], eid)
        gate = jnp.where(lane == k, ex[k] / den, gate)
    eid_ref[...] = eid
    gate_ref[...] = gate


def _post(yf, o_f, o_b, hg, att, x, mod_all, hnorm, w_out_all, ln_g, ln_b, rw_all, rb, layer, *, tpb, nlt, nb, alpha):
    n = x.shape[0]
    kern = functools.partial(_post_kernel, tpb=tpb, nlt=nlt, nb=nb, alpha=alpha)
    full = lambda a: pl.BlockSpec(a.shape, lambda i: (0, 0))
    return pl.pallas_call(
        kern,
        grid=(n // TM,),
        in_specs=[
            pl.BlockSpec((TM, FOURIER_W), lambda i: (i, 0)),
            pl.BlockSpec((TM, HGRN_W), lambda i: (i, 0)),
            pl.BlockSpec((TM, HGRN_W), lambda i: (i, 0)),
            pl.BlockSpec((TM, HGRN_W), lambda i: (i, 2)),
            pl.BlockSpec((TM, MLA_HEADS * MLA_V), lambda i: (i, 0)),
            pl.BlockSpec((TM, D_MODEL), lambda i: (i, 0)),
            _layer_block(mod_all, layer), full(hnorm),
            _layer_block(w_out_all, layer, pipeline_mode=pl.Buffered(1)),
            full(ln_g), full(ln_b), _layer_block(rw_all, layer), full(rb),
        ],
        out_specs=[
            pl.BlockSpec((TM, D_MODEL), lambda i: (i, 0)),
            pl.BlockSpec((TM, HALF), lambda i: (i, 0)),
            pl.BlockSpec((TM, LANES), lambda i: (i, 0)),
            pl.BlockSpec((TM, LANES), lambda i: (i, 0)),
        ],
        out_shape=[
            jax.ShapeDtypeStruct((n, D_MODEL), F32),
            jax.ShapeDtypeStruct((n, HALF), U32),
            jax.ShapeDtypeStruct((n, LANES), I32),
            jax.ShapeDtypeStruct((n, LANES), F32),
        ],
        compiler_params=_cparams(("arbitrary",)),
        name="post",
    )(yf, o_f, o_b, hg, att, x, mod_all, hnorm, w_out_all, ln_g, ln_b, rw_all, rb)


def _route_kernel(eid_ref, pos_ref, te_ref, meta_ref, base_ref, cnt_ref, *, n_tiles_pad):
    ph = pl.program_id(0)
    i = pl.program_id(1)
    lane = lax.broadcasted_iota(I32, (TM, LANES), 1)
    eid = eid_ref[...]
    onehot = [(lane == eid[:, k:k + 1]).astype(F32) for k in range(TOP_K)]
    colsum = [jnp.sum(oh, axis=0, keepdims=True) for oh in onehot]

    @pl.when(ph == 0)
    def _():
        @pl.when(i == 0)
        def _():
            cnt_ref[...] = jnp.zeros_like(cnt_ref)

        base_ref[pl.ds(i, 1), :] = cnt_ref[0:1, :]
        cnt_ref[0:1, :] = cnt_ref[0:1, :] + colsum[0] + colsum[1] + colsum[2] + colsum[3]

    @pl.when(ph == 1)
    def _():
        cnt = cnt_ref[0:1, :]
        tiles = jnp.floor((cnt + (TMM - 1)) * (1.0 / TMM))
        r128 = lax.broadcasted_iota(I32, (LANES, LANES), 0)
        c128 = lax.broadcasted_iota(I32, (LANES, LANES), 1)
        incl = (r128 <= c128).astype(BF16)
        pad_end = _split_dot(jnp.broadcast_to(tiles, (8, LANES)), incl)[0:1, :] * TMM
        pad_start = pad_end - tiles * TMM

        rr = lax.broadcasted_iota(I32, (TM, TM), 0)
        cc = lax.broadcasted_iota(I32, (TM, TM), 1)
        strict = (cc < rr).astype(BF16)
        run = pad_start + base_ref[pl.ds(i, 1), :]
        pos = jnp.zeros((TM, LANES), I32)
        for k in range(TOP_K):
            before = _dot(strict, onehot[k].astype(BF16))
            val = jnp.sum(onehot[k] * (before + run), axis=1, keepdims=True)
            pos = jnp.where(lane == k, val.astype(I32), pos)
            run = run + colsum[k]
        pos_ref[...] = pos

        @pl.when(i == 0)
        def _():
            tstart = (lax.broadcasted_iota(I32, (n_tiles_pad, LANES), 0) * TMM).astype(F32)
            lane_t = lax.broadcasted_iota(I32, (n_tiles_pad, LANES), 1)
            done = jnp.where((pad_end <= tstart) & (lane_t < N_EXPERTS), 1.0, 0.0)
            te = jnp.minimum(jnp.sum(done, axis=1, keepdims=True), N_EXPERTS - 1.0)
            te_ref[...] = jnp.broadcast_to(te, (n_tiles_pad, LANES)).astype(I32)
            sub = lax.broadcasted_iota(I32, (8, LANES), 0)
            meta = jnp.where(sub == 0, pad_start + cnt, jnp.where(sub == 1, pad_end - pad_start - cnt, pad_end))
            meta_ref[...] = meta.astype(I32)


def _route(eid, n_tiles):
    n = eid.shape[0]
    nt = n // TM
    n_tiles_pad = -(-n_tiles // 8) * 8
    kern = functools.partial(_route_kernel, n_tiles_pad=n_tiles_pad)
    pos, te, meta = pl.pallas_call(
        kern,
        grid=(2, nt),
        in_specs=[pl.BlockSpec((TM, LANES), lambda ph, i: (i, 0))],
        out_specs=[
            pl.BlockSpec((TM, LANES), lambda ph, i: (ph * i, 0)),
            pl.BlockSpec((n_tiles_pad, LANES), lambda ph, i: (0, 0)),
            pl.BlockSpec((8, LANES), lambda ph, i: (0, 0)),
        ],
        out_shape=[
            jax.ShapeDtypeStruct((n, LANES), I32),
            jax.ShapeDtypeStruct((n_tiles_pad, LANES), I32),
            jax.ShapeDtypeStruct((8, LANES), I32),
        ],
        scratch_shapes=[pltpu.VMEM((-(-nt // 8) * 8, LANES), F32), pltpu.VMEM((8, LANES), F32)],
        compiler_params=_cparams(("arbitrary", "arbitrary")),
        name="moe_route",
    )(eid)
    pos_flat = pos[:, :TOP_K].reshape(-1)
    tile_e = te[:n_tiles, 0]
    fill_start = meta[0, :N_EXPERTS]
    fill_n = meta[1, :N_EXPERTS]
    n_used = meta[2, N_EXPERTS - 1:N_EXPERTS] // TMM
    return pos_flat, tile_e, n_used, fill_start, fill_n


def _dispatch_kernel(pos_ref, fs_ref, fn_ref, nu_ref, h_ref, o_hbm, zero_ref, sem, fill_sem, *, n_tiles):
    i = pl.program_id(0)

    def start(r, c):
        for k in range(TOP_K):
            p = pos_ref[(i * TM + r) * TOP_K + k]
            pltpu.make_async_copy(h_ref.at[pl.ds(r, 1), :], o_hbm.at[pl.ds(p, 1), :], sem).start()
        return c

    lax.fori_loop(0, TM, start, 0, unroll=ISSUE_UNROLL)

    @pl.when(i == 0)
    def _():
        zero_ref[...] = jnp.zeros_like(zero_ref)

        def fill_row(row):
            return pltpu.make_async_copy(zero_ref.at[pl.ds(0, 1), :], o_hbm.at[pl.ds(row, 1), :], fill_sem)

        def fill_tile(t):
            return pltpu.make_async_copy(zero_ref, o_hbm.at[pl.ds(pl.multiple_of(t * TMM, TMM), TMM), :], fill_sem)

        def each(fn):
            for e in range(N_EXPERTS):
                lax.fori_loop(0, fn_ref[e], lambda t, c, e=e: (fn(fill_row(fs_ref[e] + t)), c)[1], 0)
            lax.fori_loop(nu_ref[0], n_tiles, lambda t, c: (fn(fill_tile(t)), c)[1], 0)

        each(lambda cp: cp.start())
        each(lambda cp: cp.wait())

    for _ in range(TOP_K):
        pltpu.make_async_copy(h_ref, o_hbm.at[pl.ds(0, TM), :], sem).wait()


def _dispatch(pos_flat, fill_start, fill_n, n_used, h2p, n_tiles):
    n = h2p.shape[0]
    return pl.pallas_call(
        functools.partial(_dispatch_kernel, n_tiles=n_tiles),
        grid_spec=pltpu.PrefetchScalarGridSpec(
            num_scalar_prefetch=4,
            grid=(n // TM,),
            in_specs=[pl.BlockSpec((TM, HALF), lambda i, p, fs, fn, nu: (i, 0))],
            out_specs=pl.BlockSpec(memory_space=pl.ANY),
            scratch_shapes=[pltpu.VMEM((TMM, HALF), U32), pltpu.SemaphoreType.DMA(()), pltpu.SemaphoreType.DMA(())],
        ),
        out_shape=jax.ShapeDtypeStruct((n_tiles * TMM, HALF), U32),
        compiler_params=_cparams(("arbitrary",)),
        name="moe_dispatch",
    )(pos_flat, fill_start, fill_n, n_used, h2p)


def _w1_kernel(w_ref, g_ref, l_ref, t_ref):
    for c in range(D_MODEL // LANES):
        cs = slice(c * LANES, (c + 1) * LANES)
        t_ref[c] = w_ref[cs, :].T
        g_ref[:, cs] = t_ref[c, pl.ds(0, W1_TN // 2, stride=2), :].astype(BF16)
        l_ref[:, cs] = t_ref[c, pl.ds(1, W1_TN // 2, stride=2), :].astype(BF16)


def _prep_w1(w1):
    depth, ne, d, n2 = w1.shape
    out = jax.ShapeDtypeStruct((depth * ne, n2 // 2, d), BF16)
    g, l = pl.pallas_call(
        _w1_kernel,
        grid=(depth * ne, n2 // W1_TN),
        in_specs=[pl.BlockSpec((None, d, W1_TN), lambda e, j: (e, 0, j))],
        out_specs=[pl.BlockSpec((None, W1_TN // 2, d), lambda e, j: (e, j, 0))] * 2,
        out_shape=[out, out],
        scratch_shapes=[pltpu.VMEM((d // LANES, W1_TN, LANES), F32)],
        compiler_params=_cparams(("arbitrary", "arbitrary")),
        name="moe_w1_prep",
    )(w1.reshape(depth * ne, d, n2))
    return g, l


def _moe_kernel(te_ref, nu_ref, x_ref, w1g_ref, w1l_ref, b1g_ref, b1l_ref, w2_ref, b2_ref, y_ref, w2b_ref):
    i = pl.program_id(0)

    @pl.when(i < nu_ref[0])
    def _():
        @pl.when((i == 0) | (te_ref[i] != te_ref[jnp.maximum(i - 1, 0)]))
        def _():
            w2b_ref[...] = w2_ref[...].astype(BF16)

        lo, hi = _unpack_rows(x_ref[...])
        lo = lo.astype(BF16)
        hi = hi.astype(BF16)
        ug = _dot_nt(lo, w1g_ref[:, 0:HALF]) + _dot_nt(hi, w1g_ref[:, HALF:]) + b1g_ref[...]
        ul = _dot_nt(lo, w1l_ref[:, 0:HALF]) + _dot_nt(hi, w1l_ref[:, HALF:]) + b1l_ref[...]
        xg = jnp.minimum(ug, SWIGLU_LIMIT)
        xl = jnp.clip(ul, -SWIGLU_LIMIT, SWIGLU_LIMIT)
        act = xg * _sigmoid(SWIGLU_ALPHA * xg) * (xl + 1.0)
        y = _dot(act.astype(BF16), w2b_ref[...]) + b2_ref[...]
        y_ref[...] = _pack_rows(y)

    @pl.when(i >= nu_ref[0])
    def _():
        y_ref[...] = jnp.zeros_like(y_ref)


def _moe_ffn(tile_e, n_used, xs, w1g, w1l, b1g, b1l, w2, b2, layer):
    m = xs.shape[0]
    emap = lambda i, te, nu: (layer * N_EXPERTS + te[i], 0, 0)
    return pl.pallas_call(
        _moe_kernel,
        grid_spec=pltpu.PrefetchScalarGridSpec(
            num_scalar_prefetch=2,
            grid=(m // TMM,),
            in_specs=[
                pl.BlockSpec((TMM, HALF), lambda i, te, nu: (i, 0)),
                pl.BlockSpec((None, EXPERT_FF, D_MODEL), emap),
                pl.BlockSpec((None, EXPERT_FF, D_MODEL), emap),
                pl.BlockSpec((None, 1, EXPERT_FF), emap),
                pl.BlockSpec((None, 1, EXPERT_FF), emap),
                pl.BlockSpec((None, EXPERT_FF, D_MODEL), emap),
                pl.BlockSpec((None, 1, D_MODEL), emap),
            ],
            out_specs=pl.BlockSpec((TMM, HALF), lambda i, te, nu: (i, 0)),
            scratch_shapes=[pltpu.VMEM((EXPERT_FF, D_MODEL), BF16)],
        ),
        out_shape=jax.ShapeDtypeStruct((m, HALF), U32),
        compiler_params=_cparams(("arbitrary",)),
        name="moe_ffn",
    )(tile_e, n_used, xs, w1g, w1l, b1g, b1l, w2, b2)


def _combine_kernel(pos_ref, y_hbm, gate_ref, x_ref, mod_ref, lng_ref, lnb_ref, o_ref, buf, sem, *, tpb, nlt, nb, alpha,
                    latent_only):
    i = pl.program_id(0)
    n_steps = pl.num_programs(0)
    row = _mod_row(i, tpb, nlt, nb)

    def issue(tile, slot):
        def start(r, c):
            for k in range(TOP_K):
                p = pos_ref[(tile * TM + r) * TOP_K + k]
                pltpu.make_async_copy(y_hbm.at[pl.ds(p, 1), :], buf.at[slot, k, pl.ds(r, 1), :], sem.at[slot]).start()
            return c

        lax.fori_loop(0, TM, start, 0, unroll=ISSUE_UNROLL)

    @pl.when(i == 0)
    def _():
        issue(0, 0)

    @pl.when(i + 1 < n_steps)
    def _():
        issue(i + 1, (i + 1) % 2)

    slot = i % 2
    for k in range(TOP_K):
        pltpu.make_async_copy(y_hbm.at[pl.ds(0, TM), :], buf.at[slot, k], sem.at[slot]).wait()

    gates = gate_ref[...]
    acc_lo = jnp.zeros((TM, HALF), F32)
    acc_hi = jnp.zeros((TM, HALF), F32)
    for k in range(TOP_K):
        lo, hi = _unpack_rows(buf[slot, k])
        gk = gates[:, k:k + 1]
        acc_lo = acc_lo + gk * lo
        acc_hi = acc_hi + gk * hi
    ffn = jnp.concatenate([acc_lo, acc_hi], axis=-1)
    x2 = _layer_norm(alpha * x_ref[...] + _mod(mod_ref, row, 5) * ffn) * lng_ref[...] + lnb_ref[...]
    if latent_only:
        @pl.when(i % tpb < nlt)
        def _():
            o_ref[...] = x2
    else:
        o_ref[...] = x2


def _combine(pos, ys, gates, x, mod_all, ln_g, ln_b, layer, *, latent_only, tpb, nlt, nb, alpha):
    n = x.shape[0]
    kern = functools.partial(_combine_kernel, tpb=tpb, nlt=nlt, nb=nb, alpha=alpha, latent_only=latent_only)
    full = lambda a: pl.BlockSpec(a.shape, lambda i, p: (0, 0))
    if latent_only:
        omap = lambda i, p: ((i // tpb) * nlt + jnp.minimum(i % tpb, nlt - 1), 0)
        n_out = nb * nlt * TM
    else:
        omap = lambda i, p: (i, 0)
        n_out = n
    return pl.pallas_call(
        kern,
        grid_spec=pltpu.PrefetchScalarGridSpec(
            num_scalar_prefetch=1,
            grid=(n // TM,),
            in_specs=[
                pl.BlockSpec(memory_space=pl.ANY),
                pl.BlockSpec((TM, LANES), lambda i, p: (i, 0)),
                pl.BlockSpec((TM, D_MODEL), lambda i, p: (i, 0)),
                _layer_block(mod_all, layer), full(ln_g), full(ln_b),
            ],
            out_specs=pl.BlockSpec((TM, D_MODEL), omap),
            scratch_shapes=[pltpu.VMEM((2, TOP_K, TM, HALF), U32), pltpu.SemaphoreType.DMA((2,))],
        ),
        out_shape=jax.ShapeDtypeStruct((n_out, D_MODEL), F32),
        compiler_params=_cparams(("arbitrary",)),
        name="moe_combine",
    )(pos, ys, gates, x, mod_all, ln_g, ln_b)


def _dft_pair(t_len):
    assert t_len % GRID_W == 0
    k = jnp.arange(t_len, dtype=I32)

    def table(t):
        ang = ((t[:, None] * k[None, :]) % t_len).astype(F32) * (2.0 * np.pi / t_len)
        return jnp.cos(ang), jnp.sin(ang)

    ca, sa = table(jnp.arange(t_len // GRID_W, dtype=I32) * GRID_W)
    cb, sb = table(jnp.arange(GRID_W, dtype=I32))
    s = 1.0 / np.sqrt(t_len)
    cos = ca[:, None, :] * cb[None, :, :] - sa[:, None, :] * sb[None, :, :]
    sin = sa[:, None, :] * cb[None, :, :] + ca[:, None, :] * sb[None, :, :]
    return (cos * s).reshape(t_len, t_len).astype(BF16), (sin * s).reshape(t_len, t_len).astype(BF16)


def _channel_dft():
    c = np.arange(FOURIER_GROUP)
    ang = 2.0 * np.pi * ((c[:, None] * c[None, :]) % FOURIER_GROUP) / FOURIER_GROUP
    s = 1.0 / np.sqrt(FOURIER_GROUP)
    groups = FOURIER_W // FOURIER_GROUP
    m = np.zeros((FOURIER_W, 2 * FOURIER_W), np.float32)
    for g in range(groups):
        a, b = g * FOURIER_GROUP, (g + 1) * FOURIER_GROUP
        m[a:b, a:b] = np.cos(ang) * s
        m[a:b, FOURIER_W + a:FOURIER_W + b] = np.sin(ang) * s
    return jnp.asarray(m, dtype=BF16)


def _pair_swap(w):
    pairs = w.reshape(*w.shape[:-1], w.shape[-1] // 2, 2)
    return jnp.stack([-pairs[..., 1], pairs[..., 0]], axis=-1).reshape(w.shape)


def _pad_lanes(w, width):
    return jnp.pad(w, [(0, 0)] * (w.ndim - 1) + [(0, width - w.shape[-1])])


def _rope_tables(nb, seq, ctx_len):
    t = jnp.arange(seq)
    n_freq = MLA_ROPE // 4
    inv = ROPE_BASE ** (-jnp.arange(n_freq, dtype=F32) / n_freq)
    ang = jnp.concatenate([(t // GRID_W)[:, None] * inv, (t % GRID_W)[:, None] * inv], axis=-1)
    ang = jnp.repeat(ang, 2, axis=-1)
    cos = jnp.concatenate([jnp.cos(ang), jnp.ones((ctx_len, MLA_ROPE), F32)], axis=0)
    sin = jnp.concatenate([jnp.sin(ang), jnp.zeros((ctx_len, MLA_ROPE), F32)], axis=0)
    cos = jnp.tile(_pad_lanes(cos, LANES), (nb, 1))
    sin = jnp.tile(_pad_lanes(sin, LANES), (nb, 1))
    return cos, sin


def _prep_in_proj(w_in):
    f, q, i, g, zf, zb, cq, ckv, kr = jnp.split(w_in, np.cumsum([512, 512, 512, 512, 512, 512, 512, 256]).tolist(), axis=-1)
    cols = [f, q, i, g, zf, zb, cq, ckv, _pad_lanes(kr, LANES), _pad_lanes(_pair_swap(kr), LANES)]
    return jnp.concatenate(cols, axis=-1).astype(BF16)


def _prep_wq(w_uq):
    w = w_uq.reshape(Q_LORA, MLA_HEADS, MLA_NOPE + MLA_ROPE)
    nope = w[:, :, :MLA_NOPE].reshape(Q_LORA, -1)
    rope = w[:, :, MLA_NOPE:]
    rope_p = _pad_lanes(rope, LANES).reshape(Q_LORA, -1)
    rope_s = _pad_lanes(_pair_swap(rope), LANES).reshape(Q_LORA, -1)
    return jnp.concatenate([nope, rope_p, rope_s], axis=-1).astype(BF16)


def _prep_wkv(w_ukv):
    w = w_ukv.reshape(KV_LORA, MLA_HEADS, MLA_NOPE + MLA_V)
    return jnp.concatenate([w[:, :, :MLA_NOPE].reshape(KV_LORA, -1), w[:, :, MLA_NOPE:].reshape(KV_LORA, -1)], axis=-1).astype(BF16)


def kernel(x, c, ctx, c_ctx, w_ada, b_ada, w_in, mla_q_norm, w_uq, mla_kv_norm, w_ukv, hgrn_lb_logits, hgrn_norm,
           w_out, ln1_g, ln1_b, router_w, router_b, w1, b1, w2, b2, ln2_g, ln2_b):
    nb, seq, d = x.shape
    ctx_len = ctx.shape[1]
    depth = w_ada.shape[0]
    assert d == D_MODEL and seq % TM == 0 and ctx_len % TM == 0 and seq % GRID_W == 0
    r = seq + ctx_len
    tpb, nlt = r // TM, seq // TM
    n = nb * r
    n_tiles = -(-n * TOP_K // TMM) + N_EXPERTS
    alpha = float((2 * depth) ** 0.25)
    geo = dict(tpb=tpb, nlt=nlt, nb=nb)

    xs = jnp.concatenate([x, ctx], axis=1).reshape(n, d)

    mod_rows = -(-(nb + 1) // 8) * 8
    cc = jnp.zeros((mod_rows, d), F32).at[:nb].set(c).at[nb].set(c_ctx)
    mod_all = _ada(cc, w_ada, b_ada)

    lb = jnp.cumsum(jax.nn.softmax(hgrn_lb_logits.astype(F32), axis=0), axis=0)
    lb = lb - lb[:1]
    lb_rows = jnp.stack([jnp.log(lb), jnp.log1p(-lb), 1.0 - lb] + [jnp.zeros_like(lb)] * 5, axis=2)

    dft_c = _channel_dft()
    dft_lat = _dft_pair(seq)
    dft_ctx = _dft_pair(ctx_len)
    cos_t, sin_t = _rope_tables(nb, seq, ctx_len)

    w_in_all = _prep_in_proj(w_in)
    wq_all = jax.vmap(_prep_wq)(w_uq)
    wkv_all = jax.vmap(_prep_wkv)(w_ukv)
    w_out_all = w_out.astype(BF16)
    w1g_all, w1l_all = _prep_w1(w1)
    le = depth * N_EXPERTS
    w2_all = w2.reshape(le, EXPERT_FF, d)
    b1g_all = b1[:, :, 0::2].reshape(le, 1, EXPERT_FF)
    b1l_all = b1[:, :, 1::2].reshape(le, 1, EXPERT_FF)
    b2_all = b2.reshape(le, 1, d)
    rw_f32 = _pad_lanes(router_w, LANES)
    rw_hi = rw_f32.astype(BF16)
    rw_all = jnp.concatenate([rw_hi, (rw_f32 - rw_hi.astype(F32)).astype(BF16)], axis=-1)

    for l in range(depth):
        row = lambda a: a[l].reshape(1, -1)
        z, hg, mla_in = _pre(xs, mod_all, w_in_all, dft_c, l, **geo)
        yf = _fourier(z, dft_lat, dft_ctx, nb=nb, tpb=tpb, nlt=nlt, seq=seq, ctx_len=ctx_len)
        o_f = _hgrn(hg, lb_rows[l, 0], z_block=3, reverse=False, **geo)
        o_b = _hgrn(hg, lb_rows[l, 1], z_block=4, reverse=True, **geo)
        q, k, v = _mla_up(mla_in, row(mla_q_norm), row(mla_kv_norm), wq_all, wkv_all, cos_t, sin_t, l)
        att = _attention(q, k, v, nb=nb, seq=seq, ctx_len=ctx_len)

        rb = jnp.full((1, LANES), NEG_BIG, F32).at[0, :N_EXPERTS].set(router_b[l])
        x1, h2p, eid, gates = _post(yf, o_f, o_b, hg, att, xs, mod_all, row(hgrn_norm), w_out_all, row(ln1_g), row(ln1_b),
                                    rw_all, rb, l, alpha=alpha, **geo)

        pos, tile_e, n_used, fill_start, fill_n = _route(eid, n_tiles)
        xg = _dispatch(pos, fill_start, fill_n, n_used, h2p, n_tiles)
        ys = _moe_ffn(tile_e, n_used, xg, w1g_all, w1l_all, b1g_all, b1l_all, w2_all, b2_all, l)
        xs = _combine(pos, ys, gates, x1, mod_all, row(ln2_g), row(ln2_b), l, latent_only=l == depth - 1, alpha=alpha, **geo)

    return xs.reshape(nb, seq, d)
```

```python
import functools

import numpy as np
import jax
import jax.numpy as jnp
from jax import lax
from jax.experimental import pallas as pl
from jax.experimental.pallas import tpu as pltpu

F32 = jnp.float32
BF16 = jnp.bfloat16
U32 = jnp.uint32
I32 = jnp.int32

D_MODEL = 2048
FOURIER_W = 512
FOURIER_GROUP = 128
HGRN_W = 512
HGRN_HEADS = 4
HGRN_DIM = 128
MLA_HEADS = 8
MLA_NOPE = 128
MLA_ROPE = 64
MLA_V = 128
Q_LORA = 512
KV_LORA = 256
MLA_SCALE = (MLA_NOPE + MLA_ROPE) ** -0.5
N_EXPERTS = 32
TOP_K = 4
EXPERT_FF = 768
SWIGLU_ALPHA = 1.702
SWIGLU_LIMIT = 7.0
N_MOD = 6
GRID_W = 64
ROPE_BASE = 10000.0
EPS = 1e-6
LOG2_E = 1.4426950408889634

LANES = 128
SUBLANES = 8
TM = 256
TMM = 512
TQ = 2 * TM
HEAD_SLOT = 2 * LANES
HEADS_PER_STEP = 4
KEY_BLOCK = 256
IN_W = 4096
HG_OFF = FOURIER_W
MLA_OFF = HG_OFF + 5 * HGRN_W
MLA_IN_W = IN_W - MLA_OFF
N_LEVELS = 9
ADA_TN = 1024
W1_TN = 768
HALF = D_MODEL // 2
ROW_TILE = HALF // LANES
assert ROW_TILE == SUBLANES
NEG_BIG = -1e30
ISSUE_UNROLL = 8
VMEM_LIMIT = 56 * 1024 * 1024


def _cparams(sem, vmem=VMEM_LIMIT):
    return pltpu.CompilerParams(dimension_semantics=sem, vmem_limit_bytes=vmem)


def _dot(a, b):
    return jnp.dot(a, b, preferred_element_type=F32)


def _dot_nt(a, b):
    return lax.dot_general(a, b, (((1,), (1,)), ((), ())), preferred_element_type=F32)


def _dot_tn(a, b):
    return lax.dot_general(a, b, (((0,), (0,)), ((), ())), preferred_element_type=F32)


def _split_dot(x, m):
    hi = x.astype(BF16)
    lo = (x - hi.astype(F32)).astype(BF16)
    return _dot(hi, m) + _dot(lo, m)


def _layer_norm(x):
    mu = jnp.mean(x, axis=-1, keepdims=True)
    xc = x - mu
    var = jnp.mean(xc * xc, axis=-1, keepdims=True)
    return xc * lax.rsqrt(var + EPS)


def _rms(x):
    return x * lax.rsqrt(jnp.mean(x * x, axis=-1, keepdims=True) + EPS)


def _sigmoid(x):
    return 1.0 / (1.0 + jnp.exp(-x))


def _mod_row(i, tpb, nlt, nb):
    return jnp.where(i % tpb >= nlt, nb, i // tpb)


def _mod(mod_ref, row, k):
    return mod_ref[pl.ds(row, 1), k * D_MODEL:(k + 1) * D_MODEL]


def _ada_kernel(c_ref, w_ref, b_ref, o_ref):
    c = c_ref[...]
    s = c * _sigmoid(c)
    hi = s.astype(BF16)
    lo = (s - hi.astype(F32)).astype(BF16)
    w = w_ref[...].astype(BF16)
    o_ref[...] = _dot(hi, w) + _dot(lo, w) + b_ref[...]


def _ada(cc, w_ada, b_ada):
    depth, d, n = w_ada.shape
    rows = cc.shape[0]
    return pl.pallas_call(
        _ada_kernel,
        grid=(depth, n // ADA_TN),
        in_specs=[
            pl.BlockSpec((rows, d), lambda l, j: (0, 0)),
            pl.BlockSpec((None, d, ADA_TN), lambda l, j: (l, 0, j)),
            pl.BlockSpec((None, 1, ADA_TN), lambda l, j: (l, 0, j)),
        ],
        out_specs=pl.BlockSpec((None, rows, ADA_TN), lambda l, j: (l, 0, j)),
        out_shape=jax.ShapeDtypeStruct((depth, rows, n), F32),
        compiler_params=_cparams(("arbitrary", "arbitrary")),
        name="ada",
    )(cc, w_ada, b_ada.reshape(depth, 1, n))


def _pre_kernel(x_ref, mod_ref, w_ref, dft_ref, z_ref, hg_ref, mla_ref, *, tpb, nlt, nb):
    row = _mod_row(pl.program_id(0), tpb, nlt, nb)
    h = _layer_norm(x_ref[...]) * (1.0 + _mod(mod_ref, row, 1)) + _mod(mod_ref, row, 0)
    r = _dot(h.astype(BF16), w_ref[...])
    z_ref[...] = _dot(r[:, 0:FOURIER_W].astype(BF16), dft_ref[...]).astype(BF16)
    hg_ref[...] = r[:, HG_OFF:MLA_OFF]
    mla_ref[...] = r[:, MLA_OFF:IN_W]


def _layer_block(a, layer, **kw):
    return pl.BlockSpec((None,) + a.shape[1:], lambda *_: (layer, 0, 0), **kw)


def _pre(x, mod_all, w_in_all, dft_c, layer, *, tpb, nlt, nb):
    n = x.shape[0]
    kern = functools.partial(_pre_kernel, tpb=tpb, nlt=nlt, nb=nb)
    return pl.pallas_call(
        kern,
        grid=(n // TM,),
        in_specs=[
            pl.BlockSpec((TM, D_MODEL), lambda i: (i, 0)),
            _layer_block(mod_all, layer),
            _layer_block(w_in_all, layer, pipeline_mode=pl.Buffered(1)),
            pl.BlockSpec(dft_c.shape, lambda i: (0, 0)),
        ],
        out_specs=[
            pl.BlockSpec((TM, 2 * FOURIER_W), lambda i: (i, 0)),
            pl.BlockSpec((TM, 5 * HGRN_W), lambda i: (i, 0)),
            pl.BlockSpec((TM, MLA_IN_W), lambda i: (i, 0)),
        ],
        out_shape=[
            jax.ShapeDtypeStruct((n, 2 * FOURIER_W), BF16),
            jax.ShapeDtypeStruct((n, 5 * HGRN_W), F32),
            jax.ShapeDtypeStruct((n, MLA_IN_W), F32),
        ],
        compiler_params=_cparams(("arbitrary",)),
        name="pre",
    )(x, mod_all, w_in_all, dft_c)


def _four_kernel(z_ref, cl_ref, sl_ref, cc_ref, sc_ref, y_ref, *, nlt, seq):
    j = pl.program_id(1)

    def emit(cm, sm, z):
        y = _dot(cm, z[:, 0:FOURIER_W]) - _dot(sm, z[:, FOURIER_W:])
        y_ref[...] = y.astype(BF16)

    @pl.when(j < nlt)
    def _():
        emit(cl_ref[...], sl_ref[...], z_ref[0:seq, :])

    @pl.when(j >= nlt)
    def _():
        emit(cc_ref[...], sc_ref[...], z_ref[seq:, :])


def _fourier(z, dft_lat, dft_ctx, *, nb, tpb, nlt, seq, ctx_len):
    n = z.shape[0]
    r = seq + ctx_len
    cl, sl = dft_lat
    cc, sc = dft_ctx
    kern = functools.partial(_four_kernel, nlt=nlt, seq=seq)
    lmap = lambda b, j: (jnp.minimum(j, nlt - 1), 0)
    cmap = lambda b, j: (jnp.maximum(j - nlt, 0), 0)
    return pl.pallas_call(
        kern,
        grid=(nb, tpb),
        in_specs=[
            pl.BlockSpec((None, r, 2 * FOURIER_W), lambda b, j: (b, 0, 0)),
            pl.BlockSpec((TM, seq), lmap),
            pl.BlockSpec((TM, seq), lmap),
            pl.BlockSpec((TM, ctx_len), cmap),
            pl.BlockSpec((TM, ctx_len), cmap),
        ],
        out_specs=pl.BlockSpec((TM, FOURIER_W), lambda b, j: (b * tpb + j, 0)),
        out_shape=jax.ShapeDtypeStruct((n, FOURIER_W), BF16),
        compiler_params=_cparams(("arbitrary", "arbitrary")),
        name="fourier",
    )(z.reshape(nb, r, 2 * FOURIER_W), cl, sl, cc, sc)


def _level_matrices(reverse):
    p = np.arange(TM)
    mats = [(p[None, :] <= p[:, None])]
    for lev in range(1, N_LEVELS):
        r = TM >> lev
        same = (p[None, :] // r) == (p[:, None] // r)
        is_q = ((p // r) % 2 == 1)[:, None]
        mats.append(same & np.where(is_q, p[None, :] <= p[:, None], p[None, :] > p[:, None]))
    g = np.stack(mats).astype(np.float32)
    if reverse:
        g = g[:, ::-1, ::-1]
    return jnp.asarray(g.reshape(N_LEVELS * TM, TM), dtype=BF16)


def _hgrn_kernel(q_ref, v_ref, z_ref, lb_ref, g_ref, o_ref, st_ref, *, reverse):
    @pl.when(pl.program_id(1) == 0)
    def _():
        st_ref[...] = jnp.zeros_like(st_ref)

    z = z_ref[...]
    log_lb = lb_ref[0:1, :]
    log_1m_lb = lb_ref[1:2, :]
    one_m_lb = lb_ref[2:3, :]
    log_sig = jnp.minimum(z, 0.0) - jnp.log1p(jnp.exp(-jnp.abs(z)))
    a2 = log_1m_lb + log_sig
    log_f = jnp.maximum(log_lb, a2) + jnp.log1p(jnp.exp(-jnp.abs(log_lb - a2)))
    kk = one_m_lb * _sigmoid(-z)

    wall = _split_dot_left(g_ref[...], log_f * LOG2_E)
    a = wall[0:TM]
    a_end = a[0:1] if reverse else a[TM - 1:TM]
    q = q_ref[...]
    v = v_ref[...].astype(BF16)
    qa = (q * jnp.exp2(a)).astype(BF16)
    kd = (kk * jnp.exp2(a_end - a)).astype(BF16)
    carry = jnp.exp2(a_end)

    half = TM // 2
    tiles = (slice(0, half), slice(half, TM))
    q_tile, k_tile = (0, 1) if reverse else (1, 0)
    row = lax.broadcasted_iota(I32, (half, half), 0)
    col = lax.broadcasted_iota(I32, (half, half), 1)
    diff = row ^ col
    pos = lax.broadcasted_iota(I32, (TM, HGRN_W), 0)
    if reverse:
        pos = TM - 1 - pos

    def is_q_piece(idx):
        return idx % 2 == (0 if reverse else 1)

    fac, fac_q, fac_k = {}, {}, {}
    for lev in range(1, N_LEVELS):
        r = TM >> lev
        fac[lev] = jnp.exp2(wall[lev * TM:(lev + 1) * TM])
        if r < SUBLANES:
            is_q = ((pos >> (r.bit_length() - 1)) & 1) == 1
            fac_q[lev] = jnp.where(is_q, fac[lev], 0.0)
            fac_k[lev] = jnp.where(is_q, 0.0, fac[lev])

    def level_operands(lev, qh, kh, sl, rows):
        r = TM >> lev
        if r < SUBLANES:
            return (qh[rows] * fac_q[lev][rows, sl]).astype(BF16), (kh[rows] * fac_k[lev][rows, sl]).astype(BF16)
        qs, ks = [], []
        for start in range(rows.start, rows.stop, r):
            piece = slice(start, start + r)
            zero = jnp.zeros((r, HGRN_DIM), F32)
            if is_q_piece(start // r):
                qs.append(qh[piece] * fac[lev][piece, sl])
                ks.append(zero)
            else:
                qs.append(zero)
                ks.append(kh[piece] * fac[lev][piece, sl])
        cat = lambda xs: (xs[0] if len(xs) == 1 else jnp.concatenate(xs, axis=0)).astype(BF16)
        return cat(qs), cat(ks)

    heads = [slice(h * HGRN_DIM, (h + 1) * HGRN_DIM) for h in range(HGRN_HEADS)]
    p = {}
    for h, sl in enumerate(heads):
        for d, rows in enumerate(tiles):
            p[h, d] = jnp.where(diff == 0, _dot_nt(q[rows, sl].astype(BF16), kk[rows, sl].astype(BF16)), 0.0)
    for lev in range(2, N_LEVELS):
        block = 2 * (TM >> lev)
        for h, sl in enumerate(heads):
            for d, rows in enumerate(tiles):
                qt, kt = level_operands(lev, q[:, sl], kk[:, sl], sl, rows)
                sc = _dot_nt(qt, kt)
                p[h, d] = p[h, d] + (sc if block == half else jnp.where((diff >> (block.bit_length() - 1)) == 0, sc, 0.0))

    for h, sl in enumerate(heads):
        st = st_ref[h]
        qh, kh, vh = q[:, sl], kk[:, sl], v[:, sl]
        p_off = _dot_nt((qh[tiles[q_tile]] * fac[1][tiles[q_tile], sl]).astype(BF16),
                        (kh[tiles[k_tile]] * fac[1][tiles[k_tile], sl]).astype(BF16))
        inter = _dot_nt(qa[:, sl], st.astype(BF16))
        for d, rows in enumerate(tiles):
            o = inter[rows] + _dot(p[h, d].astype(BF16), vh[rows])
            if d == q_tile:
                o = o + _dot(p_off.astype(BF16), vh[tiles[k_tile]])
            o_ref[rows, sl] = o
        st_ref[h] = st * carry[:, sl] + _dot_tn(vh, kd[:, sl])


def _split_dot_left(m, x):
    hi = x.astype(BF16)
    lo = (x - hi.astype(F32)).astype(BF16)
    return _dot(m, hi) + _dot(m, lo)


def _hgrn(hg, lb_rows, *, z_block, reverse, nb, tpb, nlt):
    n = hg.shape[0]
    nct = tpb - nlt

    def chunk(b, s):
        if reverse:
            j = jnp.where(s < nct, tpb - 1 - s, nlt - 1 - (s - nct))
        else:
            j = jnp.where(s < nct, nlt + s, s - nct)
        return b * tpb + j

    kern = functools.partial(_hgrn_kernel, reverse=reverse)
    gmat = _level_matrices(reverse)
    return pl.pallas_call(
        kern,
        grid=(nb, tpb),
        in_specs=[
            pl.BlockSpec((TM, HGRN_W), lambda b, s: (chunk(b, s), 0)),
            pl.BlockSpec((TM, HGRN_W), lambda b, s: (chunk(b, s), 1)),
            pl.BlockSpec((TM, HGRN_W), lambda b, s: (chunk(b, s), z_block)),
            pl.BlockSpec((8, HGRN_W), lambda b, s: (0, 0)),
            pl.BlockSpec(gmat.shape, lambda b, s: (0, 0)),
        ],
        out_specs=pl.BlockSpec((TM, HGRN_W), lambda b, s: (chunk(b, s), 0)),
        out_shape=jax.ShapeDtypeStruct((n, HGRN_W), F32),
        scratch_shapes=[pltpu.VMEM((HGRN_HEADS, HGRN_DIM, HGRN_DIM), F32)],
        compiler_params=_cparams(("arbitrary", "arbitrary")),
        name="hgrn_bwd" if reverse else "hgrn_fwd",
    )(hg, hg, hg, lb_rows, gmat)


def _mla_kernel(m_ref, qg_ref, kvg_ref, wq_ref, wkv_ref, cos_ref, sin_ref, q_ref, k_ref, v_ref):
    cq = m_ref[:, 0:Q_LORA]
    ckv = m_ref[:, Q_LORA:Q_LORA + KV_LORA]
    kr = m_ref[:, Q_LORA + KV_LORA:Q_LORA + KV_LORA + LANES]
    kr_sw = m_ref[:, Q_LORA + KV_LORA + LANES:MLA_IN_W]
    cos = cos_ref[...]
    sin = sin_ref[...]
    qall = _dot((_rms(cq) * qg_ref[...]).astype(BF16), wq_ref[...]) * (MLA_SCALE * LOG2_E)
    kv = _dot((_rms(ckv) * kvg_ref[...]).astype(BF16), wkv_ref[...])
    k_rot = (kr * cos + kr_sw * sin).astype(BF16)
    ones = jnp.ones((TM, LANES), BF16)
    nw = MLA_HEADS * LANES
    for h in range(MLA_HEADS):
        a, b = h * LANES, (h + 1) * LANES
        s = h * HEAD_SLOT
        q_ref[:, s:s + LANES] = qall[:, a:b].astype(BF16)
        q_ref[:, s + LANES:s + HEAD_SLOT] = (qall[:, nw + a:nw + b] * cos + qall[:, 2 * nw + a:2 * nw + b] * sin).astype(BF16)
        k_ref[:, s:s + LANES] = kv[:, a:b].astype(BF16)
        k_ref[:, s + LANES:s + HEAD_SLOT] = k_rot
        v_ref[:, s:s + LANES] = kv[:, nw + a:nw + b].astype(BF16)
        v_ref[:, s + LANES:s + HEAD_SLOT] = ones


def _mla_up(mla_in, q_gain, kv_gain, wq_all, wkv_all, cos_t, sin_t, layer):
    n = mla_in.shape[0]
    w = MLA_HEADS * HEAD_SLOT
    full = lambda a: pl.BlockSpec(a.shape, lambda i: (0, 0))
    return pl.pallas_call(
        _mla_kernel,
        grid=(n // TM,),
        in_specs=[
            pl.BlockSpec((TM, MLA_IN_W), lambda i: (i, 0)),
            full(q_gain), full(kv_gain), _layer_block(wq_all, layer), _layer_block(wkv_all, layer),
            pl.BlockSpec((TM, LANES), lambda i: (i, 0)),
            pl.BlockSpec((TM, LANES), lambda i: (i, 0)),
        ],
        out_specs=[pl.BlockSpec((TM, w), lambda i: (i, 0))] * 3,
        out_shape=[jax.ShapeDtypeStruct((n, w), BF16)] * 3,
        compiler_params=_cparams(("arbitrary",)),
        name="mla_up",
    )(mla_in, q_gain, kv_gain, wq_all, wkv_all, cos_t, sin_t)


def _attn_kernel(q_ref, k_ref, v_ref, o_ref, s_ref, *, n_lat_steps, seq, ctx_len):
    j = pl.program_id(2)

    def run(rows, k0, k1):
        blocks = [(kb, slice(k0 + kb * KEY_BLOCK, k0 + (kb + 1) * KEY_BLOCK), slice(kb * KEY_BLOCK, (kb + 1) * KEY_BLOCK))
                  for kb in range((k1 - k0) // KEY_BLOCK)]
        hslot = lambda hh: slice(hh * HEAD_SLOT, (hh + 1) * HEAD_SLOT)

        def scores(hh, blk, m_run):
            _, ks, ss = blk
            s = _dot_nt(q_ref[0:rows, hslot(hh)], k_ref[ks, hslot(hh)])
            s_ref[hh % 2, 0:rows, ss] = s
            for c in range(KEY_BLOCK // LANES):
                m_run = jnp.maximum(m_run, s[:, c * LANES:(c + 1) * LANES])
            return m_run

        def values(hh, blk, m, o):
            _, ks, ss = blk
            d = (s_ref[hh % 2, 0:rows, ss] - m).astype(BF16)
            return o + _dot(jnp.exp2(d), v_ref[ks, hslot(hh)])

        def emit(hh, o):
            o_ref[0:rows, hh * MLA_V:(hh + 1) * MLA_V] = (o[:, 0:MLA_V] / o[:, MLA_V:]).astype(BF16)

        neg = jnp.full((rows, LANES), -jnp.inf, F32)
        zero = jnp.zeros((rows, HEAD_SLOT), F32)
        m_run = neg
        for blk in blocks:
            m_run = scores(0, blk, m_run)
        m_prev = jnp.max(m_run, axis=-1, keepdims=True)
        for hh in range(1, HEADS_PER_STEP):
            m_run, o = neg, zero
            for blk in blocks:
                m_run = scores(hh, blk, m_run)
                o = values(hh - 1, blk, m_prev, o)
            emit(hh - 1, o)
            m_prev = jnp.max(m_run, axis=-1, keepdims=True)
        o = zero
        for blk in blocks:
            o = values(HEADS_PER_STEP - 1, blk, m_prev, o)
        emit(HEADS_PER_STEP - 1, o)

    @pl.when(j < n_lat_steps)
    def _():
        run(TQ, 0, seq + ctx_len)

    @pl.when(j >= n_lat_steps)
    def _():
        run(ctx_len, seq, seq + ctx_len)


def _attention(q, k, v, *, nb, seq, ctx_len):
    n = q.shape[0]
    r = seq + ctx_len
    assert seq % TQ == 0 and ctx_len <= TQ
    n_lat_steps = seq // TQ
    hp = MLA_HEADS // HEADS_PER_STEP
    wq = HEADS_PER_STEP * HEAD_SLOT
    wo = HEADS_PER_STEP * MLA_V
    kern = functools.partial(_attn_kernel, n_lat_steps=n_lat_steps, seq=seq, ctx_len=ctx_len)
    out = pl.pallas_call(
        kern,
        grid=(nb, hp, n_lat_steps + 1),
        in_specs=[
            pl.BlockSpec((None, TQ, wq), lambda b, h, j: (b, j, h)),
            pl.BlockSpec((None, r, wq), lambda b, h, j: (b, 0, h), pipeline_mode=pl.Buffered(1)),
            pl.BlockSpec((None, r, wq), lambda b, h, j: (b, 0, h), pipeline_mode=pl.Buffered(1)),
        ],
        out_specs=pl.BlockSpec((None, TQ, wo), lambda b, h, j: (b, j, h)),
        out_shape=jax.ShapeDtypeStruct((nb, r, MLA_HEADS * MLA_V), BF16),
        scratch_shapes=[pltpu.VMEM((2, TQ, r), F32)],
        compiler_params=_cparams(("arbitrary", "arbitrary", "arbitrary")),
        name="attention",
    )(q.reshape(nb, r, -1), k.reshape(nb, r, -1), v.reshape(nb, r, -1))
    return out.reshape(n, MLA_HEADS * MLA_V)


def _pack_rows(x):
    w = x.shape[1] // 2
    lo = pltpu.bitcast(x[:, :w].astype(BF16).astype(F32), U32)
    hi = pltpu.bitcast(x[:, w:].astype(BF16).astype(F32), U32)
    return (lo >> 16) | (hi & jnp.uint32(0xFFFF0000))


def _store_row_tiles(ref, packed):
    rows = packed.shape[0]
    for s in range(ROW_TILE):
        ref[pl.ds(s, rows, stride=ROW_TILE), :] = packed[:, s * LANES:(s + 1) * LANES]


def _load_row_tiles(ref, rows):
    return jnp.concatenate([ref[pl.ds(s, rows, stride=ROW_TILE), :] for s in range(ROW_TILE)], axis=1)


def _unpack_rows(w):
    lo = pltpu.bitcast(w << 16, F32)
    hi = pltpu.bitcast(w & jnp.uint32(0xFFFF0000), F32)
    return lo, hi


def _post_kernel(yf_ref, of_ref, ob_ref, g_ref, att_ref, x_ref, mod_ref, hn_ref, wo_ref, lng_ref, lnb_ref,
                 rw_ref, rb_ref, x1_ref, h2_ref, eid_ref, gate_ref, *, tpb, nlt, nb, alpha):
    row = _mod_row(pl.program_id(0), tpb, nlt, nb)
    o = of_ref[...] + ob_ref[...]
    hn = hn_ref[...]
    gate_act = g_ref[...]
    gate_act = gate_act * _sigmoid(gate_act)
    mix = _dot(yf_ref[...], wo_ref[0:FOURIER_W, :])
    for h in range(HGRN_HEADS):
        sl = slice(h * HGRN_DIM, (h + 1) * HGRN_DIM)
        r = (_rms(o[:, sl]) * hn[:, sl] * gate_act[:, sl]).astype(BF16)
        mix = mix + _dot(r, wo_ref[FOURIER_W + h * HGRN_DIM:FOURIER_W + (h + 1) * HGRN_DIM, :])
    mix = mix + _dot(att_ref[...], wo_ref[FOURIER_W + HGRN_W:, :])

    x1 = _layer_norm(alpha * x_ref[...] + _mod(mod_ref, row, 2) * mix) * lng_ref[...] + lnb_ref[...]
    x1_ref[...] = x1
    h2 = _layer_norm(x1) * (1.0 + _mod(mod_ref, row, 4)) + _mod(mod_ref, row, 3)
    _store_row_tiles(h2_ref, _pack_rows(h2))

    hi = h2.astype(BF16)
    lo = (h2 - hi.astype(F32)).astype(BF16)
    rr = _dot(jnp.concatenate([hi, lo], axis=0), rw_ref[...])
    logits = rr[0:TM, 0:LANES] + rr[0:TM, LANES:] + rr[TM:, 0:LANES] + rr[TM:, LANES:] + rb_ref[...]
    lane = lax.broadcasted_iota(I32, logits.shape, 1)
    cur = logits
    vals, ids = [], []
    for _ in range(TOP_K):
        m = jnp.max(cur, axis=-1, keepdims=True)
        idx = jnp.min(jnp.where(cur == m, lane, LANES), axis=-1, keepdims=True)
        vals.append(m)
        ids.append(idx)
        cur = jnp.where(lane == idx, -3e38, cur)
    ex = [jnp.exp(vk - vals[0]) for vk in vals]
    den = ex[0] + ex[1] + ex[2] + ex[3]
    eid = jnp.zeros(logits.shape, I32)
    gate = jnp.zeros(logits.shape, F32)
    for k in range(TOP_K):
        eid = jnp.where(lane == k, ids[k], eid)
        gate = jnp.where(lane == k, ex[k] / den, gate)
    eid_ref[...] = eid
    gate_ref[...] = gate


def _post(yf, o_f, o_b, hg, att, x, mod_all, hnorm, w_out_all, ln_g, ln_b, rw_all, rb, layer, *, tpb, nlt, nb, alpha):
    n = x.shape[0]
    kern = functools.partial(_post_kernel, tpb=tpb, nlt=nlt, nb=nb, alpha=alpha)
    full = lambda a: pl.BlockSpec(a.shape, lambda i: (0, 0))
    return pl.pallas_call(
        kern,
        grid=(n // TM,),
        in_specs=[
            pl.BlockSpec((TM, FOURIER_W), lambda i: (i, 0)),
            pl.BlockSpec((TM, HGRN_W), lambda i: (i, 0)),
            pl.BlockSpec((TM, HGRN_W), lambda i: (i, 0)),
            pl.BlockSpec((TM, HGRN_W), lambda i: (i, 2)),
            pl.BlockSpec((TM, MLA_HEADS * MLA_V), lambda i: (i, 0)),
            pl.BlockSpec((TM, D_MODEL), lambda i: (i, 0)),
            _layer_block(mod_all, layer), full(hnorm),
            _layer_block(w_out_all, layer, pipeline_mode=pl.Buffered(1)),
            full(ln_g), full(ln_b), _layer_block(rw_all, layer), full(rb),
        ],
        out_specs=[
            pl.BlockSpec((TM, D_MODEL), lambda i: (i, 0)),
            pl.BlockSpec((TM * ROW_TILE, LANES), lambda i: (i, 0)),
            pl.BlockSpec((TM, LANES), lambda i: (i, 0)),
            pl.BlockSpec((TM, LANES), lambda i: (i, 0)),
        ],
        out_shape=[
            jax.ShapeDtypeStruct((n, D_MODEL), F32),
            jax.ShapeDtypeStruct((n * ROW_TILE, LANES), U32),
            jax.ShapeDtypeStruct((n, LANES), I32),
            jax.ShapeDtypeStruct((n, LANES), F32),
        ],
        compiler_params=_cparams(("arbitrary",)),
        name="post",
    )(yf, o_f, o_b, hg, att, x, mod_all, hnorm, w_out_all, ln_g, ln_b, rw_all, rb)


def _route_kernel(eid_ref, pos_ref, te_ref, meta_ref, base_ref, cnt_ref, *, n_tiles_pad):
    ph = pl.program_id(0)
    i = pl.program_id(1)
    lane = lax.broadcasted_iota(I32, (TM, LANES), 1)
    eid = eid_ref[...]
    onehot = [(lane == eid[:, k:k + 1]).astype(F32) for k in range(TOP_K)]
    colsum = [jnp.sum(oh, axis=0, keepdims=True) for oh in onehot]

    @pl.when(ph == 0)
    def _():
        @pl.when(i == 0)
        def _():
            cnt_ref[...] = jnp.zeros_like(cnt_ref)

        base_ref[pl.ds(i, 1), :] = cnt_ref[0:1, :]
        cnt_ref[0:1, :] = cnt_ref[0:1, :] + colsum[0] + colsum[1] + colsum[2] + colsum[3]

    @pl.when(ph == 1)
    def _():
        cnt = cnt_ref[0:1, :]
        tiles = jnp.floor((cnt + (TMM - 1)) * (1.0 / TMM))
        r128 = lax.broadcasted_iota(I32, (LANES, LANES), 0)
        c128 = lax.broadcasted_iota(I32, (LANES, LANES), 1)
        incl = (r128 <= c128).astype(BF16)
        pad_end = _split_dot(jnp.broadcast_to(tiles, (8, LANES)), incl)[0:1, :] * TMM
        pad_start = pad_end - tiles * TMM

        rr = lax.broadcasted_iota(I32, (TM, TM), 0)
        cc = lax.broadcasted_iota(I32, (TM, TM), 1)
        strict = (cc < rr).astype(BF16)
        run = pad_start + base_ref[pl.ds(i, 1), :]
        pos = jnp.zeros((TM, LANES), I32)
        for k in range(TOP_K):
            before = _dot(strict, onehot[k].astype(BF16))
            val = jnp.sum(onehot[k] * (before + run), axis=1, keepdims=True)
            pos = jnp.where(lane == k, val.astype(I32), pos)
            run = run + colsum[k]
        pos_ref[...] = pos

        @pl.when(i == 0)
        def _():
            tstart = (lax.broadcasted_iota(I32, (n_tiles_pad, LANES), 0) * TMM).astype(F32)
            lane_t = lax.broadcasted_iota(I32, (n_tiles_pad, LANES), 1)
            done = jnp.where((pad_end <= tstart) & (lane_t < N_EXPERTS), 1.0, 0.0)
            te = jnp.minimum(jnp.sum(done, axis=1, keepdims=True), N_EXPERTS - 1.0)
            te_ref[...] = jnp.broadcast_to(te, (n_tiles_pad, LANES)).astype(I32)
            sub = lax.broadcasted_iota(I32, (8, LANES), 0)
            meta = jnp.where(sub == 0, pad_start + cnt, jnp.where(sub == 1, pad_end - pad_start - cnt, pad_end))
            meta_ref[...] = meta.astype(I32)


def _route(eid, n_tiles):
    n = eid.shape[0]
    nt = n // TM
    n_tiles_pad = -(-n_tiles // 8) * 8
    kern = functools.partial(_route_kernel, n_tiles_pad=n_tiles_pad)
    pos, te, meta = pl.pallas_call(
        kern,
        grid=(2, nt),
        in_specs=[pl.BlockSpec((TM, LANES), lambda ph, i: (i, 0))],
        out_specs=[
            pl.BlockSpec((TM, LANES), lambda ph, i: (ph * i, 0)),
            pl.BlockSpec((n_tiles_pad, LANES), lambda ph, i: (0, 0)),
            pl.BlockSpec((8, LANES), lambda ph, i: (0, 0)),
        ],
        out_shape=[
            jax.ShapeDtypeStruct((n, LANES), I32),
            jax.ShapeDtypeStruct((n_tiles_pad, LANES), I32),
            jax.ShapeDtypeStruct((8, LANES), I32),
        ],
        scratch_shapes=[pltpu.VMEM((-(-nt // 8) * 8, LANES), F32), pltpu.VMEM((8, LANES), F32)],
        compiler_params=_cparams(("arbitrary", "arbitrary")),
        name="moe_route",
    )(eid)
    pos_flat = pos[:, :TOP_K].reshape(-1)
    tile_e = te[:n_tiles, 0]
    fill_start = meta[0, :N_EXPERTS]
    fill_n = meta[1, :N_EXPERTS]
    n_used = meta[2, N_EXPERTS - 1:N_EXPERTS] // TMM
    return pos_flat, tile_e, n_used, fill_start, fill_n


def _dispatch_kernel(pos_ref, fs_ref, fn_ref, nu_ref, h_ref, o_hbm, zero_ref, sem, fill_sem, *, n_tiles):
    i = pl.program_id(0)

    def rows(ref, first, count):
        return ref.at[pl.ds(pl.multiple_of(first * ROW_TILE, ROW_TILE), count * ROW_TILE), :]

    def start(r, c):
        for k in range(TOP_K):
            p = pos_ref[(i * TM + r) * TOP_K + k]
            pltpu.make_async_copy(rows(h_ref, r, 1), rows(o_hbm, p, 1), sem).start()
        return c

    lax.fori_loop(0, TM, start, 0, unroll=ISSUE_UNROLL)

    @pl.when(i == 0)
    def _():
        zero_ref[...] = jnp.zeros_like(zero_ref)

        def fill(first, count):
            return pltpu.make_async_copy(rows(zero_ref, 0, count), rows(o_hbm, first, count), fill_sem)

        def each(fn):
            for e in range(N_EXPERTS):
                first, count = fs_ref[e], fn_ref[e]
                piece = TMM // 2
                while piece:
                    @pl.when((count & piece) != 0)
                    def _(first=first, piece=piece):
                        fn(fill(first, piece))

                    first = first + (count & piece)
                    piece //= 2
            lax.fori_loop(nu_ref[0], n_tiles, lambda t, c: (fn(fill(t * TMM, TMM)), c)[1], 0)

        each(lambda cp: cp.start())
        each(lambda cp: cp.wait())

    for _ in range(TOP_K):
        pltpu.make_async_copy(h_ref, rows(o_hbm, 0, TM), sem).wait()


def _dispatch(pos_flat, fill_start, fill_n, n_used, h2p, n_tiles):
    n = h2p.shape[0] // ROW_TILE
    return pl.pallas_call(
        functools.partial(_dispatch_kernel, n_tiles=n_tiles),
        grid_spec=pltpu.PrefetchScalarGridSpec(
            num_scalar_prefetch=4,
            grid=(n // TM,),
            in_specs=[pl.BlockSpec((TM * ROW_TILE, LANES), lambda i, p, fs, fn, nu: (i, 0))],
            out_specs=pl.BlockSpec(memory_space=pl.ANY),
            scratch_shapes=[pltpu.VMEM((TMM * ROW_TILE, LANES), U32), pltpu.SemaphoreType.DMA(()),
                            pltpu.SemaphoreType.DMA(())],
        ),
        out_shape=jax.ShapeDtypeStruct((n_tiles * TMM * ROW_TILE, LANES), U32),
        compiler_params=_cparams(("arbitrary",)),
        name="moe_dispatch",
    )(pos_flat, fill_start, fill_n, n_used, h2p)


def _w1_kernel(w_ref, g_ref, l_ref, t_ref):
    for c in range(D_MODEL // LANES):
        cs = slice(c * LANES, (c + 1) * LANES)
        t_ref[c] = w_ref[cs, :].T
        g_ref[:, cs] = t_ref[c, pl.ds(0, W1_TN // 2, stride=2), :].astype(BF16)
        l_ref[:, cs] = t_ref[c, pl.ds(1, W1_TN // 2, stride=2), :].astype(BF16)


def _prep_w1(w1):
    depth, ne, d, n2 = w1.shape
    out = jax.ShapeDtypeStruct((depth * ne, n2 // 2, d), BF16)
    g, l = pl.pallas_call(
        _w1_kernel,
        grid=(depth * ne, n2 // W1_TN),
        in_specs=[pl.BlockSpec((None, d, W1_TN), lambda e, j: (e, 0, j))],
        out_specs=[pl.BlockSpec((None, W1_TN // 2, d), lambda e, j: (e, j, 0))] * 2,
        out_shape=[out, out],
        scratch_shapes=[pltpu.VMEM((d // LANES, W1_TN, LANES), F32)],
        compiler_params=_cparams(("arbitrary", "arbitrary")),
        name="moe_w1_prep",
    )(w1.reshape(depth * ne, d, n2))
    return g, l


def _moe_kernel(te_ref, nu_ref, x_ref, w1g_ref, w1l_ref, b1g_ref, b1l_ref, w2_ref, b2_ref, y_ref, w2b_ref):
    i = pl.program_id(0)

    @pl.when(i < nu_ref[0])
    def _():
        @pl.when((i == 0) | (te_ref[i] != te_ref[jnp.maximum(i - 1, 0)]))
        def _():
            w2b_ref[...] = w2_ref[...].astype(BF16)

        lo, hi = _unpack_rows(_load_row_tiles(x_ref, TMM))
        lo = lo.astype(BF16)
        hi = hi.astype(BF16)
        ug = _dot_nt(lo, w1g_ref[:, 0:HALF]) + _dot_nt(hi, w1g_ref[:, HALF:]) + b1g_ref[...]
        ul = _dot_nt(lo, w1l_ref[:, 0:HALF]) + _dot_nt(hi, w1l_ref[:, HALF:]) + b1l_ref[...]
        xg = jnp.minimum(ug, SWIGLU_LIMIT)
        xl = jnp.clip(ul, -SWIGLU_LIMIT, SWIGLU_LIMIT)
        act = xg * _sigmoid(SWIGLU_ALPHA * xg) * (xl + 1.0)
        y = _dot(act.astype(BF16), w2b_ref[...]) + b2_ref[...]
        _store_row_tiles(y_ref, _pack_rows(y))

    @pl.when(i >= nu_ref[0])
    def _():
        y_ref[...] = jnp.zeros_like(y_ref)


def _moe_ffn(tile_e, n_used, xs, w1g, w1l, b1g, b1l, w2, b2, layer):
    m = xs.shape[0] // ROW_TILE
    emap = lambda i, te, nu: (layer * N_EXPERTS + te[i], 0, 0)
    row_tiles = pl.BlockSpec((TMM * ROW_TILE, LANES), lambda i, te, nu: (i, 0))
    return pl.pallas_call(
        _moe_kernel,
        grid_spec=pltpu.PrefetchScalarGridSpec(
            num_scalar_prefetch=2,
            grid=(m // TMM,),
            in_specs=[
                row_tiles,
                pl.BlockSpec((None, EXPERT_FF, D_MODEL), emap),
                pl.BlockSpec((None, EXPERT_FF, D_MODEL), emap),
                pl.BlockSpec((None, 1, EXPERT_FF), emap),
                pl.BlockSpec((None, 1, EXPERT_FF), emap),
                pl.BlockSpec((None, EXPERT_FF, D_MODEL), emap),
                pl.BlockSpec((None, 1, D_MODEL), emap),
            ],
            out_specs=row_tiles,
            scratch_shapes=[pltpu.VMEM((EXPERT_FF, D_MODEL), BF16)],
        ),
        out_shape=jax.ShapeDtypeStruct((m * ROW_TILE, LANES), U32),
        compiler_params=_cparams(("arbitrary",)),
        name="moe_ffn",
    )(tile_e, n_used, xs, w1g, w1l, b1g, b1l, w2, b2)


def _combine_kernel(pos_ref, y_hbm, gate_ref, x_ref, mod_ref, lng_ref, lnb_ref, o_ref, buf, sem, *, tpb, nlt, nb, alpha,
                    latent_only):
    i = pl.program_id(0)
    n_steps = pl.num_programs(0)
    row = _mod_row(i, tpb, nlt, nb)

    def issue(tile, slot):
        def start(r, c):
            for k in range(TOP_K):
                p = pos_ref[(tile * TM + r) * TOP_K + k]
                src = y_hbm.at[pl.ds(pl.multiple_of(p * ROW_TILE, ROW_TILE), ROW_TILE), :]
                dst = buf.at[slot, k, pl.ds(pl.multiple_of(r * ROW_TILE, ROW_TILE), ROW_TILE), :]
                pltpu.make_async_copy(src, dst, sem.at[slot]).start()
            return c

        lax.fori_loop(0, TM, start, 0, unroll=ISSUE_UNROLL)

    @pl.when(i == 0)
    def _():
        issue(0, 0)

    @pl.when(i + 1 < n_steps)
    def _():
        issue(i + 1, (i + 1) % 2)

    slot = i % 2
    for k in range(TOP_K):
        pltpu.make_async_copy(y_hbm.at[pl.ds(0, TM * ROW_TILE), :], buf.at[slot, k], sem.at[slot]).wait()

    gates = gate_ref[...]
    acc_lo = jnp.zeros((TM, HALF), F32)
    acc_hi = jnp.zeros((TM, HALF), F32)
    for k in range(TOP_K):
        lo, hi = _unpack_rows(_load_row_tiles(buf.at[slot, k], TM))
        gk = gates[:, k:k + 1]
        acc_lo = acc_lo + gk * lo
        acc_hi = acc_hi + gk * hi
    ffn = jnp.concatenate([acc_lo, acc_hi], axis=-1)
    x2 = _layer_norm(alpha * x_ref[...] + _mod(mod_ref, row, 5) * ffn) * lng_ref[...] + lnb_ref[...]
    if latent_only:
        @pl.when(i % tpb < nlt)
        def _():
            o_ref[...] = x2
    else:
        o_ref[...] = x2


def _combine(pos, ys, gates, x, mod_all, ln_g, ln_b, layer, *, latent_only, tpb, nlt, nb, alpha):
    n = x.shape[0]
    kern = functools.partial(_combine_kernel, tpb=tpb, nlt=nlt, nb=nb, alpha=alpha, latent_only=latent_only)
    full = lambda a: pl.BlockSpec(a.shape, lambda i, p: (0, 0))
    if latent_only:
        omap = lambda i, p: ((i // tpb) * nlt + jnp.minimum(i % tpb, nlt - 1), 0)
        n_out = nb * nlt * TM
    else:
        omap = lambda i, p: (i, 0)
        n_out = n
    return pl.pallas_call(
        kern,
        grid_spec=pltpu.PrefetchScalarGridSpec(
            num_scalar_prefetch=1,
            grid=(n // TM,),
            in_specs=[
                pl.BlockSpec(memory_space=pl.ANY),
                pl.BlockSpec((TM, LANES), lambda i, p: (i, 0)),
                pl.BlockSpec((TM, D_MODEL), lambda i, p: (i, 0)),
                _layer_block(mod_all, layer), full(ln_g), full(ln_b),
            ],
            out_specs=pl.BlockSpec((TM, D_MODEL), omap),
            scratch_shapes=[pltpu.VMEM((2, TOP_K, TM * ROW_TILE, LANES), U32), pltpu.SemaphoreType.DMA((2,))],
        ),
        out_shape=jax.ShapeDtypeStruct((n_out, D_MODEL), F32),
        compiler_params=_cparams(("arbitrary",)),
        name="moe_combine",
    )(pos, ys, gates, x, mod_all, ln_g, ln_b)


def _dft_pair(t_len):
    assert t_len % GRID_W == 0
    k = jnp.arange(t_len, dtype=I32)

    def table(t):
        ang = ((t[:, None] * k[None, :]) % t_len).astype(F32) * (2.0 * np.pi / t_len)
        return jnp.cos(ang), jnp.sin(ang)

    ca, sa = table(jnp.arange(t_len // GRID_W, dtype=I32) * GRID_W)
    cb, sb = table(jnp.arange(GRID_W, dtype=I32))
    s = 1.0 / np.sqrt(t_len)
    cos = ca[:, None, :] * cb[None, :, :] - sa[:, None, :] * sb[None, :, :]
    sin = sa[:, None, :] * cb[None, :, :] + ca[:, None, :] * sb[None, :, :]
    return (cos * s).reshape(t_len, t_len).astype(BF16), (sin * s).reshape(t_len, t_len).astype(BF16)


def _channel_dft():
    c = np.arange(FOURIER_GROUP)
    ang = 2.0 * np.pi * ((c[:, None] * c[None, :]) % FOURIER_GROUP) / FOURIER_GROUP
    s = 1.0 / np.sqrt(FOURIER_GROUP)
    groups = FOURIER_W // FOURIER_GROUP
    m = np.zeros((FOURIER_W, 2 * FOURIER_W), np.float32)
    for g in range(groups):
        a, b = g * FOURIER_GROUP, (g + 1) * FOURIER_GROUP
        m[a:b, a:b] = np.cos(ang) * s
        m[a:b, FOURIER_W + a:FOURIER_W + b] = np.sin(ang) * s
    return jnp.asarray(m, dtype=BF16)


def _pair_swap(w):
    pairs = w.reshape(*w.shape[:-1], w.shape[-1] // 2, 2)
    return jnp.stack([-pairs[..., 1], pairs[..., 0]], axis=-1).reshape(w.shape)


def _pad_lanes(w, width):
    return jnp.pad(w, [(0, 0)] * (w.ndim - 1) + [(0, width - w.shape[-1])])


def _rope_tables(nb, seq, ctx_len):
    t = jnp.arange(seq)
    n_freq = MLA_ROPE // 4
    inv = ROPE_BASE ** (-jnp.arange(n_freq, dtype=F32) / n_freq)
    ang = jnp.concatenate([(t // GRID_W)[:, None] * inv, (t % GRID_W)[:, None] * inv], axis=-1)
    ang = jnp.repeat(ang, 2, axis=-1)
    cos = jnp.concatenate([jnp.cos(ang), jnp.ones((ctx_len, MLA_ROPE), F32)], axis=0)
    sin = jnp.concatenate([jnp.sin(ang), jnp.zeros((ctx_len, MLA_ROPE), F32)], axis=0)
    cos = jnp.tile(_pad_lanes(cos, LANES), (nb, 1))
    sin = jnp.tile(_pad_lanes(sin, LANES), (nb, 1))
    return cos, sin


def _prep_in_proj(w_in):
    f, q, i, g, zf, zb, cq, ckv, kr = jnp.split(w_in, np.cumsum([512, 512, 512, 512, 512, 512, 512, 256]).tolist(), axis=-1)
    cols = [f, q, i, g, zf, zb, cq, ckv, _pad_lanes(kr, LANES), _pad_lanes(_pair_swap(kr), LANES)]
    return jnp.concatenate(cols, axis=-1).astype(BF16)


def _prep_wq(w_uq):
    w = w_uq.reshape(Q_LORA, MLA_HEADS, MLA_NOPE + MLA_ROPE)
    nope = w[:, :, :MLA_NOPE].reshape(Q_LORA, -1)
    rope = w[:, :, MLA_NOPE:]
    rope_p = _pad_lanes(rope, LANES).reshape(Q_LORA, -1)
    rope_s = _pad_lanes(_pair_swap(rope), LANES).reshape(Q_LORA, -1)
    return jnp.concatenate([nope, rope_p, rope_s], axis=-1).astype(BF16)


def _prep_wkv(w_ukv):
    w = w_ukv.reshape(KV_LORA, MLA_HEADS, MLA_NOPE + MLA_V)
    return jnp.concatenate([w[:, :, :MLA_NOPE].reshape(KV_LORA, -1), w[:, :, MLA_NOPE:].reshape(KV_LORA, -1)], axis=-1).astype(BF16)


def kernel(x, c, ctx, c_ctx, w_ada, b_ada, w_in, mla_q_norm, w_uq, mla_kv_norm, w_ukv, hgrn_lb_logits, hgrn_norm,
           w_out, ln1_g, ln1_b, router_w, router_b, w1, b1, w2, b2, ln2_g, ln2_b):
    nb, seq, d = x.shape
    ctx_len = ctx.shape[1]
    depth = w_ada.shape[0]
    assert d == D_MODEL and seq % TM == 0 and ctx_len % TM == 0 and seq % GRID_W == 0
    r = seq + ctx_len
    tpb, nlt = r // TM, seq // TM
    n = nb * r
    n_tiles = -(-n * TOP_K // TMM) + N_EXPERTS
    alpha = float((2 * depth) ** 0.25)
    geo = dict(tpb=tpb, nlt=nlt, nb=nb)

    xs = jnp.concatenate([x, ctx], axis=1).reshape(n, d)

    mod_rows = -(-(nb + 1) // 8) * 8
    cc = jnp.zeros((mod_rows, d), F32).at[:nb].set(c).at[nb].set(c_ctx)
    mod_all = _ada(cc, w_ada, b_ada)

    lb = jnp.cumsum(jax.nn.softmax(hgrn_lb_logits.astype(F32), axis=0), axis=0)
    lb = lb - lb[:1]
    lb_rows = jnp.stack([jnp.log(lb), jnp.log1p(-lb), 1.0 - lb] + [jnp.zeros_like(lb)] * 5, axis=2)

    dft_c = _channel_dft()
    dft_lat = _dft_pair(seq)
    dft_ctx = _dft_pair(ctx_len)
    cos_t, sin_t = _rope_tables(nb, seq, ctx_len)

    w_in_all = _prep_in_proj(w_in)
    wq_all = jax.vmap(_prep_wq)(w_uq)
    wkv_all = jax.vmap(_prep_wkv)(w_ukv)
    w_out_all = w_out.astype(BF16)
    w1g_all, w1l_all = _prep_w1(w1)
    le = depth * N_EXPERTS
    w2_all = w2.reshape(le, EXPERT_FF, d)
    b1g_all = b1[:, :, 0::2].reshape(le, 1, EXPERT_FF)
    b1l_all = b1[:, :, 1::2].reshape(le, 1, EXPERT_FF)
    b2_all = b2.reshape(le, 1, d)
    rw_f32 = _pad_lanes(router_w, LANES)
    rw_hi = rw_f32.astype(BF16)
    rw_all = jnp.concatenate([rw_hi, (rw_f32 - rw_hi.astype(F32)).astype(BF16)], axis=-1)

    for l in range(depth):
        row = lambda a: a[l].reshape(1, -1)
        z, hg, mla_in = _pre(xs, mod_all, w_in_all, dft_c, l, **geo)
        yf = _fourier(z, dft_lat, dft_ctx, nb=nb, tpb=tpb, nlt=nlt, seq=seq, ctx_len=ctx_len)
        o_f = _hgrn(hg, lb_rows[l, 0], z_block=3, reverse=False, **geo)
        o_b = _hgrn(hg, lb_rows[l, 1], z_block=4, reverse=True, **geo)
        q, k, v = _mla_up(mla_in, row(mla_q_norm), row(mla_kv_norm), wq_all, wkv_all, cos_t, sin_t, l)
        att = _attention(q, k, v, nb=nb, seq=seq, ctx_len=ctx_len)

        rb = jnp.full((1, LANES), NEG_BIG, F32).at[0, :N_EXPERTS].set(router_b[l])
        x1, h2p, eid, gates = _post(yf, o_f, o_b, hg, att, xs, mod_all, row(hgrn_norm), w_out_all, row(ln1_g), row(ln1_b),
                                    rw_all, rb, l, alpha=alpha, **geo)

        pos, tile_e, n_used, fill_start, fill_n = _route(eid, n_tiles)
        xg = _dispatch(pos, fill_start, fill_n, n_used, h2p, n_tiles)
        ys = _moe_ffn(tile_e, n_used, xg, w1g_all, w1l_all, b1g_all, b1l_all, w2_all, b2_all, l)
        xs = _combine(pos, ys, gates, x1, mod_all, row(ln2_g), row(ln2_b), l, latent_only=l == depth - 1, alpha=alpha, **geo)

    return xs.reshape(nb, seq, d)
```

```python
import functools

import numpy as np
import jax
import jax.numpy as jnp
from jax import lax
from jax.experimental import pallas as pl
from jax.experimental.pallas import tpu as pltpu

F32 = jnp.float32
BF16 = jnp.bfloat16
U32 = jnp.uint32
I32 = jnp.int32

D_MODEL = 2048
FOURIER_W = 512
FOURIER_GROUP = 128
HGRN_W = 512
HGRN_HEADS = 4
HGRN_DIM = 128
MLA_HEADS = 8
MLA_NOPE = 128
MLA_ROPE = 64
MLA_V = 128
Q_LORA = 512
KV_LORA = 256
MLA_SCALE = (MLA_NOPE + MLA_ROPE) ** -0.5
N_EXPERTS = 32
TOP_K = 4
EXPERT_FF = 768
SWIGLU_ALPHA = 1.702
SWIGLU_LIMIT = 7.0
N_MOD = 6
GRID_W = 64
ROPE_BASE = 10000.0
EPS = 1e-6
LOG2_E = 1.4426950408889634

LANES = 128
SUBLANES = 8
TM = 256
TMM = 512
TQ = 2 * TM
HEAD_SLOT = 2 * LANES
HEADS_PER_STEP = 4
KEY_BLOCK = 256
IN_W = 4096
HG_OFF = FOURIER_W
MLA_OFF = HG_OFF + 5 * HGRN_W
MLA_IN_W = IN_W - MLA_OFF
N_LEVELS = 9
ADA_TN = 1024
W1_TN = 768
HALF = D_MODEL // 2
ROW_TILE = HALF // LANES
assert ROW_TILE == SUBLANES
NEG_BIG = -1e30
ISSUE_UNROLL = 8
VMEM_LIMIT = 56 * 1024 * 1024


def _cparams(sem, vmem=VMEM_LIMIT):
    return pltpu.CompilerParams(dimension_semantics=sem, vmem_limit_bytes=vmem)


def _dot(a, b):
    return jnp.dot(a, b, preferred_element_type=F32)


def _dot_nt(a, b):
    return lax.dot_general(a, b, (((1,), (1,)), ((), ())), preferred_element_type=F32)


def _dot_tn(a, b):
    return lax.dot_general(a, b, (((0,), (0,)), ((), ())), preferred_element_type=F32)


def _split_dot(x, m):
    hi = x.astype(BF16)
    lo = (x - hi.astype(F32)).astype(BF16)
    return _dot(hi, m) + _dot(lo, m)


def _layer_norm(x):
    mu = jnp.mean(x, axis=-1, keepdims=True)
    xc = x - mu
    var = jnp.mean(xc * xc, axis=-1, keepdims=True)
    return xc * lax.rsqrt(var + EPS)


def _rms(x):
    return x * lax.rsqrt(jnp.mean(x * x, axis=-1, keepdims=True) + EPS)


def _sigmoid(x):
    return 1.0 / (1.0 + jnp.exp(-x))


def _mod_row(i, tpb, nlt, nb):
    return jnp.where(i % tpb >= nlt, nb, i // tpb)


def _mod(mod_ref, row, k):
    return mod_ref[pl.ds(row, 1), k * D_MODEL:(k + 1) * D_MODEL]


def _ada_kernel(c_ref, w_ref, b_ref, o_ref):
    c = c_ref[...]
    s = c * _sigmoid(c)
    hi = s.astype(BF16)
    lo = (s - hi.astype(F32)).astype(BF16)
    w = w_ref[...].astype(BF16)
    o_ref[...] = _dot(hi, w) + _dot(lo, w) + b_ref[...]


def _ada(cc, w_ada, b_ada):
    depth, d, n = w_ada.shape
    rows = cc.shape[0]
    return pl.pallas_call(
        _ada_kernel,
        grid=(depth, n // ADA_TN),
        in_specs=[
            pl.BlockSpec((rows, d), lambda l, j: (0, 0)),
            pl.BlockSpec((None, d, ADA_TN), lambda l, j: (l, 0, j)),
            pl.BlockSpec((None, 1, ADA_TN), lambda l, j: (l, 0, j)),
        ],
        out_specs=pl.BlockSpec((None, rows, ADA_TN), lambda l, j: (l, 0, j)),
        out_shape=jax.ShapeDtypeStruct((depth, rows, n), F32),
        compiler_params=_cparams(("arbitrary", "arbitrary")),
        name="ada",
    )(cc, w_ada, b_ada.reshape(depth, 1, n))


def _pre_kernel(x_ref, mod_ref, w_ref, dft_ref, qg_ref, kvg_ref, wq_ref, wkv_ref, cos_ref, sin_ref,
                z_ref, hg_ref, q_ref, k_ref, v_ref, *, tpb, nlt, nb):
    row = _mod_row(pl.program_id(0), tpb, nlt, nb)
    h = _layer_norm(x_ref[...]) * (1.0 + _mod(mod_ref, row, 1)) + _mod(mod_ref, row, 0)
    r = _dot(h.astype(BF16), w_ref[...])
    z_ref[...] = _dot(r[:, 0:FOURIER_W].astype(BF16), dft_ref[...]).astype(BF16)
    hg_ref[...] = r[:, HG_OFF:MLA_OFF]
    _mla_heads(r[:, MLA_OFF:IN_W], qg_ref[...], kvg_ref[...], wq_ref, wkv_ref, cos_ref[...], sin_ref[...],
               q_ref, k_ref, v_ref)


def _layer_block(a, layer, **kw):
    return pl.BlockSpec((None,) + a.shape[1:], lambda *_: (layer, 0, 0), **kw)


def _pre(x, mod_all, w_in_all, dft_c, q_gain, kv_gain, wq_all, wkv_all, cos_t, sin_t, layer, *, tpb, nlt, nb):
    n = x.shape[0]
    w = MLA_HEADS * HEAD_SLOT
    kern = functools.partial(_pre_kernel, tpb=tpb, nlt=nlt, nb=nb)
    full = lambda a: pl.BlockSpec(a.shape, lambda i: (0, 0))
    rows = lambda width: pl.BlockSpec((TM, width), lambda i: (i, 0))
    once = dict(pipeline_mode=pl.Buffered(1))
    return pl.pallas_call(
        kern,
        grid=(n // TM,),
        in_specs=[
            rows(D_MODEL),
            _layer_block(mod_all, layer),
            _layer_block(w_in_all, layer, **once),
            full(dft_c), full(q_gain), full(kv_gain),
            _layer_block(wq_all, layer, **once), _layer_block(wkv_all, layer, **once),
            rows(LANES), rows(LANES),
        ],
        out_specs=[rows(2 * FOURIER_W), rows(5 * HGRN_W), rows(w), rows(w), rows(w)],
        out_shape=[
            jax.ShapeDtypeStruct((n, 2 * FOURIER_W), BF16),
            jax.ShapeDtypeStruct((n, 5 * HGRN_W), F32),
            jax.ShapeDtypeStruct((n, w), BF16),
            jax.ShapeDtypeStruct((n, w), BF16),
            jax.ShapeDtypeStruct((n, w), BF16),
        ],
        compiler_params=_cparams(("arbitrary",)),
        name="pre",
    )(x, mod_all, w_in_all, dft_c, q_gain, kv_gain, wq_all, wkv_all, cos_t, sin_t)


def _four_kernel(z_ref, cl_ref, sl_ref, cc_ref, sc_ref, y_ref, *, nlt, seq):
    j = pl.program_id(1)

    def emit(cm, sm, z):
        y = _dot(cm, z[:, 0:FOURIER_W]) - _dot(sm, z[:, FOURIER_W:])
        y_ref[...] = y.astype(BF16)

    @pl.when(j < nlt)
    def _():
        emit(cl_ref[...], sl_ref[...], z_ref[0:seq, :])

    @pl.when(j >= nlt)
    def _():
        emit(cc_ref[...], sc_ref[...], z_ref[seq:, :])


def _fourier(z, dft_lat, dft_ctx, *, nb, tpb, nlt, seq, ctx_len):
    n = z.shape[0]
    r = seq + ctx_len
    cl, sl = dft_lat
    cc, sc = dft_ctx
    kern = functools.partial(_four_kernel, nlt=nlt, seq=seq)
    lmap = lambda b, j: (jnp.minimum(j, nlt - 1), 0)
    cmap = lambda b, j: (jnp.maximum(j - nlt, 0), 0)
    return pl.pallas_call(
        kern,
        grid=(nb, tpb),
        in_specs=[
            pl.BlockSpec((None, r, 2 * FOURIER_W), lambda b, j: (b, 0, 0)),
            pl.BlockSpec((TM, seq), lmap),
            pl.BlockSpec((TM, seq), lmap),
            pl.BlockSpec((TM, ctx_len), cmap),
            pl.BlockSpec((TM, ctx_len), cmap),
        ],
        out_specs=pl.BlockSpec((TM, FOURIER_W), lambda b, j: (b * tpb + j, 0)),
        out_shape=jax.ShapeDtypeStruct((n, FOURIER_W), BF16),
        compiler_params=_cparams(("arbitrary", "arbitrary")),
        name="fourier",
    )(z.reshape(nb, r, 2 * FOURIER_W), cl, sl, cc, sc)


def _level_matrices(reverse):
    p = np.arange(TM)
    mats = [(p[None, :] <= p[:, None])]
    for lev in range(1, N_LEVELS):
        r = TM >> lev
        same = (p[None, :] // r) == (p[:, None] // r)
        is_q = ((p // r) % 2 == 1)[:, None]
        mats.append(same & np.where(is_q, p[None, :] <= p[:, None], p[None, :] > p[:, None]))
    g = np.stack(mats).astype(np.float32)
    if reverse:
        g = g[:, ::-1, ::-1]
    return jnp.asarray(g.reshape(N_LEVELS * TM, TM), dtype=BF16)


def _hgrn_kernel(q_ref, v_ref, z_ref, lb_ref, g_ref, o_ref, st_ref, *, reverse):
    @pl.when(pl.program_id(1) == 0)
    def _():
        st_ref[...] = jnp.zeros_like(st_ref)

    z = z_ref[...]
    log_lb = lb_ref[0:1, :]
    log_1m_lb = lb_ref[1:2, :]
    one_m_lb = lb_ref[2:3, :]
    log_sig = jnp.minimum(z, 0.0) - jnp.log1p(jnp.exp(-jnp.abs(z)))
    a2 = log_1m_lb + log_sig
    log_f = jnp.maximum(log_lb, a2) + jnp.log1p(jnp.exp(-jnp.abs(log_lb - a2)))
    kk = one_m_lb * _sigmoid(-z)

    wall = _split_dot_left(g_ref[...], log_f * LOG2_E)
    a = wall[0:TM]
    a_end = a[0:1] if reverse else a[TM - 1:TM]
    q = q_ref[...]
    v = v_ref[...].astype(BF16)
    qa = (q * jnp.exp2(a)).astype(BF16)
    kd = (kk * jnp.exp2(a_end - a)).astype(BF16)
    carry = jnp.exp2(a_end)

    half = TM // 2
    tiles = (slice(0, half), slice(half, TM))
    q_tile, k_tile = (0, 1) if reverse else (1, 0)
    row = lax.broadcasted_iota(I32, (half, half), 0)
    col = lax.broadcasted_iota(I32, (half, half), 1)
    diff = row ^ col
    pos = lax.broadcasted_iota(I32, (TM, HGRN_W), 0)
    if reverse:
        pos = TM - 1 - pos

    def is_q_piece(idx):
        return idx % 2 == (0 if reverse else 1)

    fac, fac_q, fac_k = {}, {}, {}
    for lev in range(1, N_LEVELS):
        r = TM >> lev
        fac[lev] = jnp.exp2(wall[lev * TM:(lev + 1) * TM])
        if r < SUBLANES:
            is_q = ((pos >> (r.bit_length() - 1)) & 1) == 1
            fac_q[lev] = jnp.where(is_q, fac[lev], 0.0)
            fac_k[lev] = jnp.where(is_q, 0.0, fac[lev])

    def level_operands(lev, qh, kh, sl, rows):
        r = TM >> lev
        if r < SUBLANES:
            return (qh[rows] * fac_q[lev][rows, sl]).astype(BF16), (kh[rows] * fac_k[lev][rows, sl]).astype(BF16)
        qs, ks = [], []
        for start in range(rows.start, rows.stop, r):
            piece = slice(start, start + r)
            zero = jnp.zeros((r, HGRN_DIM), F32)
            if is_q_piece(start // r):
                qs.append(qh[piece] * fac[lev][piece, sl])
                ks.append(zero)
            else:
                qs.append(zero)
                ks.append(kh[piece] * fac[lev][piece, sl])
        cat = lambda xs: (xs[0] if len(xs) == 1 else jnp.concatenate(xs, axis=0)).astype(BF16)
        return cat(qs), cat(ks)

    heads = [slice(h * HGRN_DIM, (h + 1) * HGRN_DIM) for h in range(HGRN_HEADS)]
    p = {}
    for h, sl in enumerate(heads):
        for d, rows in enumerate(tiles):
            p[h, d] = jnp.where(diff == 0, _dot_nt(q[rows, sl].astype(BF16), kk[rows, sl].astype(BF16)), 0.0)
    for lev in range(2, N_LEVELS):
        block = 2 * (TM >> lev)
        for h, sl in enumerate(heads):
            for d, rows in enumerate(tiles):
                qt, kt = level_operands(lev, q[:, sl], kk[:, sl], sl, rows)
                sc = _dot_nt(qt, kt)
                p[h, d] = p[h, d] + (sc if block == half else jnp.where((diff >> (block.bit_length() - 1)) == 0, sc, 0.0))

    for h, sl in enumerate(heads):
        st = st_ref[h]
        qh, kh, vh = q[:, sl], kk[:, sl], v[:, sl]
        p_off = _dot_nt((qh[tiles[q_tile]] * fac[1][tiles[q_tile], sl]).astype(BF16),
                        (kh[tiles[k_tile]] * fac[1][tiles[k_tile], sl]).astype(BF16))
        inter = _dot_nt(qa[:, sl], st.astype(BF16))
        for d, rows in enumerate(tiles):
            o = inter[rows] + _dot(p[h, d].astype(BF16), vh[rows])
            if d == q_tile:
                o = o + _dot(p_off.astype(BF16), vh[tiles[k_tile]])
            o_ref[rows, sl] = o
        st_ref[h] = st * carry[:, sl] + _dot_tn(vh, kd[:, sl])


def _split_dot_left(m, x):
    hi = x.astype(BF16)
    lo = (x - hi.astype(F32)).astype(BF16)
    return _dot(m, hi) + _dot(m, lo)


def _hgrn(hg, lb_rows, *, z_block, reverse, nb, tpb, nlt):
    n = hg.shape[0]
    nct = tpb - nlt

    def chunk(b, s):
        if reverse:
            j = jnp.where(s < nct, tpb - 1 - s, nlt - 1 - (s - nct))
        else:
            j = jnp.where(s < nct, nlt + s, s - nct)
        return b * tpb + j

    kern = functools.partial(_hgrn_kernel, reverse=reverse)
    gmat = _level_matrices(reverse)
    return pl.pallas_call(
        kern,
        grid=(nb, tpb),
        in_specs=[
            pl.BlockSpec((TM, HGRN_W), lambda b, s: (chunk(b, s), 0)),
            pl.BlockSpec((TM, HGRN_W), lambda b, s: (chunk(b, s), 1)),
            pl.BlockSpec((TM, HGRN_W), lambda b, s: (chunk(b, s), z_block)),
            pl.BlockSpec((8, HGRN_W), lambda b, s: (0, 0)),
            pl.BlockSpec(gmat.shape, lambda b, s: (0, 0)),
        ],
        out_specs=pl.BlockSpec((TM, HGRN_W), lambda b, s: (chunk(b, s), 0)),
        out_shape=jax.ShapeDtypeStruct((n, HGRN_W), F32),
        scratch_shapes=[pltpu.VMEM((HGRN_HEADS, HGRN_DIM, HGRN_DIM), F32)],
        compiler_params=_cparams(("arbitrary", "arbitrary")),
        name="hgrn_bwd" if reverse else "hgrn_fwd",
    )(hg, hg, hg, lb_rows, gmat)


def _mla_heads(m, q_gain, kv_gain, wq_ref, wkv_ref, cos, sin, q_ref, k_ref, v_ref):
    cq = m[:, 0:Q_LORA]
    ckv = m[:, Q_LORA:Q_LORA + KV_LORA]
    kr = m[:, Q_LORA + KV_LORA:Q_LORA + KV_LORA + LANES]
    kr_sw = m[:, Q_LORA + KV_LORA + LANES:MLA_IN_W]
    qall = _dot((_rms(cq) * q_gain).astype(BF16), wq_ref[...]) * (MLA_SCALE * LOG2_E)
    kv = _dot((_rms(ckv) * kv_gain).astype(BF16), wkv_ref[...])
    k_rot = (kr * cos + kr_sw * sin).astype(BF16)
    ones = jnp.ones((TM, LANES), BF16)
    nw = MLA_HEADS * LANES
    for h in range(MLA_HEADS):
        a, b = h * LANES, (h + 1) * LANES
        s = h * HEAD_SLOT
        q_ref[:, s:s + LANES] = qall[:, a:b].astype(BF16)
        q_ref[:, s + LANES:s + HEAD_SLOT] = (qall[:, nw + a:nw + b] * cos + qall[:, 2 * nw + a:2 * nw + b] * sin).astype(BF16)
        k_ref[:, s:s + LANES] = kv[:, a:b].astype(BF16)
        k_ref[:, s + LANES:s + HEAD_SLOT] = k_rot
        v_ref[:, s:s + LANES] = kv[:, nw + a:nw + b].astype(BF16)
        v_ref[:, s + LANES:s + HEAD_SLOT] = ones


def _attn_kernel(q_ref, k_ref, v_ref, o_ref, s_ref, *, n_lat_steps, seq, ctx_len):
    j = pl.program_id(2)

    def run(rows, k0, k1):
        blocks = [(kb, slice(k0 + kb * KEY_BLOCK, k0 + (kb + 1) * KEY_BLOCK), slice(kb * KEY_BLOCK, (kb + 1) * KEY_BLOCK))
                  for kb in range((k1 - k0) // KEY_BLOCK)]
        hslot = lambda hh: slice(hh * HEAD_SLOT, (hh + 1) * HEAD_SLOT)

        def scores(hh, blk, m_run):
            _, ks, ss = blk
            s = _dot_nt(q_ref[0:rows, hslot(hh)], k_ref[ks, hslot(hh)])
            s_ref[hh % 2, 0:rows, ss] = s
            for c in range(KEY_BLOCK // LANES):
                m_run = jnp.maximum(m_run, s[:, c * LANES:(c + 1) * LANES])
            return m_run

        def values(hh, blk, m, o):
            _, ks, ss = blk
            d = (s_ref[hh % 2, 0:rows, ss] - m).astype(BF16)
            return o + _dot(jnp.exp2(d), v_ref[ks, hslot(hh)])

        def emit(hh, o):
            o_ref[0:rows, hh * MLA_V:(hh + 1) * MLA_V] = (o[:, 0:MLA_V] / o[:, MLA_V:]).astype(BF16)

        neg = jnp.full((rows, LANES), -jnp.inf, F32)
        zero = jnp.zeros((rows, HEAD_SLOT), F32)
        m_run = neg
        for blk in blocks:
            m_run = scores(0, blk, m_run)
        m_prev = jnp.max(m_run, axis=-1, keepdims=True)
        for hh in range(1, HEADS_PER_STEP):
            m_run, o = neg, zero
            for blk in blocks:
                m_run = scores(hh, blk, m_run)
                o = values(hh - 1, blk, m_prev, o)
            emit(hh - 1, o)
            m_prev = jnp.max(m_run, axis=-1, keepdims=True)
        o = zero
        for blk in blocks:
            o = values(HEADS_PER_STEP - 1, blk, m_prev, o)
        emit(HEADS_PER_STEP - 1, o)

    @pl.when(j < n_lat_steps)
    def _():
        run(TQ, 0, seq + ctx_len)

    @pl.when(j >= n_lat_steps)
    def _():
        run(ctx_len, seq, seq + ctx_len)


def _attention(q, k, v, *, nb, seq, ctx_len):
    n = q.shape[0]
    r = seq + ctx_len
    assert seq % TQ == 0 and ctx_len <= TQ
    n_lat_steps = seq // TQ
    hp = MLA_HEADS // HEADS_PER_STEP
    wq = HEADS_PER_STEP * HEAD_SLOT
    wo = HEADS_PER_STEP * MLA_V
    kern = functools.partial(_attn_kernel, n_lat_steps=n_lat_steps, seq=seq, ctx_len=ctx_len)
    out = pl.pallas_call(
        kern,
        grid=(nb, hp, n_lat_steps + 1),
        in_specs=[
            pl.BlockSpec((None, TQ, wq), lambda b, h, j: (b, j, h)),
            pl.BlockSpec((None, r, wq), lambda b, h, j: (b, 0, h)),
            pl.BlockSpec((None, r, wq), lambda b, h, j: (b, 0, h), pipeline_mode=pl.Buffered(1)),
        ],
        out_specs=pl.BlockSpec((None, TQ, wo), lambda b, h, j: (b, j, h)),
        out_shape=jax.ShapeDtypeStruct((nb, r, MLA_HEADS * MLA_V), BF16),
        scratch_shapes=[pltpu.VMEM((2, TQ, r), F32)],
        compiler_params=_cparams(("arbitrary", "arbitrary", "arbitrary")),
        name="attention",
    )(q.reshape(nb, r, -1), k.reshape(nb, r, -1), v.reshape(nb, r, -1))
    return out.reshape(n, MLA_HEADS * MLA_V)


def _pack_rows(x):
    w = x.shape[1] // 2
    lo = pltpu.bitcast(x[:, :w].astype(BF16).astype(F32), U32)
    hi = pltpu.bitcast(x[:, w:].astype(BF16).astype(F32), U32)
    return (lo >> 16) | (hi & jnp.uint32(0xFFFF0000))


def _store_row_tiles(ref, packed):
    rows = packed.shape[0]
    for s in range(ROW_TILE):
        ref[pl.ds(s, rows, stride=ROW_TILE), :] = packed[:, s * LANES:(s + 1) * LANES]


def _load_row_tiles(ref, rows):
    return jnp.concatenate([ref[pl.ds(s, rows, stride=ROW_TILE), :] for s in range(ROW_TILE)], axis=1)


def _unpack_rows(w):
    lo = pltpu.bitcast(w << 16, F32)
    hi = pltpu.bitcast(w & jnp.uint32(0xFFFF0000), F32)
    return lo, hi


def _post_kernel(yf_ref, of_ref, ob_ref, g_ref, att_ref, x_ref, mod_ref, hn_ref, wo_ref, lng_ref, lnb_ref,
                 rw_ref, rb_ref, x1_ref, h2_ref, eid_ref, gate_ref, base_ref, cnt_ref, *, tpb, nlt, nb, alpha):
    row = _mod_row(pl.program_id(0), tpb, nlt, nb)
    o = of_ref[...] + ob_ref[...]
    hn = hn_ref[...]
    gate_act = g_ref[...]
    gate_act = gate_act * _sigmoid(gate_act)
    mix = _dot(yf_ref[...], wo_ref[0:FOURIER_W, :])
    for h in range(HGRN_HEADS):
        sl = slice(h * HGRN_DIM, (h + 1) * HGRN_DIM)
        r = (_rms(o[:, sl]) * hn[:, sl] * gate_act[:, sl]).astype(BF16)
        mix = mix + _dot(r, wo_ref[FOURIER_W + h * HGRN_DIM:FOURIER_W + (h + 1) * HGRN_DIM, :])
    mix = mix + _dot(att_ref[...], wo_ref[FOURIER_W + HGRN_W:, :])

    x1 = _layer_norm(alpha * x_ref[...] + _mod(mod_ref, row, 2) * mix) * lng_ref[...] + lnb_ref[...]
    x1_ref[...] = x1
    h2 = _layer_norm(x1) * (1.0 + _mod(mod_ref, row, 4)) + _mod(mod_ref, row, 3)
    _store_row_tiles(h2_ref, _pack_rows(h2))

    hi = h2.astype(BF16)
    lo = (h2 - hi.astype(F32)).astype(BF16)
    rr = _dot(jnp.concatenate([hi, lo], axis=0), rw_ref[...])
    logits = rr[0:TM, 0:LANES] + rr[0:TM, LANES:] + rr[TM:, 0:LANES] + rr[TM:, LANES:] + rb_ref[...]
    lane = lax.broadcasted_iota(I32, logits.shape, 1)
    cur = logits
    vals, ids = [], []
    for _ in range(TOP_K):
        m = jnp.max(cur, axis=-1, keepdims=True)
        idx = jnp.min(jnp.where(cur == m, lane, LANES), axis=-1, keepdims=True)
        vals.append(m)
        ids.append(idx)
        cur = jnp.where(lane == idx, -3e38, cur)
    ex = [jnp.exp(vk - vals[0]) for vk in vals]
    den = ex[0] + ex[1] + ex[2] + ex[3]
    eid = jnp.zeros(logits.shape, I32)
    gate = jnp.zeros(logits.shape, F32)
    for k in range(TOP_K):
        eid = jnp.where(lane == k, ids[k], eid)
        gate = jnp.where(lane == k, ex[k] / den, gate)
    eid_ref[...] = eid
    gate_ref[...] = gate

    i = pl.program_id(0)

    @pl.when(i == 0)
    def _():
        base_ref[...] = jnp.zeros_like(base_ref)
        cnt_ref[...] = jnp.zeros_like(cnt_ref)

    base_ref[pl.ds(i, 1), :] = cnt_ref[0:1, :]
    tile_cnt = jnp.zeros((1, LANES), F32)
    for k in range(TOP_K):
        tile_cnt = tile_cnt + jnp.sum((lane == ids[k]).astype(F32), axis=0, keepdims=True)
    cnt_ref[0:1, :] = cnt_ref[0:1, :] + tile_cnt


def _post(yf, o_f, o_b, hg, att, x, mod_all, hnorm, w_out_all, ln_g, ln_b, rw_all, rb, layer, *, tpb, nlt, nb, alpha):
    n = x.shape[0]
    nt_pad = -(-(n // TM) // SUBLANES) * SUBLANES
    kern = functools.partial(_post_kernel, tpb=tpb, nlt=nlt, nb=nb, alpha=alpha)
    full = lambda a: pl.BlockSpec(a.shape, lambda i: (0, 0))
    return pl.pallas_call(
        kern,
        grid=(n // TM,),
        in_specs=[
            pl.BlockSpec((TM, FOURIER_W), lambda i: (i, 0)),
            pl.BlockSpec((TM, HGRN_W), lambda i: (i, 0)),
            pl.BlockSpec((TM, HGRN_W), lambda i: (i, 0)),
            pl.BlockSpec((TM, HGRN_W), lambda i: (i, 2)),
            pl.BlockSpec((TM, MLA_HEADS * MLA_V), lambda i: (i, 0)),
            pl.BlockSpec((TM, D_MODEL), lambda i: (i, 0)),
            _layer_block(mod_all, layer), full(hnorm),
            _layer_block(w_out_all, layer, pipeline_mode=pl.Buffered(1)),
            full(ln_g), full(ln_b), _layer_block(rw_all, layer), full(rb),
        ],
        out_specs=[
            pl.BlockSpec((TM, D_MODEL), lambda i: (i, 0)),
            pl.BlockSpec((TM * ROW_TILE, LANES), lambda i: (i, 0)),
            pl.BlockSpec((TM, LANES), lambda i: (i, 0)),
            pl.BlockSpec((TM, LANES), lambda i: (i, 0)),
            pl.BlockSpec((nt_pad, LANES), lambda i: (0, 0)),
            pl.BlockSpec((SUBLANES, LANES), lambda i: (0, 0)),
        ],
        out_shape=[
            jax.ShapeDtypeStruct((n, D_MODEL), F32),
            jax.ShapeDtypeStruct((n * ROW_TILE, LANES), U32),
            jax.ShapeDtypeStruct((n, LANES), I32),
            jax.ShapeDtypeStruct((n, LANES), F32),
            jax.ShapeDtypeStruct((nt_pad, LANES), F32),
            jax.ShapeDtypeStruct((SUBLANES, LANES), F32),
        ],
        compiler_params=_cparams(("arbitrary",)),
        name="post",
    )(yf, o_f, o_b, hg, att, x, mod_all, hnorm, w_out_all, ln_g, ln_b, rw_all, rb)


def _route_kernel(eid_ref, base_ref, cnt_ref, pos_ref, te_ref, meta_ref, *, n_tiles_pad):
    i = pl.program_id(0)
    lane = lax.broadcasted_iota(I32, (TM, LANES), 1)
    eid = eid_ref[...]
    onehot = [(lane == eid[:, k:k + 1]).astype(F32) for k in range(TOP_K)]
    colsum = [jnp.sum(oh, axis=0, keepdims=True) for oh in onehot]

    cnt = cnt_ref[0:1, :]
    tiles = jnp.floor((cnt + (TMM - 1)) * (1.0 / TMM))
    r128 = lax.broadcasted_iota(I32, (LANES, LANES), 0)
    c128 = lax.broadcasted_iota(I32, (LANES, LANES), 1)
    incl = (r128 <= c128).astype(BF16)
    pad_end = _split_dot(jnp.broadcast_to(tiles, (8, LANES)), incl)[0:1, :] * TMM
    pad_start = pad_end - tiles * TMM

    rr = lax.broadcasted_iota(I32, (TM, TM), 0)
    cc = lax.broadcasted_iota(I32, (TM, TM), 1)
    strict = (cc < rr).astype(BF16)
    run = pad_start + base_ref[pl.ds(i, 1), :]
    pos = jnp.zeros((TM, LANES), I32)
    for k in range(TOP_K):
        before = _dot(strict, onehot[k].astype(BF16))
        val = jnp.sum(onehot[k] * (before + run), axis=1, keepdims=True)
        pos = jnp.where(lane == k, val.astype(I32), pos)
        run = run + colsum[k]
    pos_ref[...] = pos

    @pl.when(i == 0)
    def _():
        tstart = (lax.broadcasted_iota(I32, (n_tiles_pad, LANES), 0) * TMM).astype(F32)
        lane_t = lax.broadcasted_iota(I32, (n_tiles_pad, LANES), 1)
        done = jnp.where((pad_end <= tstart) & (lane_t < N_EXPERTS), 1.0, 0.0)
        te = jnp.minimum(jnp.sum(done, axis=1, keepdims=True), N_EXPERTS - 1.0)
        te_ref[...] = jnp.broadcast_to(te, (n_tiles_pad, LANES)).astype(I32)
        sub = lax.broadcasted_iota(I32, (8, LANES), 0)
        meta = jnp.where(sub == 0, pad_start + cnt, jnp.where(sub == 1, pad_end - pad_start - cnt, pad_end))
        meta_ref[...] = meta.astype(I32)


def _route(eid, base, cnt, n_tiles):
    n = eid.shape[0]
    n_tiles_pad = -(-n_tiles // 8) * 8
    kern = functools.partial(_route_kernel, n_tiles_pad=n_tiles_pad)
    full = lambda a: pl.BlockSpec(a.shape, lambda i: (0, 0))
    pos, te, meta = pl.pallas_call(
        kern,
        grid=(n // TM,),
        in_specs=[pl.BlockSpec((TM, LANES), lambda i: (i, 0)), full(base), full(cnt)],
        out_specs=[
            pl.BlockSpec((TM, LANES), lambda i: (i, 0)),
            pl.BlockSpec((n_tiles_pad, LANES), lambda i: (0, 0)),
            pl.BlockSpec((8, LANES), lambda i: (0, 0)),
        ],
        out_shape=[
            jax.ShapeDtypeStruct((n, LANES), I32),
            jax.ShapeDtypeStruct((n_tiles_pad, LANES), I32),
            jax.ShapeDtypeStruct((8, LANES), I32),
        ],
        compiler_params=_cparams(("arbitrary",)),
        name="moe_route",
    )(eid, base, cnt)
    pos_flat = pos[:, :TOP_K].reshape(-1)
    tile_e = te[:n_tiles, 0]
    fill_start = meta[0, :N_EXPERTS]
    fill_n = meta[1, :N_EXPERTS]
    n_used = meta[2, N_EXPERTS - 1:N_EXPERTS] // TMM
    return pos_flat, tile_e, n_used, fill_start, fill_n


def _dispatch_kernel(pos_ref, fs_ref, fn_ref, nu_ref, h_ref, o_hbm, zero_ref, sem, fill_sem, *, n_tiles):
    i = pl.program_id(0)

    def rows(ref, first, count):
        return ref.at[pl.ds(pl.multiple_of(first * ROW_TILE, ROW_TILE), count * ROW_TILE), :]

    def start(r, c):
        for k in range(TOP_K):
            p = pos_ref[(i * TM + r) * TOP_K + k]
            pltpu.make_async_copy(rows(h_ref, r, 1), rows(o_hbm, p, 1), sem).start()
        return c

    lax.fori_loop(0, TM, start, 0, unroll=ISSUE_UNROLL)

    @pl.when(i == 0)
    def _():
        zero_ref[...] = jnp.zeros_like(zero_ref)

        def fill(first, count):
            return pltpu.make_async_copy(rows(zero_ref, 0, count), rows(o_hbm, first, count), fill_sem)

        def each(fn):
            for e in range(N_EXPERTS):
                first, count = fs_ref[e], fn_ref[e]
                piece = TMM // 2
                while piece:
                    @pl.when((count & piece) != 0)
                    def _(first=first, piece=piece):
                        fn(fill(first, piece))

                    first = first + (count & piece)
                    piece //= 2
            lax.fori_loop(nu_ref[0], n_tiles, lambda t, c: (fn(fill(t * TMM, TMM)), c)[1], 0)

        each(lambda cp: cp.start())
        each(lambda cp: cp.wait())

    for _ in range(TOP_K):
        pltpu.make_async_copy(h_ref, rows(o_hbm, 0, TM), sem).wait()


def _dispatch(pos_flat, fill_start, fill_n, n_used, h2p, n_tiles):
    n = h2p.shape[0] // ROW_TILE
    return pl.pallas_call(
        functools.partial(_dispatch_kernel, n_tiles=n_tiles),
        grid_spec=pltpu.PrefetchScalarGridSpec(
            num_scalar_prefetch=4,
            grid=(n // TM,),
            in_specs=[pl.BlockSpec((TM * ROW_TILE, LANES), lambda i, p, fs, fn, nu: (i, 0))],
            out_specs=pl.BlockSpec(memory_space=pl.ANY),
            scratch_shapes=[pltpu.VMEM((TMM * ROW_TILE, LANES), U32), pltpu.SemaphoreType.DMA(()),
                            pltpu.SemaphoreType.DMA(())],
        ),
        out_shape=jax.ShapeDtypeStruct((n_tiles * TMM * ROW_TILE, LANES), U32),
        compiler_params=_cparams(("arbitrary",)),
        name="moe_dispatch",
    )(pos_flat, fill_start, fill_n, n_used, h2p)


def _w1_kernel(w_ref, g_ref, l_ref, t_ref):
    for c in range(D_MODEL // LANES):
        cs = slice(c * LANES, (c + 1) * LANES)
        t_ref[c] = w_ref[cs, :].T
        g_ref[:, cs] = t_ref[c, pl.ds(0, W1_TN // 2, stride=2), :].astype(BF16)
        l_ref[:, cs] = t_ref[c, pl.ds(1, W1_TN // 2, stride=2), :].astype(BF16)


def _prep_w1(w1):
    depth, ne, d, n2 = w1.shape
    out = jax.ShapeDtypeStruct((depth * ne, n2 // 2, d), BF16)
    g, l = pl.pallas_call(
        _w1_kernel,
        grid=(depth * ne, n2 // W1_TN),
        in_specs=[pl.BlockSpec((None, d, W1_TN), lambda e, j: (e, 0, j))],
        out_specs=[pl.BlockSpec((None, W1_TN // 2, d), lambda e, j: (e, j, 0))] * 2,
        out_shape=[out, out],
        scratch_shapes=[pltpu.VMEM((d // LANES, W1_TN, LANES), F32)],
        compiler_params=_cparams(("arbitrary", "arbitrary")),
        name="moe_w1_prep",
    )(w1.reshape(depth * ne, d, n2))
    return g, l


def _moe_kernel(te_ref, nu_ref, x_ref, w1g_ref, w1l_ref, b1g_ref, b1l_ref, w2_ref, b2_ref, y_ref, w2b_ref):
    i = pl.program_id(0)

    @pl.when(i < nu_ref[0])
    def _():
        @pl.when((i == 0) | (te_ref[i] != te_ref[jnp.maximum(i - 1, 0)]))
        def _():
            w2b_ref[...] = w2_ref[...].astype(BF16)

        lo, hi = _unpack_rows(_load_row_tiles(x_ref, TMM))
        lo = lo.astype(BF16)
        hi = hi.astype(BF16)
        ug = _dot_nt(lo, w1g_ref[:, 0:HALF]) + _dot_nt(hi, w1g_ref[:, HALF:]) + b1g_ref[...]
        ul = _dot_nt(lo, w1l_ref[:, 0:HALF]) + _dot_nt(hi, w1l_ref[:, HALF:]) + b1l_ref[...]
        xg = jnp.minimum(ug, SWIGLU_LIMIT)
        xl = jnp.clip(ul, -SWIGLU_LIMIT, SWIGLU_LIMIT)
        act = xg * _sigmoid(SWIGLU_ALPHA * xg) * (xl + 1.0)
        y = _dot(act.astype(BF16), w2b_ref[...]) + b2_ref[...]
        _store_row_tiles(y_ref, _pack_rows(y))

    @pl.when(i >= nu_ref[0])
    def _():
        y_ref[...] = jnp.zeros_like(y_ref)


def _moe_ffn(tile_e, n_used, xs, w1g, w1l, b1g, b1l, w2, b2, layer):
    m = xs.shape[0] // ROW_TILE
    emap = lambda i, te, nu: (layer * N_EXPERTS + te[i], 0, 0)
    row_tiles = pl.BlockSpec((TMM * ROW_TILE, LANES), lambda i, te, nu: (i, 0))
    return pl.pallas_call(
        _moe_kernel,
        grid_spec=pltpu.PrefetchScalarGridSpec(
            num_scalar_prefetch=2,
            grid=(m // TMM,),
            in_specs=[
                row_tiles,
                pl.BlockSpec((None, EXPERT_FF, D_MODEL), emap),
                pl.BlockSpec((None, EXPERT_FF, D_MODEL), emap),
                pl.BlockSpec((None, 1, EXPERT_FF), emap),
                pl.BlockSpec((None, 1, EXPERT_FF), emap),
                pl.BlockSpec((None, EXPERT_FF, D_MODEL), emap),
                pl.BlockSpec((None, 1, D_MODEL), emap),
            ],
            out_specs=row_tiles,
            scratch_shapes=[pltpu.VMEM((EXPERT_FF, D_MODEL), BF16)],
        ),
        out_shape=jax.ShapeDtypeStruct((m * ROW_TILE, LANES), U32),
        compiler_params=_cparams(("arbitrary",)),
        name="moe_ffn",
    )(tile_e, n_used, xs, w1g, w1l, b1g, b1l, w2, b2)


def _combine_kernel(pos_ref, y_hbm, gate_ref, x_ref, mod_ref, lng_ref, lnb_ref, o_ref, buf, sem, *, tpb, nlt, nb, alpha,
                    latent_only):
    i = pl.program_id(0)
    n_steps = pl.num_programs(0)
    row = _mod_row(i, tpb, nlt, nb)

    def issue(tile, slot):
        def start(r, c):
            for k in range(TOP_K):
                p = pos_ref[(tile * TM + r) * TOP_K + k]
                src = y_hbm.at[pl.ds(pl.multiple_of(p * ROW_TILE, ROW_TILE), ROW_TILE), :]
                dst = buf.at[slot, k, pl.ds(pl.multiple_of(r * ROW_TILE, ROW_TILE), ROW_TILE), :]
                pltpu.make_async_copy(src, dst, sem.at[slot]).start()
            return c

        lax.fori_loop(0, TM, start, 0, unroll=ISSUE_UNROLL)

    @pl.when(i == 0)
    def _():
        issue(0, 0)

    @pl.when(i + 1 < n_steps)
    def _():
        issue(i + 1, (i + 1) % 2)

    slot = i % 2
    for k in range(TOP_K):
        pltpu.make_async_copy(y_hbm.at[pl.ds(0, TM * ROW_TILE), :], buf.at[slot, k], sem.at[slot]).wait()

    gates = gate_ref[...]
    acc_lo = jnp.zeros((TM, HALF), F32)
    acc_hi = jnp.zeros((TM, HALF), F32)
    for k in range(TOP_K):
        lo, hi = _unpack_rows(_load_row_tiles(buf.at[slot, k], TM))
        gk = gates[:, k:k + 1]
        acc_lo = acc_lo + gk * lo
        acc_hi = acc_hi + gk * hi
    ffn = jnp.concatenate([acc_lo, acc_hi], axis=-1)
    x2 = _layer_norm(alpha * x_ref[...] + _mod(mod_ref, row, 5) * ffn) * lng_ref[...] + lnb_ref[...]
    if latent_only:
        @pl.when(i % tpb < nlt)
        def _():
            o_ref[...] = x2
    else:
        o_ref[...] = x2


def _combine(pos, ys, gates, x, mod_all, ln_g, ln_b, layer, *, latent_only, tpb, nlt, nb, alpha):
    n = x.shape[0]
    kern = functools.partial(_combine_kernel, tpb=tpb, nlt=nlt, nb=nb, alpha=alpha, latent_only=latent_only)
    full = lambda a: pl.BlockSpec(a.shape, lambda i, p: (0, 0))
    if latent_only:
        omap = lambda i, p: ((i // tpb) * nlt + jnp.minimum(i % tpb, nlt - 1), 0)
        n_out = nb * nlt * TM
    else:
        omap = lambda i, p: (i, 0)
        n_out = n
    return pl.pallas_call(
        kern,
        grid_spec=pltpu.PrefetchScalarGridSpec(
            num_scalar_prefetch=1,
            grid=(n // TM,),
            in_specs=[
                pl.BlockSpec(memory_space=pl.ANY),
                pl.BlockSpec((TM, LANES), lambda i, p: (i, 0)),
                pl.BlockSpec((TM, D_MODEL), lambda i, p: (i, 0)),
                _layer_block(mod_all, layer), full(ln_g), full(ln_b),
            ],
            out_specs=pl.BlockSpec((TM, D_MODEL), omap),
            scratch_shapes=[pltpu.VMEM((2, TOP_K, TM * ROW_TILE, LANES), U32), pltpu.SemaphoreType.DMA((2,))],
        ),
        out_shape=jax.ShapeDtypeStruct((n_out, D_MODEL), F32),
        compiler_params=_cparams(("arbitrary",)),
        name="moe_combine",
    )(pos, ys, gates, x, mod_all, ln_g, ln_b)


def _dft_pair(t_len):
    assert t_len % GRID_W == 0
    k = jnp.arange(t_len, dtype=I32)

    def table(t):
        ang = ((t[:, None] * k[None, :]) % t_len).astype(F32) * (2.0 * np.pi / t_len)
        return jnp.cos(ang), jnp.sin(ang)

    ca, sa = table(jnp.arange(t_len // GRID_W, dtype=I32) * GRID_W)
    cb, sb = table(jnp.arange(GRID_W, dtype=I32))
    s = 1.0 / np.sqrt(t_len)
    cos = ca[:, None, :] * cb[None, :, :] - sa[:, None, :] * sb[None, :, :]
    sin = sa[:, None, :] * cb[None, :, :] + ca[:, None, :] * sb[None, :, :]
    return (cos * s).reshape(t_len, t_len).astype(BF16), (sin * s).reshape(t_len, t_len).astype(BF16)


def _channel_dft():
    c = np.arange(FOURIER_GROUP)
    ang = 2.0 * np.pi * ((c[:, None] * c[None, :]) % FOURIER_GROUP) / FOURIER_GROUP
    s = 1.0 / np.sqrt(FOURIER_GROUP)
    groups = FOURIER_W // FOURIER_GROUP
    m = np.zeros((FOURIER_W, 2 * FOURIER_W), np.float32)
    for g in range(groups):
        a, b = g * FOURIER_GROUP, (g + 1) * FOURIER_GROUP
        m[a:b, a:b] = np.cos(ang) * s
        m[a:b, FOURIER_W + a:FOURIER_W + b] = np.sin(ang) * s
    return jnp.asarray(m, dtype=BF16)


def _pair_swap(w):
    pairs = w.reshape(*w.shape[:-1], w.shape[-1] // 2, 2)
    return jnp.stack([-pairs[..., 1], pairs[..., 0]], axis=-1).reshape(w.shape)


def _pad_lanes(w, width):
    return jnp.pad(w, [(0, 0)] * (w.ndim - 1) + [(0, width - w.shape[-1])])


def _rope_tables(nb, seq, ctx_len):
    t = jnp.arange(seq)
    n_freq = MLA_ROPE // 4
    inv = ROPE_BASE ** (-jnp.arange(n_freq, dtype=F32) / n_freq)
    ang = jnp.concatenate([(t // GRID_W)[:, None] * inv, (t % GRID_W)[:, None] * inv], axis=-1)
    ang = jnp.repeat(ang, 2, axis=-1)
    cos = jnp.concatenate([jnp.cos(ang), jnp.ones((ctx_len, MLA_ROPE), F32)], axis=0)
    sin = jnp.concatenate([jnp.sin(ang), jnp.zeros((ctx_len, MLA_ROPE), F32)], axis=0)
    cos = jnp.tile(_pad_lanes(cos, LANES), (nb, 1))
    sin = jnp.tile(_pad_lanes(sin, LANES), (nb, 1))
    return cos, sin


def _prep_in_proj(w_in):
    f, q, i, g, zf, zb, cq, ckv, kr = jnp.split(w_in, np.cumsum([512, 512, 512, 512, 512, 512, 512, 256]).tolist(), axis=-1)
    cols = [f, q, i, g, zf, zb, cq, ckv, _pad_lanes(kr, LANES), _pad_lanes(_pair_swap(kr), LANES)]
    return jnp.concatenate(cols, axis=-1).astype(BF16)


def _prep_wq(w_uq):
    w = w_uq.reshape(Q_LORA, MLA_HEADS, MLA_NOPE + MLA_ROPE)
    nope = w[:, :, :MLA_NOPE].reshape(Q_LORA, -1)
    rope = w[:, :, MLA_NOPE:]
    rope_p = _pad_lanes(rope, LANES).reshape(Q_LORA, -1)
    rope_s = _pad_lanes(_pair_swap(rope), LANES).reshape(Q_LORA, -1)
    return jnp.concatenate([nope, rope_p, rope_s], axis=-1).astype(BF16)


def _prep_wkv(w_ukv):
    w = w_ukv.reshape(KV_LORA, MLA_HEADS, MLA_NOPE + MLA_V)
    return jnp.concatenate([w[:, :, :MLA_NOPE].reshape(KV_LORA, -1), w[:, :, MLA_NOPE:].reshape(KV_LORA, -1)], axis=-1).astype(BF16)


def kernel(x, c, ctx, c_ctx, w_ada, b_ada, w_in, mla_q_norm, w_uq, mla_kv_norm, w_ukv, hgrn_lb_logits, hgrn_norm,
           w_out, ln1_g, ln1_b, router_w, router_b, w1, b1, w2, b2, ln2_g, ln2_b):
    nb, seq, d = x.shape
    ctx_len = ctx.shape[1]
    depth = w_ada.shape[0]
    assert d == D_MODEL and seq % TM == 0 and ctx_len % TM == 0 and seq % GRID_W == 0
    r = seq + ctx_len
    tpb, nlt = r // TM, seq // TM
    n = nb * r
    n_tiles = -(-n * TOP_K // TMM) + N_EXPERTS
    alpha = float((2 * depth) ** 0.25)
    geo = dict(tpb=tpb, nlt=nlt, nb=nb)

    xs = jnp.concatenate([x, ctx], axis=1).reshape(n, d)

    mod_rows = -(-(nb + 1) // 8) * 8
    cc = jnp.zeros((mod_rows, d), F32).at[:nb].set(c).at[nb].set(c_ctx)
    mod_all = _ada(cc, w_ada, b_ada)

    lb = jnp.cumsum(jax.nn.softmax(hgrn_lb_logits.astype(F32), axis=0), axis=0)
    lb = lb - lb[:1]
    lb_rows = jnp.stack([jnp.log(lb), jnp.log1p(-lb), 1.0 - lb] + [jnp.zeros_like(lb)] * 5, axis=2)

    dft_c = _channel_dft()
    dft_lat = _dft_pair(seq)
    dft_ctx = _dft_pair(ctx_len)
    cos_t, sin_t = _rope_tables(nb, seq, ctx_len)

    w_in_all = _prep_in_proj(w_in)
    wq_all = jax.vmap(_prep_wq)(w_uq)
    wkv_all = jax.vmap(_prep_wkv)(w_ukv)
    w_out_all = w_out.astype(BF16)
    w1g_all, w1l_all = _prep_w1(w1)
    le = depth * N_EXPERTS
    w2_all = w2.reshape(le, EXPERT_FF, d)
    b1g_all = b1[:, :, 0::2].reshape(le, 1, EXPERT_FF)
    b1l_all = b1[:, :, 1::2].reshape(le, 1, EXPERT_FF)
    b2_all = b2.reshape(le, 1, d)
    rw_f32 = _pad_lanes(router_w, LANES)
    rw_hi = rw_f32.astype(BF16)
    rw_all = jnp.concatenate([rw_hi, (rw_f32 - rw_hi.astype(F32)).astype(BF16)], axis=-1)

    for l in range(depth):
        row = lambda a: a[l].reshape(1, -1)
        z, hg, q, k, v = _pre(xs, mod_all, w_in_all, dft_c, row(mla_q_norm), row(mla_kv_norm), wq_all, wkv_all,
                              cos_t, sin_t, l, **geo)
        yf = _fourier(z, dft_lat, dft_ctx, nb=nb, tpb=tpb, nlt=nlt, seq=seq, ctx_len=ctx_len)
        o_f = _hgrn(hg, lb_rows[l, 0], z_block=3, reverse=False, **geo)
        o_b = _hgrn(hg, lb_rows[l, 1], z_block=4, reverse=True, **geo)
        att = _attention(q, k, v, nb=nb, seq=seq, ctx_len=ctx_len)

        rb = jnp.full((1, LANES), NEG_BIG, F32).at[0, :N_EXPERTS].set(router_b[l])
        x1, h2p, eid, gates, base, cnt = _post(yf, o_f, o_b, hg, att, xs, mod_all, row(hgrn_norm), w_out_all, row(ln1_g), row(ln1_b),
                                    rw_all, rb, l, alpha=alpha, **geo)

        pos, tile_e, n_used, fill_start, fill_n = _route(eid, base, cnt, n_tiles)
        xg = _dispatch(pos, fill_start, fill_n, n_used, h2p, n_tiles)
        ys = _moe_ffn(tile_e, n_used, xg, w1g_all, w1l_all, b1g_all, b1l_all, w2_all, b2_all, l)
        xs = _combine(pos, ys, gates, x1, mod_all, row(ln2_g), row(ln2_b), l, latent_only=l == depth - 1, alpha=alpha, **geo)

    return xs.reshape(nb, seq, d)
```

```python
import functools

import numpy as np
import jax
import jax.numpy as jnp
from jax import lax
from jax.experimental import pallas as pl
from jax.experimental.pallas import tpu as pltpu

F32 = jnp.float32
BF16 = jnp.bfloat16
U32 = jnp.uint32
I32 = jnp.int32

D_MODEL = 2048
FOURIER_W = 512
FOURIER_GROUP = 128
HGRN_W = 512
HGRN_HEADS = 4
HGRN_DIM = 128
MLA_HEADS = 8
MLA_NOPE = 128
MLA_ROPE = 64
MLA_V = 128
Q_LORA = 512
KV_LORA = 256
MLA_SCALE = (MLA_NOPE + MLA_ROPE) ** -0.5
N_EXPERTS = 32
TOP_K = 4
EXPERT_FF = 768
SWIGLU_ALPHA = 1.702
SWIGLU_LIMIT = 7.0
N_MOD = 6
GRID_W = 64
ROPE_BASE = 10000.0
EPS = 1e-6
LOG2_E = 1.4426950408889634

LANES = 128
SUBLANES = 8
TM = 256
TMM = 512
TQ = 2 * TM
HEAD_SLOT = 2 * LANES
HEADS_PER_STEP = 4
KEY_BLOCK = 256
IN_W = 4096
HG_OFF = FOURIER_W
MLA_OFF = HG_OFF + 5 * HGRN_W
MLA_IN_W = IN_W - MLA_OFF
N_LEVELS = 9
ADA_TN = 1024
W1_TN = 768
HALF = D_MODEL // 2
ROW_TILE = HALF // LANES
assert ROW_TILE == SUBLANES
NEG_BIG = -1e30
ISSUE_UNROLL = 8
VMEM_LIMIT = 56 * 1024 * 1024


def _cparams(sem, vmem=VMEM_LIMIT):
    return pltpu.CompilerParams(dimension_semantics=sem, vmem_limit_bytes=vmem)


def _dot(a, b):
    return jnp.dot(a, b, preferred_element_type=F32)


def _dot_nt(a, b):
    return lax.dot_general(a, b, (((1,), (1,)), ((), ())), preferred_element_type=F32)


def _dot_tn(a, b):
    return lax.dot_general(a, b, (((0,), (0,)), ((), ())), preferred_element_type=F32)


def _split_dot(x, m):
    hi = x.astype(BF16)
    lo = (x - hi.astype(F32)).astype(BF16)
    return _dot(hi, m) + _dot(lo, m)


def _layer_norm(x):
    mu = jnp.mean(x, axis=-1, keepdims=True)
    xc = x - mu
    var = jnp.mean(xc * xc, axis=-1, keepdims=True)
    return xc * lax.rsqrt(var + EPS)


def _rms(x):
    return x * lax.rsqrt(jnp.mean(x * x, axis=-1, keepdims=True) + EPS)


def _sigmoid(x):
    return 1.0 / (1.0 + jnp.exp(-x))


def _mod_row(i, tpb, nlt, nb):
    return jnp.where(i % tpb >= nlt, nb, i // tpb)


def _mod(mod_ref, row, k):
    return mod_ref[pl.ds(row, 1), k * D_MODEL:(k + 1) * D_MODEL]


def _ada_kernel(c_ref, w_ref, b_ref, o_ref):
    c = c_ref[...]
    s = c * _sigmoid(c)
    hi = s.astype(BF16)
    lo = (s - hi.astype(F32)).astype(BF16)
    w = w_ref[...].astype(BF16)
    o_ref[...] = _dot(hi, w) + _dot(lo, w) + b_ref[...]


def _ada(cc, w_ada, b_ada):
    depth, d, n = w_ada.shape
    rows = cc.shape[0]
    return pl.pallas_call(
        _ada_kernel,
        grid=(depth, n // ADA_TN),
        in_specs=[
            pl.BlockSpec((rows, d), lambda l, j: (0, 0)),
            pl.BlockSpec((None, d, ADA_TN), lambda l, j: (l, 0, j)),
            pl.BlockSpec((None, 1, ADA_TN), lambda l, j: (l, 0, j)),
        ],
        out_specs=pl.BlockSpec((None, rows, ADA_TN), lambda l, j: (l, 0, j)),
        out_shape=jax.ShapeDtypeStruct((depth, rows, n), F32),
        compiler_params=_cparams(("arbitrary", "arbitrary")),
        name="ada",
    )(cc, w_ada, b_ada.reshape(depth, 1, n))


def _pre_kernel(x_ref, mod_ref, w_ref, dft_ref, qg_ref, kvg_ref, wq_ref, wkv_ref, cos_ref, sin_ref,
                z_ref, hg_ref, hz_ref, q_ref, k_ref, v_ref, *, tpb, nlt, nb):
    row = _mod_row(pl.program_id(0), tpb, nlt, nb)
    h = _layer_norm(x_ref[...]) * (1.0 + _mod(mod_ref, row, 1)) + _mod(mod_ref, row, 0)
    r = _dot(h.astype(BF16), w_ref[...])
    z_ref[...] = _dot(r[:, 0:FOURIER_W].astype(BF16), dft_ref[...]).astype(BF16)
    hg_ref[...] = r[:, HG_OFF:HG_OFF + 3 * HGRN_W].astype(BF16)
    hz_ref[...] = r[:, HG_OFF + 3 * HGRN_W:MLA_OFF]
    _mla_heads(r[:, MLA_OFF:IN_W], qg_ref[...], kvg_ref[...], wq_ref, wkv_ref, cos_ref[...], sin_ref[...],
               q_ref, k_ref, v_ref)


def _layer_block(a, layer, **kw):
    return pl.BlockSpec((None,) + a.shape[1:], lambda *_: (layer, 0, 0), **kw)


def _pre(x, mod_all, w_in_all, dft_c, q_gain, kv_gain, wq_all, wkv_all, cos_t, sin_t, layer, *, tpb, nlt, nb):
    n = x.shape[0]
    w = MLA_HEADS * HEAD_SLOT
    kern = functools.partial(_pre_kernel, tpb=tpb, nlt=nlt, nb=nb)
    full = lambda a: pl.BlockSpec(a.shape, lambda i: (0, 0))
    rows = lambda width: pl.BlockSpec((TM, width), lambda i: (i, 0))
    once = dict(pipeline_mode=pl.Buffered(1))
    return pl.pallas_call(
        kern,
        grid=(n // TM,),
        in_specs=[
            rows(D_MODEL),
            _layer_block(mod_all, layer),
            _layer_block(w_in_all, layer, **once),
            full(dft_c), full(q_gain), full(kv_gain),
            _layer_block(wq_all, layer, **once), _layer_block(wkv_all, layer, **once),
            rows(LANES), rows(LANES),
        ],
        out_specs=[rows(2 * FOURIER_W), rows(3 * HGRN_W), rows(2 * HGRN_W), rows(w), rows(w), rows(w)],
        out_shape=[
            jax.ShapeDtypeStruct((n, 2 * FOURIER_W), BF16),
            jax.ShapeDtypeStruct((n, 3 * HGRN_W), BF16),
            jax.ShapeDtypeStruct((n, 2 * HGRN_W), F32),
            jax.ShapeDtypeStruct((n, w), BF16),
            jax.ShapeDtypeStruct((n, w), BF16),
            jax.ShapeDtypeStruct((n, w), BF16),
        ],
        compiler_params=_cparams(("arbitrary",)),
        name="pre",
    )(x, mod_all, w_in_all, dft_c, q_gain, kv_gain, wq_all, wkv_all, cos_t, sin_t)


def _four_kernel(z_ref, cl_ref, sl_ref, cc_ref, sc_ref, y_ref, *, nlt, seq):
    j = pl.program_id(1)

    def emit(cm, sm, z):
        y = _dot(cm, z[:, 0:FOURIER_W]) - _dot(sm, z[:, FOURIER_W:])
        y_ref[...] = y.astype(BF16)

    @pl.when(j < nlt)
    def _():
        emit(cl_ref[...], sl_ref[...], z_ref[0:seq, :])

    @pl.when(j >= nlt)
    def _():
        emit(cc_ref[...], sc_ref[...], z_ref[seq:, :])


def _fourier(z, dft_lat, dft_ctx, *, nb, tpb, nlt, seq, ctx_len):
    n = z.shape[0]
    r = seq + ctx_len
    cl, sl = dft_lat
    cc, sc = dft_ctx
    kern = functools.partial(_four_kernel, nlt=nlt, seq=seq)
    lmap = lambda b, j: (jnp.minimum(j, nlt - 1), 0)
    cmap = lambda b, j: (jnp.maximum(j - nlt, 0), 0)
    return pl.pallas_call(
        kern,
        grid=(nb, tpb),
        in_specs=[
            pl.BlockSpec((None, r, 2 * FOURIER_W), lambda b, j: (b, 0, 0)),
            pl.BlockSpec((TM, seq), lmap),
            pl.BlockSpec((TM, seq), lmap),
            pl.BlockSpec((TM, ctx_len), cmap),
            pl.BlockSpec((TM, ctx_len), cmap),
        ],
        out_specs=pl.BlockSpec((TM, FOURIER_W), lambda b, j: (b * tpb + j, 0)),
        out_shape=jax.ShapeDtypeStruct((n, FOURIER_W), BF16),
        compiler_params=_cparams(("arbitrary", "arbitrary")),
        name="fourier",
    )(z.reshape(nb, r, 2 * FOURIER_W), cl, sl, cc, sc)


def _level_matrices(reverse):
    p = np.arange(TM)
    mats = [(p[None, :] <= p[:, None])]
    for lev in range(1, N_LEVELS):
        r = TM >> lev
        same = (p[None, :] // r) == (p[:, None] // r)
        is_q = ((p // r) % 2 == 1)[:, None]
        mats.append(same & np.where(is_q, p[None, :] <= p[:, None], p[None, :] > p[:, None]))
    g = np.stack(mats).astype(np.float32)
    if reverse:
        g = g[:, ::-1, ::-1]
    return jnp.asarray(g.reshape(N_LEVELS * TM, TM), dtype=BF16)


def _hgrn_kernel(qf_ref, vf_ref, zf_ref, lbf_ref, gf_ref, qb_ref, vb_ref, zb_ref, lbb_ref, gb_ref,
                 of_ref, ob_ref, stf_ref, stb_ref):
    @pl.when(pl.program_id(1) == 0)
    def _():
        stf_ref[...] = jnp.zeros_like(stf_ref)
        stb_ref[...] = jnp.zeros_like(stb_ref)

    gates_f = _hgrn_gates(zf_ref, lbf_ref, gf_ref)
    gates_b = _hgrn_gates(zb_ref, lbb_ref, gb_ref)
    _hgrn_chunk(qf_ref, vf_ref, gates_f, of_ref, stf_ref, reverse=False)
    _hgrn_chunk(qb_ref, vb_ref, gates_b, ob_ref, stb_ref, reverse=True)


def _hgrn_gates(z_ref, lb_ref, g_ref):
    z = z_ref[...]
    log_lb = lb_ref[0:1, :]
    log_1m_lb = lb_ref[1:2, :]
    one_m_lb = lb_ref[2:3, :]
    log_sig = jnp.minimum(z, 0.0) - jnp.log1p(jnp.exp(-jnp.abs(z)))
    a2 = log_1m_lb + log_sig
    log_f = jnp.maximum(log_lb, a2) + jnp.log1p(jnp.exp(-jnp.abs(log_lb - a2)))
    kk = one_m_lb * _sigmoid(-z)

    return kk, _split_dot_left(g_ref[...], log_f * LOG2_E)


def _hgrn_chunk(q_ref, v_ref, gates, o_ref, st_ref, *, reverse):
    kk, wall = gates
    a = wall[0:TM]
    a_end = a[0:1] if reverse else a[TM - 1:TM]
    q = q_ref[...].astype(F32)
    v = v_ref[...]
    qa = (q * jnp.exp2(a)).astype(BF16)
    kd = (kk * jnp.exp2(a_end - a)).astype(BF16)
    carry = jnp.exp2(a_end)

    half = TM // 2
    tiles = (slice(0, half), slice(half, TM))
    q_tile, k_tile = (0, 1) if reverse else (1, 0)
    row = lax.broadcasted_iota(I32, (half, half), 0)
    col = lax.broadcasted_iota(I32, (half, half), 1)
    diff = row ^ col
    pos = lax.broadcasted_iota(I32, (TM, HGRN_W), 0)
    if reverse:
        pos = TM - 1 - pos

    def is_q_piece(idx):
        return idx % 2 == (0 if reverse else 1)

    fac, fac_q, fac_k = {}, {}, {}
    for lev in range(1, N_LEVELS):
        r = TM >> lev
        fac[lev] = jnp.exp2(wall[lev * TM:(lev + 1) * TM])
        if r < SUBLANES:
            is_q = ((pos >> (r.bit_length() - 1)) & 1) == 1
            fac_q[lev] = jnp.where(is_q, fac[lev], 0.0)
            fac_k[lev] = jnp.where(is_q, 0.0, fac[lev])

    def level_operands(lev, qh, kh, sl, rows):
        r = TM >> lev
        if r < SUBLANES:
            return (qh[rows] * fac_q[lev][rows, sl]).astype(BF16), (kh[rows] * fac_k[lev][rows, sl]).astype(BF16)
        qs, ks = [], []
        for start in range(rows.start, rows.stop, r):
            piece = slice(start, start + r)
            zero = jnp.zeros((r, HGRN_DIM), F32)
            if is_q_piece(start // r):
                qs.append(qh[piece] * fac[lev][piece, sl])
                ks.append(zero)
            else:
                qs.append(zero)
                ks.append(kh[piece] * fac[lev][piece, sl])
        cat = lambda xs: (xs[0] if len(xs) == 1 else jnp.concatenate(xs, axis=0)).astype(BF16)
        return cat(qs), cat(ks)

    heads = [slice(h * HGRN_DIM, (h + 1) * HGRN_DIM) for h in range(HGRN_HEADS)]
    p = {}
    for h, sl in enumerate(heads):
        for d, rows in enumerate(tiles):
            p[h, d] = jnp.where(diff == 0, _dot_nt(q[rows, sl].astype(BF16), kk[rows, sl].astype(BF16)), 0.0)
    for lev in range(2, N_LEVELS):
        block = 2 * (TM >> lev)
        for h, sl in enumerate(heads):
            for d, rows in enumerate(tiles):
                qt, kt = level_operands(lev, q[:, sl], kk[:, sl], sl, rows)
                sc = _dot_nt(qt, kt)
                p[h, d] = p[h, d] + (sc if block == half else jnp.where((diff >> (block.bit_length() - 1)) == 0, sc, 0.0))

    for h, sl in enumerate(heads):
        st = st_ref[h]
        qh, kh, vh = q[:, sl], kk[:, sl], v[:, sl]
        p_off = _dot_nt((qh[tiles[q_tile]] * fac[1][tiles[q_tile], sl]).astype(BF16),
                        (kh[tiles[k_tile]] * fac[1][tiles[k_tile], sl]).astype(BF16))
        inter = _dot_nt(qa[:, sl], st.astype(BF16))
        for d, rows in enumerate(tiles):
            o = inter[rows] + _dot(p[h, d].astype(BF16), vh[rows])
            if d == q_tile:
                o = o + _dot(p_off.astype(BF16), vh[tiles[k_tile]])
            o_ref[rows, sl] = o
        st_ref[h] = st * carry[:, sl] + _dot_tn(vh, kd[:, sl])


def _split_dot_left(m, x):
    hi = x.astype(BF16)
    lo = (x - hi.astype(F32)).astype(BF16)
    return _dot(m, hi) + _dot(m, lo)


def _hgrn(hg, hz, lb_fwd, lb_bwd, *, nb, tpb, nlt):
    n = hg.shape[0]
    nct = tpb - nlt

    def chunk(reverse):
        def index(b, s):
            if reverse:
                j = jnp.where(s < nct, tpb - 1 - s, nlt - 1 - (s - nct))
            else:
                j = jnp.where(s < nct, nlt + s, s - nct)
            return b * tpb + j
        return index

    def direction_specs(reverse, z_block, gmat):
        c = chunk(reverse)
        return [
            pl.BlockSpec((TM, HGRN_W), lambda b, s: (c(b, s), 0)),
            pl.BlockSpec((TM, HGRN_W), lambda b, s: (c(b, s), 1)),
            pl.BlockSpec((TM, HGRN_W), lambda b, s: (c(b, s), z_block)),
            pl.BlockSpec((8, HGRN_W), lambda b, s: (0, 0)),
            pl.BlockSpec(gmat.shape, lambda b, s: (0, 0)),
        ]

    g_fwd, g_bwd = _level_matrices(False), _level_matrices(True)
    cf, cb = chunk(False), chunk(True)
    state = pltpu.VMEM((HGRN_HEADS, HGRN_DIM, HGRN_DIM), F32)
    return pl.pallas_call(
        _hgrn_kernel,
        grid=(nb, tpb),
        in_specs=direction_specs(False, 0, g_fwd) + direction_specs(True, 1, g_bwd),
        out_specs=[
            pl.BlockSpec((TM, HGRN_W), lambda b, s: (cf(b, s), 0)),
            pl.BlockSpec((TM, HGRN_W), lambda b, s: (cb(b, s), 0)),
        ],
        out_shape=[jax.ShapeDtypeStruct((n, HGRN_W), F32)] * 2,
        scratch_shapes=[state, state],
        compiler_params=_cparams(("arbitrary", "arbitrary")),
        name="hgrn",
    )(hg, hg, hz, lb_fwd, g_fwd, hg, hg, hz, lb_bwd, g_bwd)


def _mla_heads(m, q_gain, kv_gain, wq_ref, wkv_ref, cos, sin, q_ref, k_ref, v_ref):
    cq = m[:, 0:Q_LORA]
    ckv = m[:, Q_LORA:Q_LORA + KV_LORA]
    kr = m[:, Q_LORA + KV_LORA:Q_LORA + KV_LORA + LANES]
    kr_sw = m[:, Q_LORA + KV_LORA + LANES:MLA_IN_W]
    qall = _dot((_rms(cq) * q_gain).astype(BF16), wq_ref[...]) * (MLA_SCALE * LOG2_E)
    kv = _dot((_rms(ckv) * kv_gain).astype(BF16), wkv_ref[...])
    k_rot = (kr * cos + kr_sw * sin).astype(BF16)
    ones = jnp.ones((TM, LANES), BF16)
    nw = MLA_HEADS * LANES
    for h in range(MLA_HEADS):
        a, b = h * LANES, (h + 1) * LANES
        s = h * HEAD_SLOT
        q_ref[:, s:s + LANES] = qall[:, a:b].astype(BF16)
        q_ref[:, s + LANES:s + HEAD_SLOT] = (qall[:, nw + a:nw + b] * cos + qall[:, 2 * nw + a:2 * nw + b] * sin).astype(BF16)
        k_ref[:, s:s + LANES] = kv[:, a:b].astype(BF16)
        k_ref[:, s + LANES:s + HEAD_SLOT] = k_rot
        v_ref[:, s:s + LANES] = kv[:, nw + a:nw + b].astype(BF16)
        v_ref[:, s + LANES:s + HEAD_SLOT] = ones


def _attn_kernel(q_ref, k_ref, v_ref, o_ref, s_ref, *, n_lat_steps, seq, ctx_len):
    j = pl.program_id(2)

    def run(rows, k0, k1):
        blocks = [(kb, slice(k0 + kb * KEY_BLOCK, k0 + (kb + 1) * KEY_BLOCK), slice(kb * KEY_BLOCK, (kb + 1) * KEY_BLOCK))
                  for kb in range((k1 - k0) // KEY_BLOCK)]
        hslot = lambda hh: slice(hh * HEAD_SLOT, (hh + 1) * HEAD_SLOT)

        def scores(hh, blk, m_run):
            _, ks, ss = blk
            s = _dot_nt(q_ref[0:rows, hslot(hh)], k_ref[ks, hslot(hh)])
            s_ref[hh % 2, 0:rows, ss] = s
            for c in range(KEY_BLOCK // LANES):
                m_run = jnp.maximum(m_run, s[:, c * LANES:(c + 1) * LANES])
            return m_run

        def values(hh, blk, m, o):
            _, ks, ss = blk
            d = (s_ref[hh % 2, 0:rows, ss] - m).astype(BF16)
            return o + _dot(jnp.exp2(d), v_ref[ks, hslot(hh)])

        def emit(hh, o):
            o_ref[0:rows, hh * MLA_V:(hh + 1) * MLA_V] = (o[:, 0:MLA_V] / o[:, MLA_V:]).astype(BF16)

        neg = jnp.full((rows, LANES), -jnp.inf, F32)
        zero = jnp.zeros((rows, HEAD_SLOT), F32)
        m_run = neg
        for blk in blocks:
            m_run = scores(0, blk, m_run)
        m_prev = jnp.max(m_run, axis=-1, keepdims=True)
        for hh in range(1, HEADS_PER_STEP):
            m_run, o = neg, zero
            for blk in blocks:
                m_run = scores(hh, blk, m_run)
                o = values(hh - 1, blk, m_prev, o)
            emit(hh - 1, o)
            m_prev = jnp.max(m_run, axis=-1, keepdims=True)
        o = zero
        for blk in blocks:
            o = values(HEADS_PER_STEP - 1, blk, m_prev, o)
        emit(HEADS_PER_STEP - 1, o)

    @pl.when(j < n_lat_steps)
    def _():
        run(TQ, 0, seq + ctx_len)

    @pl.when(j >= n_lat_steps)
    def _():
        run(ctx_len, seq, seq + ctx_len)


def _attention(q, k, v, *, nb, seq, ctx_len):
    n = q.shape[0]
    r = seq + ctx_len
    assert seq % TQ == 0 and ctx_len <= TQ
    n_lat_steps = seq // TQ
    hp = MLA_HEADS // HEADS_PER_STEP
    wq = HEADS_PER_STEP * HEAD_SLOT
    wo = HEADS_PER_STEP * MLA_V
    kern = functools.partial(_attn_kernel, n_lat_steps=n_lat_steps, seq=seq, ctx_len=ctx_len)
    out = pl.pallas_call(
        kern,
        grid=(nb, hp, n_lat_steps + 1),
        in_specs=[
            pl.BlockSpec((None, TQ, wq), lambda b, h, j: (b, j, h)),
            pl.BlockSpec((None, r, wq), lambda b, h, j: (b, 0, h)),
            pl.BlockSpec((None, r, wq), lambda b, h, j: (b, 0, h), pipeline_mode=pl.Buffered(1)),
        ],
        out_specs=pl.BlockSpec((None, TQ, wo), lambda b, h, j: (b, j, h)),
        out_shape=jax.ShapeDtypeStruct((nb, r, MLA_HEADS * MLA_V), BF16),
        scratch_shapes=[pltpu.VMEM((2, TQ, r), F32)],
        compiler_params=_cparams(("arbitrary", "arbitrary", "arbitrary")),
        name="attention",
    )(q.reshape(nb, r, -1), k.reshape(nb, r, -1), v.reshape(nb, r, -1))
    return out.reshape(n, MLA_HEADS * MLA_V)


def _pack_rows(x):
    w = x.shape[1] // 2
    lo = pltpu.bitcast(x[:, :w].astype(BF16).astype(F32), U32)
    hi = pltpu.bitcast(x[:, w:].astype(BF16).astype(F32), U32)
    return (lo >> 16) | (hi & jnp.uint32(0xFFFF0000))


def _store_row_tiles(ref, packed):
    rows = packed.shape[0]
    for s in range(ROW_TILE):
        ref[pl.ds(s, rows, stride=ROW_TILE), :] = packed[:, s * LANES:(s + 1) * LANES]


def _load_row_tiles(ref, rows):
    return jnp.concatenate([ref[pl.ds(s, rows, stride=ROW_TILE), :] for s in range(ROW_TILE)], axis=1)


def _unpack_rows(w):
    lo = pltpu.bitcast(w << 16, F32)
    hi = pltpu.bitcast(w & jnp.uint32(0xFFFF0000), F32)
    return lo, hi


def _post_kernel(yf_ref, of_ref, ob_ref, g_ref, att_ref, x_ref, mod_ref, hn_ref, wo_ref, lng_ref, lnb_ref,
                 rw_ref, rb_ref, x1_ref, h2_ref, eid_ref, gate_ref, base_ref, cnt_ref, *, tpb, nlt, nb, alpha):
    row = _mod_row(pl.program_id(0), tpb, nlt, nb)
    o = of_ref[...] + ob_ref[...]
    hn = hn_ref[...]
    gate_act = g_ref[...].astype(F32)
    gate_act = gate_act * _sigmoid(gate_act)
    mix = _dot(yf_ref[...], wo_ref[0:FOURIER_W, :])
    for h in range(HGRN_HEADS):
        sl = slice(h * HGRN_DIM, (h + 1) * HGRN_DIM)
        r = (_rms(o[:, sl]) * hn[:, sl] * gate_act[:, sl]).astype(BF16)
        mix = mix + _dot(r, wo_ref[FOURIER_W + h * HGRN_DIM:FOURIER_W + (h + 1) * HGRN_DIM, :])
    mix = mix + _dot(att_ref[...], wo_ref[FOURIER_W + HGRN_W:, :])

    x1 = _layer_norm(alpha * x_ref[...] + _mod(mod_ref, row, 2) * mix) * lng_ref[...] + lnb_ref[...]
    x1_ref[...] = x1
    h2 = _layer_norm(x1) * (1.0 + _mod(mod_ref, row, 4)) + _mod(mod_ref, row, 3)
    _store_row_tiles(h2_ref, _pack_rows(h2))

    hi = h2.astype(BF16)
    lo = (h2 - hi.astype(F32)).astype(BF16)
    rr = _dot(jnp.concatenate([hi, lo], axis=0), rw_ref[...])
    logits = rr[0:TM, 0:LANES] + rr[0:TM, LANES:] + rr[TM:, 0:LANES] + rr[TM:, LANES:] + rb_ref[...]
    lane = lax.broadcasted_iota(I32, logits.shape, 1)
    cur = logits
    vals, ids = [], []
    for _ in range(TOP_K):
        m = jnp.max(cur, axis=-1, keepdims=True)
        idx = jnp.min(jnp.where(cur == m, lane, LANES), axis=-1, keepdims=True)
        vals.append(m)
        ids.append(idx)
        cur = jnp.where(lane == idx, -3e38, cur)
    ex = [jnp.exp(vk - vals[0]) for vk in vals]
    den = ex[0] + ex[1] + ex[2] + ex[3]
    eid = jnp.zeros(logits.shape, I32)
    gate = jnp.zeros(logits.shape, F32)
    for k in range(TOP_K):
        eid = jnp.where(lane == k, ids[k], eid)
        gate = jnp.where(lane == k, ex[k] / den, gate)
    eid_ref[...] = eid
    gate_ref[...] = gate

    i = pl.program_id(0)

    @pl.when(i == 0)
    def _():
        base_ref[...] = jnp.zeros_like(base_ref)
        cnt_ref[...] = jnp.zeros_like(cnt_ref)

    base_ref[pl.ds(i, 1), :] = cnt_ref[0:1, :]
    tile_cnt = jnp.zeros((1, LANES), F32)
    for k in range(TOP_K):
        tile_cnt = tile_cnt + jnp.sum((lane == ids[k]).astype(F32), axis=0, keepdims=True)
    cnt_ref[0:1, :] = cnt_ref[0:1, :] + tile_cnt


def _post(yf, o_f, o_b, hg, att, x, mod_all, hnorm, w_out_all, ln_g, ln_b, rw_all, rb, layer, *, tpb, nlt, nb, alpha):
    n = x.shape[0]
    nt_pad = -(-(n // TM) // SUBLANES) * SUBLANES
    kern = functools.partial(_post_kernel, tpb=tpb, nlt=nlt, nb=nb, alpha=alpha)
    full = lambda a: pl.BlockSpec(a.shape, lambda i: (0, 0))
    return pl.pallas_call(
        kern,
        grid=(n // TM,),
        in_specs=[
            pl.BlockSpec((TM, FOURIER_W), lambda i: (i, 0)),
            pl.BlockSpec((TM, HGRN_W), lambda i: (i, 0)),
            pl.BlockSpec((TM, HGRN_W), lambda i: (i, 0)),
            pl.BlockSpec((TM, HGRN_W), lambda i: (i, 2)),
            pl.BlockSpec((TM, MLA_HEADS * MLA_V), lambda i: (i, 0)),
            pl.BlockSpec((TM, D_MODEL), lambda i: (i, 0)),
            _layer_block(mod_all, layer), full(hnorm),
            _layer_block(w_out_all, layer, pipeline_mode=pl.Buffered(1)),
            full(ln_g), full(ln_b), _layer_block(rw_all, layer), full(rb),
        ],
        out_specs=[
            pl.BlockSpec((TM, D_MODEL), lambda i: (i, 0)),
            pl.BlockSpec((TM * ROW_TILE, LANES), lambda i: (i, 0)),
            pl.BlockSpec((TM, LANES), lambda i: (i, 0)),
            pl.BlockSpec((TM, LANES), lambda i: (i, 0)),
            pl.BlockSpec((nt_pad, LANES), lambda i: (0, 0)),
            pl.BlockSpec((SUBLANES, LANES), lambda i: (0, 0)),
        ],
        out_shape=[
            jax.ShapeDtypeStruct((n, D_MODEL), F32),
            jax.ShapeDtypeStruct((n * ROW_TILE, LANES), U32),
            jax.ShapeDtypeStruct((n, LANES), I32),
            jax.ShapeDtypeStruct((n, LANES), F32),
            jax.ShapeDtypeStruct((nt_pad, LANES), F32),
            jax.ShapeDtypeStruct((SUBLANES, LANES), F32),
        ],
        compiler_params=_cparams(("arbitrary",)),
        name="post",
    )(yf, o_f, o_b, hg, att, x, mod_all, hnorm, w_out_all, ln_g, ln_b, rw_all, rb)


def _route_kernel(eid_ref, base_ref, cnt_ref, pos_ref, te_ref, meta_ref, *, n_tiles_pad):
    i = pl.program_id(0)
    lane = lax.broadcasted_iota(I32, (TM, LANES), 1)
    eid = eid_ref[...]
    onehot = [(lane == eid[:, k:k + 1]).astype(F32) for k in range(TOP_K)]
    colsum = [jnp.sum(oh, axis=0, keepdims=True) for oh in onehot]

    cnt = cnt_ref[0:1, :]
    tiles = jnp.floor((cnt + (TMM - 1)) * (1.0 / TMM))
    r128 = lax.broadcasted_iota(I32, (LANES, LANES), 0)
    c128 = lax.broadcasted_iota(I32, (LANES, LANES), 1)
    incl = (r128 <= c128).astype(BF16)
    pad_end = _split_dot(jnp.broadcast_to(tiles, (8, LANES)), incl)[0:1, :] * TMM
    pad_start = pad_end - tiles * TMM

    rr = lax.broadcasted_iota(I32, (TM, TM), 0)
    cc = lax.broadcasted_iota(I32, (TM, TM), 1)
    strict = (cc < rr).astype(BF16)
    run = pad_start + base_ref[pl.ds(i, 1), :]
    pos = jnp.zeros((TM, LANES), I32)
    for k in range(TOP_K):
        before = _dot(strict, onehot[k].astype(BF16))
        val = jnp.sum(onehot[k] * (before + run), axis=1, keepdims=True)
        pos = jnp.where(lane == k, val.astype(I32), pos)
        run = run + colsum[k]
    pos_ref[...] = pos

    @pl.when(i == 0)
    def _():
        tstart = (lax.broadcasted_iota(I32, (n_tiles_pad, LANES), 0) * TMM).astype(F32)
        lane_t = lax.broadcasted_iota(I32, (n_tiles_pad, LANES), 1)
        done = jnp.where((pad_end <= tstart) & (lane_t < N_EXPERTS), 1.0, 0.0)
        te = jnp.minimum(jnp.sum(done, axis=1, keepdims=True), N_EXPERTS - 1.0)
        te_ref[...] = jnp.broadcast_to(te, (n_tiles_pad, LANES)).astype(I32)
        sub = lax.broadcasted_iota(I32, (8, LANES), 0)
        meta = jnp.where(sub == 0, pad_start + cnt, jnp.where(sub == 1, pad_end - pad_start - cnt, pad_end))
        meta_ref[...] = meta.astype(I32)


def _route(eid, base, cnt, n_tiles):
    n = eid.shape[0]
    n_tiles_pad = -(-n_tiles // 8) * 8
    kern = functools.partial(_route_kernel, n_tiles_pad=n_tiles_pad)
    full = lambda a: pl.BlockSpec(a.shape, lambda i: (0, 0))
    pos, te, meta = pl.pallas_call(
        kern,
        grid=(n // TM,),
        in_specs=[pl.BlockSpec((TM, LANES), lambda i: (i, 0)), full(base), full(cnt)],
        out_specs=[
            pl.BlockSpec((TM, LANES), lambda i: (i, 0)),
            pl.BlockSpec((n_tiles_pad, LANES), lambda i: (0, 0)),
            pl.BlockSpec((8, LANES), lambda i: (0, 0)),
        ],
        out_shape=[
            jax.ShapeDtypeStruct((n, LANES), I32),
            jax.ShapeDtypeStruct((n_tiles_pad, LANES), I32),
            jax.ShapeDtypeStruct((8, LANES), I32),
        ],
        compiler_params=_cparams(("arbitrary",)),
        name="moe_route",
    )(eid, base, cnt)
    pos_flat = pos[:, :TOP_K].reshape(-1)
    tile_e = te[:n_tiles, 0]
    fill_start = meta[0, :N_EXPERTS]
    fill_n = meta[1, :N_EXPERTS]
    n_used = meta[2, N_EXPERTS - 1:N_EXPERTS] // TMM
    return pos_flat, tile_e, n_used, fill_start, fill_n


def _dispatch_kernel(pos_ref, fs_ref, fn_ref, nu_ref, h_ref, o_hbm, zero_ref, sem, fill_sem, *, n_tiles):
    i = pl.program_id(0)

    def rows(ref, first, count):
        return ref.at[pl.ds(pl.multiple_of(first * ROW_TILE, ROW_TILE), count * ROW_TILE), :]

    def start(r, c):
        for k in range(TOP_K):
            p = pos_ref[(i * TM + r) * TOP_K + k]
            pltpu.make_async_copy(rows(h_ref, r, 1), rows(o_hbm, p, 1), sem).start()
        return c

    lax.fori_loop(0, TM, start, 0, unroll=ISSUE_UNROLL)

    @pl.when(i == 0)
    def _():
        zero_ref[...] = jnp.zeros_like(zero_ref)

        def fill(first, count):
            return pltpu.make_async_copy(rows(zero_ref, 0, count), rows(o_hbm, first, count), fill_sem)

        def each(fn):
            for e in range(N_EXPERTS):
                first, count = fs_ref[e], fn_ref[e]
                piece = TMM // 2
                while piece:
                    @pl.when((count & piece) != 0)
                    def _(first=first, piece=piece):
                        fn(fill(first, piece))

                    first = first + (count & piece)
                    piece //= 2
            lax.fori_loop(nu_ref[0], n_tiles, lambda t, c: (fn(fill(t * TMM, TMM)), c)[1], 0)

        each(lambda cp: cp.start())
        each(lambda cp: cp.wait())

    for _ in range(TOP_K):
        pltpu.make_async_copy(h_ref, rows(o_hbm, 0, TM), sem).wait()


def _dispatch(pos_flat, fill_start, fill_n, n_used, h2p, n_tiles):
    n = h2p.shape[0] // ROW_TILE
    return pl.pallas_call(
        functools.partial(_dispatch_kernel, n_tiles=n_tiles),
        grid_spec=pltpu.PrefetchScalarGridSpec(
            num_scalar_prefetch=4,
            grid=(n // TM,),
            in_specs=[pl.BlockSpec((TM * ROW_TILE, LANES), lambda i, p, fs, fn, nu: (i, 0))],
            out_specs=pl.BlockSpec(memory_space=pl.ANY),
            scratch_shapes=[pltpu.VMEM((TMM * ROW_TILE, LANES), U32), pltpu.SemaphoreType.DMA(()),
                            pltpu.SemaphoreType.DMA(())],
        ),
        out_shape=jax.ShapeDtypeStruct((n_tiles * TMM * ROW_TILE, LANES), U32),
        compiler_params=_cparams(("arbitrary",)),
        name="moe_dispatch",
    )(pos_flat, fill_start, fill_n, n_used, h2p)


def _w1_kernel(w_ref, g_ref, l_ref, t_ref):
    for c in range(D_MODEL // LANES):
        cs = slice(c * LANES, (c + 1) * LANES)
        t_ref[c] = w_ref[cs, :].T
        g_ref[:, cs] = t_ref[c, pl.ds(0, W1_TN // 2, stride=2), :].astype(BF16)
        l_ref[:, cs] = t_ref[c, pl.ds(1, W1_TN // 2, stride=2), :].astype(BF16)


def _prep_w1(w1):
    depth, ne, d, n2 = w1.shape
    out = jax.ShapeDtypeStruct((depth * ne, n2 // 2, d), BF16)
    g, l = pl.pallas_call(
        _w1_kernel,
        grid=(depth * ne, n2 // W1_TN),
        in_specs=[pl.BlockSpec((None, d, W1_TN), lambda e, j: (e, 0, j))],
        out_specs=[pl.BlockSpec((None, W1_TN // 2, d), lambda e, j: (e, j, 0))] * 2,
        out_shape=[out, out],
        scratch_shapes=[pltpu.VMEM((d // LANES, W1_TN, LANES), F32)],
        compiler_params=_cparams(("arbitrary", "arbitrary")),
        name="moe_w1_prep",
    )(w1.reshape(depth * ne, d, n2))
    return g, l


def _moe_kernel(te_ref, nu_ref, x_ref, w1g_ref, w1l_ref, b1g_ref, b1l_ref, w2_ref, b2_ref, y_ref, w2b_ref):
    i = pl.program_id(0)

    @pl.when(i < nu_ref[0])
    def _():
        @pl.when((i == 0) | (te_ref[i] != te_ref[jnp.maximum(i - 1, 0)]))
        def _():
            w2b_ref[...] = w2_ref[...].astype(BF16)

        lo, hi = _unpack_rows(_load_row_tiles(x_ref, TMM))
        lo = lo.astype(BF16)
        hi = hi.astype(BF16)
        ug = _dot_nt(lo, w1g_ref[:, 0:HALF]) + _dot_nt(hi, w1g_ref[:, HALF:]) + b1g_ref[...]
        ul = _dot_nt(lo, w1l_ref[:, 0:HALF]) + _dot_nt(hi, w1l_ref[:, HALF:]) + b1l_ref[...]
        xg = jnp.minimum(ug, SWIGLU_LIMIT)
        xl = jnp.clip(ul, -SWIGLU_LIMIT, SWIGLU_LIMIT)
        act = xg * _sigmoid(SWIGLU_ALPHA * xg) * (xl + 1.0)
        y = _dot(act.astype(BF16), w2b_ref[...]) + b2_ref[...]
        _store_row_tiles(y_ref, _pack_rows(y))

    @pl.when(i >= nu_ref[0])
    def _():
        y_ref[...] = jnp.zeros_like(y_ref)


def _moe_ffn(tile_e, n_used, xs, w1g, w1l, b1g, b1l, w2, b2, layer):
    m = xs.shape[0] // ROW_TILE
    emap = lambda i, te, nu: (layer * N_EXPERTS + te[i], 0, 0)
    row_tiles = pl.BlockSpec((TMM * ROW_TILE, LANES), lambda i, te, nu: (i, 0))
    return pl.pallas_call(
        _moe_kernel,
        grid_spec=pltpu.PrefetchScalarGridSpec(
            num_scalar_prefetch=2,
            grid=(m // TMM,),
            in_specs=[
                row_tiles,
                pl.BlockSpec((None, EXPERT_FF, D_MODEL), emap),
                pl.BlockSpec((None, EXPERT_FF, D_MODEL), emap),
                pl.BlockSpec((None, 1, EXPERT_FF), emap),
                pl.BlockSpec((None, 1, EXPERT_FF), emap),
                pl.BlockSpec((None, EXPERT_FF, D_MODEL), emap),
                pl.BlockSpec((None, 1, D_MODEL), emap),
            ],
            out_specs=row_tiles,
            scratch_shapes=[pltpu.VMEM((EXPERT_FF, D_MODEL), BF16)],
        ),
        out_shape=jax.ShapeDtypeStruct((m * ROW_TILE, LANES), U32),
        compiler_params=_cparams(("arbitrary",)),
        name="moe_ffn",
    )(tile_e, n_used, xs, w1g, w1l, b1g, b1l, w2, b2)


def _combine_kernel(pos_ref, y_hbm, gate_ref, x_ref, mod_ref, lng_ref, lnb_ref, o_ref, buf, sem, *, tpb, nlt, nb, alpha,
                    latent_only):
    i = pl.program_id(0)
    n_steps = pl.num_programs(0)
    row = _mod_row(i, tpb, nlt, nb)

    def issue(tile, slot):
        def start(r, c):
            for k in range(TOP_K):
                p = pos_ref[(tile * TM + r) * TOP_K + k]
                src = y_hbm.at[pl.ds(pl.multiple_of(p * ROW_TILE, ROW_TILE), ROW_TILE), :]
                dst = buf.at[slot, k, pl.ds(pl.multiple_of(r * ROW_TILE, ROW_TILE), ROW_TILE), :]
                pltpu.make_async_copy(src, dst, sem.at[slot]).start()
            return c

        lax.fori_loop(0, TM, start, 0, unroll=ISSUE_UNROLL)

    @pl.when(i == 0)
    def _():
        issue(0, 0)

    @pl.when(i + 1 < n_steps)
    def _():
        issue(i + 1, (i + 1) % 2)

    slot = i % 2
    for k in range(TOP_K):
        pltpu.make_async_copy(y_hbm.at[pl.ds(0, TM * ROW_TILE), :], buf.at[slot, k], sem.at[slot]).wait()

    gates = gate_ref[...]
    acc_lo = jnp.zeros((TM, HALF), F32)
    acc_hi = jnp.zeros((TM, HALF), F32)
    for k in range(TOP_K):
        lo, hi = _unpack_rows(_load_row_tiles(buf.at[slot, k], TM))
        gk = gates[:, k:k + 1]
        acc_lo = acc_lo + gk * lo
        acc_hi = acc_hi + gk * hi
    ffn = jnp.concatenate([acc_lo, acc_hi], axis=-1)
    x2 = _layer_norm(alpha * x_ref[...] + _mod(mod_ref, row, 5) * ffn) * lng_ref[...] + lnb_ref[...]
    if latent_only:
        @pl.when(i % tpb < nlt)
        def _():
            o_ref[...] = x2
    else:
        o_ref[...] = x2


def _combine(pos, ys, gates, x, mod_all, ln_g, ln_b, layer, *, latent_only, tpb, nlt, nb, alpha):
    n = x.shape[0]
    kern = functools.partial(_combine_kernel, tpb=tpb, nlt=nlt, nb=nb, alpha=alpha, latent_only=latent_only)
    full = lambda a: pl.BlockSpec(a.shape, lambda i, p: (0, 0))
    if latent_only:
        omap = lambda i, p: ((i // tpb) * nlt + jnp.minimum(i % tpb, nlt - 1), 0)
        n_out = nb * nlt * TM
    else:
        omap = lambda i, p: (i, 0)
        n_out = n
    return pl.pallas_call(
        kern,
        grid_spec=pltpu.PrefetchScalarGridSpec(
            num_scalar_prefetch=1,
            grid=(n // TM,),
            in_specs=[
                pl.BlockSpec(memory_space=pl.ANY),
                pl.BlockSpec((TM, LANES), lambda i, p: (i, 0)),
                pl.BlockSpec((TM, D_MODEL), lambda i, p: (i, 0)),
                _layer_block(mod_all, layer), full(ln_g), full(ln_b),
            ],
            out_specs=pl.BlockSpec((TM, D_MODEL), omap),
            scratch_shapes=[pltpu.VMEM((2, TOP_K, TM * ROW_TILE, LANES), U32), pltpu.SemaphoreType.DMA((2,))],
        ),
        out_shape=jax.ShapeDtypeStruct((n_out, D_MODEL), F32),
        compiler_params=_cparams(("arbitrary",)),
        name="moe_combine",
    )(pos, ys, gates, x, mod_all, ln_g, ln_b)


def _dft_pair(t_len):
    assert t_len % GRID_W == 0
    k = jnp.arange(t_len, dtype=I32)

    def table(t):
        ang = ((t[:, None] * k[None, :]) % t_len).astype(F32) * (2.0 * np.pi / t_len)
        return jnp.cos(ang), jnp.sin(ang)

    ca, sa = table(jnp.arange(t_len // GRID_W, dtype=I32) * GRID_W)
    cb, sb = table(jnp.arange(GRID_W, dtype=I32))
    s = 1.0 / np.sqrt(t_len)
    cos = ca[:, None, :] * cb[None, :, :] - sa[:, None, :] * sb[None, :, :]
    sin = sa[:, None, :] * cb[None, :, :] + ca[:, None, :] * sb[None, :, :]
    return (cos * s).reshape(t_len, t_len).astype(BF16), (sin * s).reshape(t_len, t_len).astype(BF16)


def _channel_dft():
    c = np.arange(FOURIER_GROUP)
    ang = 2.0 * np.pi * ((c[:, None] * c[None, :]) % FOURIER_GROUP) / FOURIER_GROUP
    s = 1.0 / np.sqrt(FOURIER_GROUP)
    groups = FOURIER_W // FOURIER_GROUP
    m = np.zeros((FOURIER_W, 2 * FOURIER_W), np.float32)
    for g in range(groups):
        a, b = g * FOURIER_GROUP, (g + 1) * FOURIER_GROUP
        m[a:b, a:b] = np.cos(ang) * s
        m[a:b, FOURIER_W + a:FOURIER_W + b] = np.sin(ang) * s
    return jnp.asarray(m, dtype=BF16)


def _pair_swap(w):
    pairs = w.reshape(*w.shape[:-1], w.shape[-1] // 2, 2)
    return jnp.stack([-pairs[..., 1], pairs[..., 0]], axis=-1).reshape(w.shape)


def _pad_lanes(w, width):
    return jnp.pad(w, [(0, 0)] * (w.ndim - 1) + [(0, width - w.shape[-1])])


def _rope_tables(nb, seq, ctx_len):
    t = jnp.arange(seq)
    n_freq = MLA_ROPE // 4
    inv = ROPE_BASE ** (-jnp.arange(n_freq, dtype=F32) / n_freq)
    ang = jnp.concatenate([(t // GRID_W)[:, None] * inv, (t % GRID_W)[:, None] * inv], axis=-1)
    ang = jnp.repeat(ang, 2, axis=-1)
    cos = jnp.concatenate([jnp.cos(ang), jnp.ones((ctx_len, MLA_ROPE), F32)], axis=0)
    sin = jnp.concatenate([jnp.sin(ang), jnp.zeros((ctx_len, MLA_ROPE), F32)], axis=0)
    cos = jnp.tile(_pad_lanes(cos, LANES), (nb, 1))
    sin = jnp.tile(_pad_lanes(sin, LANES), (nb, 1))
    return cos, sin


def _prep_in_proj(w_in):
    f, q, i, g, zf, zb, cq, ckv, kr = jnp.split(w_in, np.cumsum([512, 512, 512, 512, 512, 512, 512, 256]).tolist(), axis=-1)
    cols = [f, q, i, g, zf, zb, cq, ckv, _pad_lanes(kr, LANES), _pad_lanes(_pair_swap(kr), LANES)]
    return jnp.concatenate(cols, axis=-1).astype(BF16)


def _prep_wq(w_uq):
    w = w_uq.reshape(Q_LORA, MLA_HEADS, MLA_NOPE + MLA_ROPE)
    nope = w[:, :, :MLA_NOPE].reshape(Q_LORA, -1)
    rope = w[:, :, MLA_NOPE:]
    rope_p = _pad_lanes(rope, LANES).reshape(Q_LORA, -1)
    rope_s = _pad_lanes(_pair_swap(rope), LANES).reshape(Q_LORA, -1)
    return jnp.concatenate([nope, rope_p, rope_s], axis=-1).astype(BF16)


def _prep_wkv(w_ukv):
    w = w_ukv.reshape(KV_LORA, MLA_HEADS, MLA_NOPE + MLA_V)
    return jnp.concatenate([w[:, :, :MLA_NOPE].reshape(KV_LORA, -1), w[:, :, MLA_NOPE:].reshape(KV_LORA, -1)], axis=-1).astype(BF16)


def kernel(x, c, ctx, c_ctx, w_ada, b_ada, w_in, mla_q_norm, w_uq, mla_kv_norm, w_ukv, hgrn_lb_logits, hgrn_norm,
           w_out, ln1_g, ln1_b, router_w, router_b, w1, b1, w2, b2, ln2_g, ln2_b):
    nb, seq, d = x.shape
    ctx_len = ctx.shape[1]
    depth = w_ada.shape[0]
    assert d == D_MODEL and seq % TM == 0 and ctx_len % TM == 0 and seq % GRID_W == 0
    r = seq + ctx_len
    tpb, nlt = r // TM, seq // TM
    n = nb * r
    n_tiles = -(-n * TOP_K // TMM) + N_EXPERTS
    alpha = float((2 * depth) ** 0.25)
    geo = dict(tpb=tpb, nlt=nlt, nb=nb)

    xs = jnp.concatenate([x, ctx], axis=1).reshape(n, d)

    mod_rows = -(-(nb + 1) // 8) * 8
    cc = jnp.zeros((mod_rows, d), F32).at[:nb].set(c).at[nb].set(c_ctx)
    mod_all = _ada(cc, w_ada, b_ada)

    lb = jnp.cumsum(jax.nn.softmax(hgrn_lb_logits.astype(F32), axis=0), axis=0)
    lb = lb - lb[:1]
    lb_rows = jnp.stack([jnp.log(lb), jnp.log1p(-lb), 1.0 - lb] + [jnp.zeros_like(lb)] * 5, axis=2)

    dft_c = _channel_dft()
    dft_lat = _dft_pair(seq)
    dft_ctx = _dft_pair(ctx_len)
    cos_t, sin_t = _rope_tables(nb, seq, ctx_len)

    w_in_all = _prep_in_proj(w_in)
    wq_all = jax.vmap(_prep_wq)(w_uq)
    wkv_all = jax.vmap(_prep_wkv)(w_ukv)
    w_out_all = w_out.astype(BF16)
    w1g_all, w1l_all = _prep_w1(w1)
    le = depth * N_EXPERTS
    w2_all = w2.reshape(le, EXPERT_FF, d)
    b1g_all = b1[:, :, 0::2].reshape(le, 1, EXPERT_FF)
    b1l_all = b1[:, :, 1::2].reshape(le, 1, EXPERT_FF)
    b2_all = b2.reshape(le, 1, d)
    rw_f32 = _pad_lanes(router_w, LANES)
    rw_hi = rw_f32.astype(BF16)
    rw_all = jnp.concatenate([rw_hi, (rw_f32 - rw_hi.astype(F32)).astype(BF16)], axis=-1)

    for l in range(depth):
        row = lambda a: a[l].reshape(1, -1)
        z, hg, hz, q, k, v = _pre(xs, mod_all, w_in_all, dft_c, row(mla_q_norm), row(mla_kv_norm), wq_all, wkv_all,
                              cos_t, sin_t, l, **geo)
        yf = _fourier(z, dft_lat, dft_ctx, nb=nb, tpb=tpb, nlt=nlt, seq=seq, ctx_len=ctx_len)
        o_f, o_b = _hgrn(hg, hz, lb_rows[l, 0], lb_rows[l, 1], **geo)
        att = _attention(q, k, v, nb=nb, seq=seq, ctx_len=ctx_len)

        rb = jnp.full((1, LANES), NEG_BIG, F32).at[0, :N_EXPERTS].set(router_b[l])
        x1, h2p, eid, gates, base, cnt = _post(yf, o_f, o_b, hg, att, xs, mod_all, row(hgrn_norm), w_out_all, row(ln1_g), row(ln1_b),
                                    rw_all, rb, l, alpha=alpha, **geo)

        pos, tile_e, n_used, fill_start, fill_n = _route(eid, base, cnt, n_tiles)
        xg = _dispatch(pos, fill_start, fill_n, n_used, h2p, n_tiles)
        ys = _moe_ffn(tile_e, n_used, xg, w1g_all, w1l_all, b1g_all, b1l_all, w2_all, b2_all, l)
        xs = _combine(pos, ys, gates, x1, mod_all, row(ln2_g), row(ln2_b), l, latent_only=l == depth - 1, alpha=alpha, **geo)

    return xs.reshape(nb, seq, d)
```

```python
import functools

import numpy as np
import jax
import jax.numpy as jnp
from jax import lax
from jax.experimental import pallas as pl
from jax.experimental.pallas import tpu as pltpu

F32 = jnp.float32
BF16 = jnp.bfloat16
U32 = jnp.uint32
I32 = jnp.int32

D_MODEL = 2048
FOURIER_W = 512
FOURIER_GROUP = 128
HGRN_W = 512
HGRN_HEADS = 4
HGRN_DIM = 128
MLA_HEADS = 8
MLA_NOPE = 128
MLA_ROPE = 64
MLA_V = 128
Q_LORA = 512
KV_LORA = 256
MLA_SCALE = (MLA_NOPE + MLA_ROPE) ** -0.5
N_EXPERTS = 32
TOP_K = 4
EXPERT_FF = 768
SWIGLU_ALPHA = 1.702
SWIGLU_LIMIT = 7.0
N_MOD = 6
GRID_W = 64
ROPE_BASE = 10000.0
EPS = 1e-6
LOG2_E = 1.4426950408889634

LANES = 128
SUBLANES = 8
TM = 256
TMM = 512
TQ = 2 * TM
HEAD_SLOT = 2 * LANES
HEADS_PER_STEP = 4
KEY_BLOCK = 256
IN_W = 4096
HG_OFF = FOURIER_W
MLA_OFF = HG_OFF + 5 * HGRN_W
MLA_IN_W = IN_W - MLA_OFF
N_LEVELS = 9
ADA_TN = 1024
W1_TN = 768
HALF = D_MODEL // 2
ROW_TILE = HALF // LANES
assert ROW_TILE == SUBLANES
NEG_BIG = -1e30
ISSUE_UNROLL = 8
VMEM_LIMIT = 56 * 1024 * 1024


def _cparams(sem, vmem=VMEM_LIMIT):
    return pltpu.CompilerParams(dimension_semantics=sem, vmem_limit_bytes=vmem)


def _dot(a, b):
    return jnp.dot(a, b, preferred_element_type=F32)


def _dot_nt(a, b):
    return lax.dot_general(a, b, (((1,), (1,)), ((), ())), preferred_element_type=F32)


def _dot_tn(a, b):
    return lax.dot_general(a, b, (((0,), (0,)), ((), ())), preferred_element_type=F32)


def _split_dot(x, m):
    hi = x.astype(BF16)
    lo = (x - hi.astype(F32)).astype(BF16)
    return _dot(hi, m) + _dot(lo, m)


def _layer_norm(x):
    mu = jnp.mean(x, axis=-1, keepdims=True)
    xc = x - mu
    var = jnp.mean(xc * xc, axis=-1, keepdims=True)
    return xc * lax.rsqrt(var + EPS)


def _rms(x):
    return x * lax.rsqrt(jnp.mean(x * x, axis=-1, keepdims=True) + EPS)


def _sigmoid(x):
    return 1.0 / (1.0 + jnp.exp(-x))


def _mod_row(i, tpb, nlt, nb):
    return jnp.where(i % tpb >= nlt, nb, i // tpb)


def _mod(mod_ref, row, k):
    return mod_ref[pl.ds(row, 1), k * D_MODEL:(k + 1) * D_MODEL]


def _ada_kernel(c_ref, w_ref, b_ref, o_ref):
    c = c_ref[...]
    s = c * _sigmoid(c)
    hi = s.astype(BF16)
    lo = (s - hi.astype(F32)).astype(BF16)
    w = w_ref[...].astype(BF16)
    o_ref[...] = _dot(hi, w) + _dot(lo, w) + b_ref[...]


def _ada(cc, w_ada, b_ada):
    depth, d, n = w_ada.shape
    rows = cc.shape[0]
    return pl.pallas_call(
        _ada_kernel,
        grid=(depth, n // ADA_TN),
        in_specs=[
            pl.BlockSpec((rows, d), lambda l, j: (0, 0)),
            pl.BlockSpec((None, d, ADA_TN), lambda l, j: (l, 0, j)),
            pl.BlockSpec((None, 1, ADA_TN), lambda l, j: (l, 0, j)),
        ],
        out_specs=pl.BlockSpec((None, rows, ADA_TN), lambda l, j: (l, 0, j)),
        out_shape=jax.ShapeDtypeStruct((depth, rows, n), F32),
        compiler_params=_cparams(("arbitrary", "arbitrary")),
        name="ada",
    )(cc, w_ada, b_ada.reshape(depth, 1, n))


def _pre_kernel(x_ref, mod_ref, w_ref, dft_ref, qg_ref, kvg_ref, wq_ref, wkv_ref, cos_ref, sin_ref,
                z_ref, hg_ref, hz_ref, q_ref, k_ref, v_ref, *, tpb, nlt, nb):
    row = _mod_row(pl.program_id(0), tpb, nlt, nb)
    h = _layer_norm(x_ref[...]) * (1.0 + _mod(mod_ref, row, 1)) + _mod(mod_ref, row, 0)
    r = _dot(h.astype(BF16), w_ref[...])
    z_ref[...] = _dot(r[:, 0:FOURIER_W].astype(BF16), dft_ref[...]).astype(BF16)
    hg_ref[...] = r[:, HG_OFF:HG_OFF + 3 * HGRN_W].astype(BF16)
    hz_ref[...] = r[:, HG_OFF + 3 * HGRN_W:MLA_OFF]
    _mla_heads(r[:, MLA_OFF:IN_W], qg_ref[...], kvg_ref[...], wq_ref, wkv_ref, cos_ref[...], sin_ref[...],
               q_ref, k_ref, v_ref)


def _layer_block(a, layer, **kw):
    return pl.BlockSpec((None,) + a.shape[1:], lambda *_: (layer, 0, 0), **kw)


def _pre(x, mod_all, w_in_all, dft_c, q_gain, kv_gain, wq_all, wkv_all, cos_t, sin_t, layer, *, tpb, nlt, nb):
    n = x.shape[0]
    w = MLA_HEADS * HEAD_SLOT
    kern = functools.partial(_pre_kernel, tpb=tpb, nlt=nlt, nb=nb)
    full = lambda a: pl.BlockSpec(a.shape, lambda i: (0, 0))
    rows = lambda width: pl.BlockSpec((TM, width), lambda i: (i, 0))
    once = dict(pipeline_mode=pl.Buffered(1))
    return pl.pallas_call(
        kern,
        grid=(n // TM,),
        in_specs=[
            rows(D_MODEL),
            _layer_block(mod_all, layer),
            _layer_block(w_in_all, layer, **once),
            full(dft_c), full(q_gain), full(kv_gain),
            _layer_block(wq_all, layer, **once), _layer_block(wkv_all, layer, **once),
            rows(LANES), rows(LANES),
        ],
        out_specs=[rows(2 * FOURIER_W), rows(3 * HGRN_W), rows(2 * HGRN_W), rows(w), rows(w), rows(w)],
        out_shape=[
            jax.ShapeDtypeStruct((n, 2 * FOURIER_W), BF16),
            jax.ShapeDtypeStruct((n, 3 * HGRN_W), BF16),
            jax.ShapeDtypeStruct((n, 2 * HGRN_W), F32),
            jax.ShapeDtypeStruct((n, w), BF16),
            jax.ShapeDtypeStruct((n, w), BF16),
            jax.ShapeDtypeStruct((n, w), BF16),
        ],
        compiler_params=_cparams(("arbitrary",)),
        name="pre",
    )(x, mod_all, w_in_all, dft_c, q_gain, kv_gain, wq_all, wkv_all, cos_t, sin_t)


def _four_kernel(z_ref, cl_ref, sl_ref, cc_ref, sc_ref, y_ref, *, nlt, seq):
    j = pl.program_id(1)

    def emit(cm, sm, z):
        y = _dot(cm, z[:, 0:FOURIER_W]) - _dot(sm, z[:, FOURIER_W:])
        y_ref[...] = y.astype(BF16)

    @pl.when(j < nlt)
    def _():
        emit(cl_ref[...], sl_ref[...], z_ref[0:seq, :])

    @pl.when(j >= nlt)
    def _():
        emit(cc_ref[...], sc_ref[...], z_ref[seq:, :])


def _fourier(z, dft_lat, dft_ctx, *, nb, tpb, nlt, seq, ctx_len):
    n = z.shape[0]
    r = seq + ctx_len
    cl, sl = dft_lat
    cc, sc = dft_ctx
    kern = functools.partial(_four_kernel, nlt=nlt, seq=seq)
    lmap = lambda b, j: (jnp.minimum(j, nlt - 1), 0)
    cmap = lambda b, j: (jnp.maximum(j - nlt, 0), 0)
    return pl.pallas_call(
        kern,
        grid=(nb, tpb),
        in_specs=[
            pl.BlockSpec((None, r, 2 * FOURIER_W), lambda b, j: (b, 0, 0)),
            pl.BlockSpec((TM, seq), lmap),
            pl.BlockSpec((TM, seq), lmap),
            pl.BlockSpec((TM, ctx_len), cmap),
            pl.BlockSpec((TM, ctx_len), cmap),
        ],
        out_specs=pl.BlockSpec((TM, FOURIER_W), lambda b, j: (b * tpb + j, 0)),
        out_shape=jax.ShapeDtypeStruct((n, FOURIER_W), BF16),
        compiler_params=_cparams(("arbitrary", "arbitrary")),
        name="fourier",
    )(z.reshape(nb, r, 2 * FOURIER_W), cl, sl, cc, sc)


def _level_matrices(reverse):
    p = np.arange(TM)
    mats = [(p[None, :] <= p[:, None])]
    for lev in range(1, N_LEVELS):
        r = TM >> lev
        same = (p[None, :] // r) == (p[:, None] // r)
        is_q = ((p // r) % 2 == 1)[:, None]
        mats.append(same & np.where(is_q, p[None, :] <= p[:, None], p[None, :] > p[:, None]))
    g = np.stack(mats).astype(np.float32)
    if reverse:
        g = g[:, ::-1, ::-1]
    return jnp.asarray(g.reshape(N_LEVELS * TM, TM), dtype=BF16)


def _hgrn_kernel(qf_ref, vf_ref, zf_ref, lbf_ref, gf_ref, qb_ref, vb_ref, zb_ref, lbb_ref, gb_ref,
                 of_ref, ob_ref, stf_ref, stb_ref):
    @pl.when(pl.program_id(1) == 0)
    def _():
        stf_ref[...] = jnp.zeros_like(stf_ref)
        stb_ref[...] = jnp.zeros_like(stb_ref)

    gates_f = _hgrn_gates(zf_ref, lbf_ref, gf_ref)
    gates_b = _hgrn_gates(zb_ref, lbb_ref, gb_ref)
    _hgrn_chunk(qf_ref, vf_ref, gates_f, of_ref, stf_ref, reverse=False)
    _hgrn_chunk(qb_ref, vb_ref, gates_b, ob_ref, stb_ref, reverse=True)


def _hgrn_gates(z_ref, lb_ref, g_ref):
    z = z_ref[...]
    log_lb = lb_ref[0:1, :]
    log_1m_lb = lb_ref[1:2, :]
    one_m_lb = lb_ref[2:3, :]
    log_sig = jnp.minimum(z, 0.0) - jnp.log1p(jnp.exp(-jnp.abs(z)))
    a2 = log_1m_lb + log_sig
    log_f = jnp.maximum(log_lb, a2) + jnp.log1p(jnp.exp(-jnp.abs(log_lb - a2)))
    kk = one_m_lb * _sigmoid(-z)

    return kk, _split_dot_left(g_ref[...], log_f * LOG2_E)


def _hgrn_chunk(q_ref, v_ref, gates, o_ref, st_ref, *, reverse):
    kk, wall = gates
    a = wall[0:TM]
    a_end = a[0:1] if reverse else a[TM - 1:TM]
    q = q_ref[...].astype(F32)
    v = v_ref[...]
    qa = (q * jnp.exp2(a)).astype(BF16)
    kd = (kk * jnp.exp2(a_end - a)).astype(BF16)
    carry = jnp.exp2(a_end)

    half = TM // 2
    tiles = (slice(0, half), slice(half, TM))
    q_tile, k_tile = (0, 1) if reverse else (1, 0)
    row = lax.broadcasted_iota(I32, (half, half), 0)
    col = lax.broadcasted_iota(I32, (half, half), 1)
    diff = row ^ col
    pos = lax.broadcasted_iota(I32, (TM, HGRN_W), 0)
    if reverse:
        pos = TM - 1 - pos

    def is_q_piece(idx):
        return idx % 2 == (0 if reverse else 1)

    fac, fac_q, fac_k = {}, {}, {}
    for lev in range(1, N_LEVELS):
        r = TM >> lev
        fac[lev] = jnp.exp2(wall[lev * TM:(lev + 1) * TM])
        if r < SUBLANES:
            is_q = ((pos >> (r.bit_length() - 1)) & 1) == 1
            fac_q[lev] = jnp.where(is_q, fac[lev], 0.0)
            fac_k[lev] = jnp.where(is_q, 0.0, fac[lev])

    def level_operands(lev, qh, kh, sl, rows):
        r = TM >> lev
        if r < SUBLANES:
            return (qh[rows] * fac_q[lev][rows, sl]).astype(BF16), (kh[rows] * fac_k[lev][rows, sl]).astype(BF16)
        qs, ks = [], []
        for start in range(rows.start, rows.stop, r):
            piece = slice(start, start + r)
            zero = jnp.zeros((r, HGRN_DIM), F32)
            if is_q_piece(start // r):
                qs.append(qh[piece] * fac[lev][piece, sl])
                ks.append(zero)
            else:
                qs.append(zero)
                ks.append(kh[piece] * fac[lev][piece, sl])
        cat = lambda xs: (xs[0] if len(xs) == 1 else jnp.concatenate(xs, axis=0)).astype(BF16)
        return cat(qs), cat(ks)

    heads = [slice(h * HGRN_DIM, (h + 1) * HGRN_DIM) for h in range(HGRN_HEADS)]
    p = {}
    for h, sl in enumerate(heads):
        for d, rows in enumerate(tiles):
            p[h, d] = jnp.where(diff == 0, _dot_nt(q[rows, sl].astype(BF16), kk[rows, sl].astype(BF16)), 0.0)
    for lev in range(2, N_LEVELS):
        block = 2 * (TM >> lev)
        for h, sl in enumerate(heads):
            for d, rows in enumerate(tiles):
                qt, kt = level_operands(lev, q[:, sl], kk[:, sl], sl, rows)
                sc = _dot_nt(qt, kt)
                p[h, d] = p[h, d] + (sc if block == half else jnp.where((diff >> (block.bit_length() - 1)) == 0, sc, 0.0))

    for h, sl in enumerate(heads):
        st = st_ref[h]
        qh, kh, vh = q[:, sl], kk[:, sl], v[:, sl]
        p_off = _dot_nt((qh[tiles[q_tile]] * fac[1][tiles[q_tile], sl]).astype(BF16),
                        (kh[tiles[k_tile]] * fac[1][tiles[k_tile], sl]).astype(BF16))
        inter = _dot_nt(qa[:, sl], st.astype(BF16))
        for d, rows in enumerate(tiles):
            o = inter[rows] + _dot(p[h, d].astype(BF16), vh[rows])
            if d == q_tile:
                o = o + _dot(p_off.astype(BF16), vh[tiles[k_tile]])
            o_ref[rows, sl] = o
        st_ref[h] = st * carry[:, sl] + _dot_tn(vh, kd[:, sl])


def _split_dot_left(m, x):
    hi = x.astype(BF16)
    lo = (x - hi.astype(F32)).astype(BF16)
    return _dot(m, hi) + _dot(m, lo)


def _hgrn(hg, hz, lb_fwd, lb_bwd, *, nb, tpb, nlt):
    n = hg.shape[0]
    nct = tpb - nlt

    def chunk(reverse):
        def index(b, s):
            if reverse:
                j = jnp.where(s < nct, tpb - 1 - s, nlt - 1 - (s - nct))
            else:
                j = jnp.where(s < nct, nlt + s, s - nct)
            return b * tpb + j
        return index

    def direction_specs(reverse, z_block, gmat):
        c = chunk(reverse)
        return [
            pl.BlockSpec((TM, HGRN_W), lambda b, s: (c(b, s), 0)),
            pl.BlockSpec((TM, HGRN_W), lambda b, s: (c(b, s), 1)),
            pl.BlockSpec((TM, HGRN_W), lambda b, s: (c(b, s), z_block)),
            pl.BlockSpec((8, HGRN_W), lambda b, s: (0, 0)),
            pl.BlockSpec(gmat.shape, lambda b, s: (0, 0)),
        ]

    g_fwd, g_bwd = _level_matrices(False), _level_matrices(True)
    cf, cb = chunk(False), chunk(True)
    state = pltpu.VMEM((HGRN_HEADS, HGRN_DIM, HGRN_DIM), F32)
    return pl.pallas_call(
        _hgrn_kernel,
        grid=(nb, tpb),
        in_specs=direction_specs(False, 0, g_fwd) + direction_specs(True, 1, g_bwd),
        out_specs=[
            pl.BlockSpec((TM, HGRN_W), lambda b, s: (cf(b, s), 0)),
            pl.BlockSpec((TM, HGRN_W), lambda b, s: (cb(b, s), 0)),
        ],
        out_shape=[jax.ShapeDtypeStruct((n, HGRN_W), F32)] * 2,
        scratch_shapes=[state, state],
        compiler_params=_cparams(("arbitrary", "arbitrary")),
        name="hgrn",
    )(hg, hg, hz, lb_fwd, g_fwd, hg, hg, hz, lb_bwd, g_bwd)


def _mla_heads(m, q_gain, kv_gain, wq_ref, wkv_ref, cos, sin, q_ref, k_ref, v_ref):
    cq = m[:, 0:Q_LORA]
    ckv = m[:, Q_LORA:Q_LORA + KV_LORA]
    kr = m[:, Q_LORA + KV_LORA:Q_LORA + KV_LORA + LANES]
    kr_sw = m[:, Q_LORA + KV_LORA + LANES:MLA_IN_W]
    qall = _dot((_rms(cq) * q_gain).astype(BF16), wq_ref[...]) * (MLA_SCALE * LOG2_E)
    kv = _dot((_rms(ckv) * kv_gain).astype(BF16), wkv_ref[...])
    k_rot = (kr * cos + kr_sw * sin).astype(BF16)
    ones = jnp.ones((TM, LANES), BF16)
    nw = MLA_HEADS * LANES
    for h in range(MLA_HEADS):
        a, b = h * LANES, (h + 1) * LANES
        s = h * HEAD_SLOT
        q_ref[:, s:s + LANES] = qall[:, a:b].astype(BF16)
        q_ref[:, s + LANES:s + HEAD_SLOT] = (qall[:, nw + a:nw + b] * cos + qall[:, 2 * nw + a:2 * nw + b] * sin).astype(BF16)
        k_ref[:, s:s + LANES] = kv[:, a:b].astype(BF16)
        k_ref[:, s + LANES:s + HEAD_SLOT] = k_rot
        v_ref[:, s:s + LANES] = kv[:, nw + a:nw + b].astype(BF16)
        v_ref[:, s + LANES:s + HEAD_SLOT] = ones


def _attn_kernel(q_ref, k_ref, v_ref, o_ref, s_ref, *, n_lat_steps, seq, ctx_len):
    j = pl.program_id(2)

    def run(rows, k0, k1):
        blocks = [(kb, slice(k0 + kb * KEY_BLOCK, k0 + (kb + 1) * KEY_BLOCK), slice(kb * KEY_BLOCK, (kb + 1) * KEY_BLOCK))
                  for kb in range((k1 - k0) // KEY_BLOCK)]
        hslot = lambda hh: slice(hh * HEAD_SLOT, (hh + 1) * HEAD_SLOT)

        def scores(hh, blk, m_run):
            _, ks, ss = blk
            s = _dot_nt(q_ref[0:rows, hslot(hh)], k_ref[ks, hslot(hh)])
            s_ref[hh % 2, 0:rows, ss] = s
            for c in range(KEY_BLOCK // LANES):
                m_run = jnp.maximum(m_run, s[:, c * LANES:(c + 1) * LANES])
            return m_run

        def values(hh, blk, m, o):
            _, ks, ss = blk
            d = (s_ref[hh % 2, 0:rows, ss] - m).astype(BF16)
            return o + _dot(jnp.exp2(d), v_ref[ks, hslot(hh)])

        def emit(hh, o):
            o_ref[0:rows, hh * MLA_V:(hh + 1) * MLA_V] = (o[:, 0:MLA_V] / o[:, MLA_V:]).astype(BF16)

        neg = jnp.full((rows, LANES), -jnp.inf, F32)
        zero = jnp.zeros((rows, HEAD_SLOT), F32)
        m_run = neg
        for blk in blocks:
            m_run = scores(0, blk, m_run)
        m_prev = jnp.max(m_run, axis=-1, keepdims=True)
        for hh in range(1, HEADS_PER_STEP):
            m_run, o = neg, zero
            for blk in blocks:
                m_run = scores(hh, blk, m_run)
                o = values(hh - 1, blk, m_prev, o)
            emit(hh - 1, o)
            m_prev = jnp.max(m_run, axis=-1, keepdims=True)
        o = zero
        for blk in blocks:
            o = values(HEADS_PER_STEP - 1, blk, m_prev, o)
        emit(HEADS_PER_STEP - 1, o)

    @pl.when(j < n_lat_steps)
    def _():
        run(TQ, 0, seq + ctx_len)

    @pl.when(j >= n_lat_steps)
    def _():
        run(ctx_len, seq, seq + ctx_len)


def _attention(q, k, v, *, nb, seq, ctx_len):
    n = q.shape[0]
    r = seq + ctx_len
    assert seq % TQ == 0 and ctx_len <= TQ
    n_lat_steps = seq // TQ
    hp = MLA_HEADS // HEADS_PER_STEP
    wq = HEADS_PER_STEP * HEAD_SLOT
    wo = HEADS_PER_STEP * MLA_V
    kern = functools.partial(_attn_kernel, n_lat_steps=n_lat_steps, seq=seq, ctx_len=ctx_len)
    out = pl.pallas_call(
        kern,
        grid=(nb, hp, n_lat_steps + 1),
        in_specs=[
            pl.BlockSpec((None, TQ, wq), lambda b, h, j: (b, j, h)),
            pl.BlockSpec((None, r, wq), lambda b, h, j: (b, 0, h)),
            pl.BlockSpec((None, r, wq), lambda b, h, j: (b, 0, h), pipeline_mode=pl.Buffered(1)),
        ],
        out_specs=pl.BlockSpec((None, TQ, wo), lambda b, h, j: (b, j, h)),
        out_shape=jax.ShapeDtypeStruct((nb, r, MLA_HEADS * MLA_V), BF16),
        scratch_shapes=[pltpu.VMEM((2, TQ, r), F32)],
        compiler_params=_cparams(("arbitrary", "arbitrary", "arbitrary")),
        name="attention",
    )(q.reshape(nb, r, -1), k.reshape(nb, r, -1), v.reshape(nb, r, -1))
    return out.reshape(n, MLA_HEADS * MLA_V)


def _pack_rows(x):
    w = x.shape[1] // 2
    lo = pltpu.bitcast(x[:, :w].astype(BF16).astype(F32), U32)
    hi = pltpu.bitcast(x[:, w:].astype(BF16).astype(F32), U32)
    return (lo >> 16) | (hi & jnp.uint32(0xFFFF0000))


def _store_row_tiles(ref, packed):
    rows = packed.shape[0]
    for s in range(ROW_TILE):
        ref[pl.ds(s, rows, stride=ROW_TILE), :] = packed[:, s * LANES:(s + 1) * LANES]


def _load_row_tiles(ref, rows):
    return jnp.concatenate([ref[pl.ds(s, rows, stride=ROW_TILE), :] for s in range(ROW_TILE)], axis=1)


def _unpack_rows(w):
    lo = pltpu.bitcast(w << 16, F32)
    hi = pltpu.bitcast(w & jnp.uint32(0xFFFF0000), F32)
    return lo, hi


def _post_kernel(yf_ref, of_ref, ob_ref, g_ref, att_ref, x_ref, mod_ref, hn_ref, wo_ref, lng_ref, lnb_ref,
                 rw_ref, rb_ref, x1_ref, h2_ref, eid_ref, gate_ref, base_ref, cnt_ref, *, tpb, nlt, nb, alpha):
    row = _mod_row(pl.program_id(0), tpb, nlt, nb)
    o = of_ref[...] + ob_ref[...]
    hn = hn_ref[...]
    gate_act = g_ref[...].astype(F32)
    gate_act = gate_act * _sigmoid(gate_act)
    mix = _dot(yf_ref[...], wo_ref[0:FOURIER_W, :])
    for h in range(HGRN_HEADS):
        sl = slice(h * HGRN_DIM, (h + 1) * HGRN_DIM)
        r = (_rms(o[:, sl]) * hn[:, sl] * gate_act[:, sl]).astype(BF16)
        mix = mix + _dot(r, wo_ref[FOURIER_W + h * HGRN_DIM:FOURIER_W + (h + 1) * HGRN_DIM, :])
    mix = mix + _dot(att_ref[...], wo_ref[FOURIER_W + HGRN_W:, :])

    x1 = _layer_norm(alpha * x_ref[...] + _mod(mod_ref, row, 2) * mix) * lng_ref[...] + lnb_ref[...]
    x1_ref[...] = x1
    h2 = _layer_norm(x1) * (1.0 + _mod(mod_ref, row, 4)) + _mod(mod_ref, row, 3)
    _store_row_tiles(h2_ref, _pack_rows(h2))

    hi = h2.astype(BF16)
    lo = (h2 - hi.astype(F32)).astype(BF16)
    rr = _dot(jnp.concatenate([hi, lo], axis=0), rw_ref[...])
    logits = rr[0:TM, 0:LANES] + rr[0:TM, LANES:] + rr[TM:, 0:LANES] + rr[TM:, LANES:] + rb_ref[...]
    lane = lax.broadcasted_iota(I32, logits.shape, 1)
    cur = logits
    vals, ids = [], []
    for _ in range(TOP_K):
        m = jnp.max(cur, axis=-1, keepdims=True)
        idx = jnp.min(jnp.where(cur == m, lane, LANES), axis=-1, keepdims=True)
        vals.append(m)
        ids.append(idx)
        cur = jnp.where(lane == idx, -3e38, cur)
    ex = [jnp.exp(vk - vals[0]) for vk in vals]
    den = ex[0] + ex[1] + ex[2] + ex[3]
    eid = jnp.zeros(logits.shape, I32)
    gate = jnp.zeros(logits.shape, F32)
    for k in range(TOP_K):
        eid = jnp.where(lane == k, ids[k], eid)
        gate = jnp.where(lane == k, ex[k] / den, gate)
    eid_ref[...] = eid
    gate_ref[...] = gate

    i = pl.program_id(0)

    @pl.when(i == 0)
    def _():
        base_ref[...] = jnp.zeros_like(base_ref)
        cnt_ref[...] = jnp.zeros_like(cnt_ref)

    base_ref[pl.ds(i, 1), :] = cnt_ref[0:1, :]
    tile_cnt = jnp.zeros((1, LANES), F32)
    for k in range(TOP_K):
        tile_cnt = tile_cnt + jnp.sum((lane == ids[k]).astype(F32), axis=0, keepdims=True)
    cnt_ref[0:1, :] = cnt_ref[0:1, :] + tile_cnt


def _post(yf, o_f, o_b, hg, att, x, mod_all, hnorm, w_out_all, ln_g, ln_b, rw_all, rb, layer, *, tpb, nlt, nb, alpha):
    n = x.shape[0]
    nt_pad = -(-(n // TM) // SUBLANES) * SUBLANES
    kern = functools.partial(_post_kernel, tpb=tpb, nlt=nlt, nb=nb, alpha=alpha)
    full = lambda a: pl.BlockSpec(a.shape, lambda i: (0, 0))
    return pl.pallas_call(
        kern,
        grid=(n // TM,),
        in_specs=[
            pl.BlockSpec((TM, FOURIER_W), lambda i: (i, 0)),
            pl.BlockSpec((TM, HGRN_W), lambda i: (i, 0)),
            pl.BlockSpec((TM, HGRN_W), lambda i: (i, 0)),
            pl.BlockSpec((TM, HGRN_W), lambda i: (i, 2)),
            pl.BlockSpec((TM, MLA_HEADS * MLA_V), lambda i: (i, 0)),
            pl.BlockSpec((TM, D_MODEL), lambda i: (i, 0)),
            _layer_block(mod_all, layer), full(hnorm),
            _layer_block(w_out_all, layer, pipeline_mode=pl.Buffered(1)),
            full(ln_g), full(ln_b), _layer_block(rw_all, layer), full(rb),
        ],
        out_specs=[
            pl.BlockSpec((TM, D_MODEL), lambda i: (i, 0)),
            pl.BlockSpec((TM * ROW_TILE, LANES), lambda i: (i, 0)),
            pl.BlockSpec((TM, LANES), lambda i: (i, 0)),
            pl.BlockSpec((TM, LANES), lambda i: (i, 0)),
            pl.BlockSpec((nt_pad, LANES), lambda i: (0, 0)),
            pl.BlockSpec((SUBLANES, LANES), lambda i: (0, 0)),
        ],
        out_shape=[
            jax.ShapeDtypeStruct((n, D_MODEL), F32),
            jax.ShapeDtypeStruct((n * ROW_TILE, LANES), U32),
            jax.ShapeDtypeStruct((n, LANES), I32),
            jax.ShapeDtypeStruct((n, LANES), F32),
            jax.ShapeDtypeStruct((nt_pad, LANES), F32),
            jax.ShapeDtypeStruct((SUBLANES, LANES), F32),
        ],
        compiler_params=_cparams(("arbitrary",)),
        name="post",
    )(yf, o_f, o_b, hg, att, x, mod_all, hnorm, w_out_all, ln_g, ln_b, rw_all, rb)


def _route_kernel(eid_ref, base_ref, cnt_ref, pos_ref, te_ref, meta_ref, *, n_tiles_pad):
    i = pl.program_id(0)
    lane = lax.broadcasted_iota(I32, (TM, LANES), 1)
    eid = eid_ref[...]
    onehot = [(lane == eid[:, k:k + 1]).astype(F32) for k in range(TOP_K)]
    colsum = [jnp.sum(oh, axis=0, keepdims=True) for oh in onehot]

    cnt = cnt_ref[0:1, :]
    tiles = jnp.floor((cnt + (TMM - 1)) * (1.0 / TMM))
    r128 = lax.broadcasted_iota(I32, (LANES, LANES), 0)
    c128 = lax.broadcasted_iota(I32, (LANES, LANES), 1)
    incl = (r128 <= c128).astype(BF16)
    pad_end = _split_dot(jnp.broadcast_to(tiles, (8, LANES)), incl)[0:1, :] * TMM
    pad_start = pad_end - tiles * TMM

    rr = lax.broadcasted_iota(I32, (TM, TM), 0)
    cc = lax.broadcasted_iota(I32, (TM, TM), 1)
    strict = (cc < rr).astype(BF16)
    run = pad_start + base_ref[pl.ds(i, 1), :]
    pos = jnp.zeros((TM, LANES), I32)
    for k in range(TOP_K):
        before = _dot(strict, onehot[k].astype(BF16))
        val = jnp.sum(onehot[k] * (before + run), axis=1, keepdims=True)
        pos = jnp.where(lane == k, val.astype(I32), pos)
        run = run + colsum[k]
    pos_ref[...] = pos

    @pl.when(i == 0)
    def _():
        tstart = (lax.broadcasted_iota(I32, (n_tiles_pad, LANES), 0) * TMM).astype(F32)
        lane_t = lax.broadcasted_iota(I32, (n_tiles_pad, LANES), 1)
        done = jnp.where((pad_end <= tstart) & (lane_t < N_EXPERTS), 1.0, 0.0)
        te = jnp.minimum(jnp.sum(done, axis=1, keepdims=True), N_EXPERTS - 1.0)
        te_ref[...] = jnp.broadcast_to(te, (n_tiles_pad, LANES)).astype(I32)
        sub = lax.broadcasted_iota(I32, (8, LANES), 0)
        meta = jnp.where(sub == 0, pad_start + cnt, jnp.where(sub == 1, pad_end - pad_start - cnt, pad_end))
        meta_ref[...] = meta.astype(I32)


def _route(eid, base, cnt, n_tiles):
    n = eid.shape[0]
    n_tiles_pad = -(-n_tiles // 8) * 8
    kern = functools.partial(_route_kernel, n_tiles_pad=n_tiles_pad)
    full = lambda a: pl.BlockSpec(a.shape, lambda i: (0, 0))
    pos, te, meta = pl.pallas_call(
        kern,
        grid=(n // TM,),
        in_specs=[pl.BlockSpec((TM, LANES), lambda i: (i, 0)), full(base), full(cnt)],
        out_specs=[
            pl.BlockSpec((TM, LANES), lambda i: (i, 0)),
            pl.BlockSpec((n_tiles_pad, LANES), lambda i: (0, 0)),
            pl.BlockSpec((8, LANES), lambda i: (0, 0)),
        ],
        out_shape=[
            jax.ShapeDtypeStruct((n, LANES), I32),
            jax.ShapeDtypeStruct((n_tiles_pad, LANES), I32),
            jax.ShapeDtypeStruct((8, LANES), I32),
        ],
        compiler_params=_cparams(("arbitrary",)),
        name="moe_route",
    )(eid, base, cnt)
    pos_flat = pos[:, :TOP_K].reshape(-1)
    tile_e = te[:n_tiles, 0]
    fill_start = meta[0, :N_EXPERTS]
    fill_n = meta[1, :N_EXPERTS]
    n_used = meta[2, N_EXPERTS - 1:N_EXPERTS] // TMM
    return pos_flat, tile_e, n_used, fill_start, fill_n


def _dispatch_kernel(pos_ref, fs_ref, fn_ref, nu_ref, h_ref, o_hbm, zero_ref, sem, fill_sem, *, n_tiles):
    i = pl.program_id(0)

    def rows(ref, first, count):
        return ref.at[pl.ds(pl.multiple_of(first * ROW_TILE, ROW_TILE), count * ROW_TILE), :]

    def start(r, c):
        for k in range(TOP_K):
            p = pos_ref[(i * TM + r) * TOP_K + k]
            pltpu.make_async_copy(rows(h_ref, r, 1), rows(o_hbm, p, 1), sem).start()
        return c

    lax.fori_loop(0, TM, start, 0, unroll=ISSUE_UNROLL)

    @pl.when(i == 0)
    def _():
        zero_ref[...] = jnp.zeros_like(zero_ref)

        def fill(first, count):
            return pltpu.make_async_copy(rows(zero_ref, 0, count), rows(o_hbm, first, count), fill_sem)

        def each(fn):
            for e in range(N_EXPERTS):
                first, count = fs_ref[e], fn_ref[e]
                piece = TMM // 2
                while piece:
                    @pl.when((count & piece) != 0)
                    def _(first=first, piece=piece):
                        fn(fill(first, piece))

                    first = first + (count & piece)
                    piece //= 2
            lax.fori_loop(nu_ref[0], n_tiles, lambda t, c: (fn(fill(t * TMM, TMM)), c)[1], 0)

        each(lambda cp: cp.start())
        each(lambda cp: cp.wait())

    for _ in range(TOP_K):
        pltpu.make_async_copy(h_ref, rows(o_hbm, 0, TM), sem).wait()


def _dispatch(pos_flat, fill_start, fill_n, n_used, h2p, n_tiles):
    n = h2p.shape[0] // ROW_TILE
    return pl.pallas_call(
        functools.partial(_dispatch_kernel, n_tiles=n_tiles),
        grid_spec=pltpu.PrefetchScalarGridSpec(
            num_scalar_prefetch=4,
            grid=(n // TM,),
            in_specs=[pl.BlockSpec((TM * ROW_TILE, LANES), lambda i, p, fs, fn, nu: (i, 0))],
            out_specs=pl.BlockSpec(memory_space=pl.ANY),
            scratch_shapes=[pltpu.VMEM((TMM * ROW_TILE, LANES), U32), pltpu.SemaphoreType.DMA(()),
                            pltpu.SemaphoreType.DMA(())],
        ),
        out_shape=jax.ShapeDtypeStruct((n_tiles * TMM * ROW_TILE, LANES), U32),
        compiler_params=_cparams(("arbitrary",)),
        name="moe_dispatch",
    )(pos_flat, fill_start, fill_n, n_used, h2p)


def _w1_kernel(w_ref, g_ref, l_ref, t_ref):
    for c in range(D_MODEL // LANES):
        cs = slice(c * LANES, (c + 1) * LANES)
        t_ref[c] = w_ref[cs, :].T
        g_ref[:, cs] = t_ref[c, pl.ds(0, W1_TN // 2, stride=2), :].astype(BF16)
        l_ref[:, cs] = t_ref[c, pl.ds(1, W1_TN // 2, stride=2), :].astype(BF16)


def _prep_w1(w1):
    depth, ne, d, n2 = w1.shape
    out = jax.ShapeDtypeStruct((depth * ne, n2 // 2, d), BF16)
    g, l = pl.pallas_call(
        _w1_kernel,
        grid=(depth * ne, n2 // W1_TN),
        in_specs=[pl.BlockSpec((None, d, W1_TN), lambda e, j: (e, 0, j))],
        out_specs=[pl.BlockSpec((None, W1_TN // 2, d), lambda e, j: (e, j, 0))] * 2,
        out_shape=[out, out],
        scratch_shapes=[pltpu.VMEM((d // LANES, W1_TN, LANES), F32)],
        compiler_params=_cparams(("arbitrary", "arbitrary")),
        name="moe_w1_prep",
    )(w1.reshape(depth * ne, d, n2))
    return g, l


def _moe_kernel(te_ref, nu_ref, nxt_ref, slot_ref, x_ref, w1g_hbm, w1l_hbm, b1g_ref, b1l_ref, w2_hbm, b2_ref, y_ref,
                w1g_buf, w1l_buf, w2_buf, w2b_ref, sem, *, layer):
    i = pl.program_id(0)

    def fetch(expert, slot):
        e = layer * N_EXPERTS + expert
        return [pltpu.make_async_copy(src.at[e], dst.at[slot], sem.at[slot])
                for src, dst in ((w1g_hbm, w1g_buf), (w1l_hbm, w1l_buf), (w2_hbm, w2_buf))]

    @pl.when(i < nu_ref[0])
    def _():
        expert = te_ref[i]
        slot = slot_ref[i]

        @pl.when(i == 0)
        def _():
            for cp in fetch(expert, slot):
                cp.start()

        @pl.when((i == 0) | (expert != te_ref[jnp.maximum(i - 1, 0)]))
        def _():
            for cp in fetch(expert, slot):
                cp.wait()
            w2b_ref[...] = w2_buf[slot].astype(BF16)

            @pl.when(nxt_ref[i] >= 0)
            def _():
                for cp in fetch(nxt_ref[i], 1 - slot):
                    cp.start()

        w1g_ref = w1g_buf.at[slot]
        w1l_ref = w1l_buf.at[slot]
        lo, hi = _unpack_rows(_load_row_tiles(x_ref, TMM))
        lo = lo.astype(BF16)
        hi = hi.astype(BF16)
        ug = _dot_nt(lo, w1g_ref[:, 0:HALF]) + _dot_nt(hi, w1g_ref[:, HALF:]) + b1g_ref[...]
        ul = _dot_nt(lo, w1l_ref[:, 0:HALF]) + _dot_nt(hi, w1l_ref[:, HALF:]) + b1l_ref[...]
        xg = jnp.minimum(ug, SWIGLU_LIMIT)
        xl = jnp.clip(ul, -SWIGLU_LIMIT, SWIGLU_LIMIT)
        act = xg * _sigmoid(SWIGLU_ALPHA * xg) * (xl + 1.0)
        y = _dot(act.astype(BF16), w2b_ref[...]) + b2_ref[...]
        _store_row_tiles(y_ref, _pack_rows(y))

    @pl.when(i >= nu_ref[0])
    def _():
        y_ref[...] = jnp.zeros_like(y_ref)


def _expert_runs(tile_e, n_used):
    used = jnp.arange(tile_e.shape[0]) < n_used[0]
    present = jnp.zeros((N_EXPERTS,), I32).at[jnp.where(used, tile_e, N_EXPERTS)].set(1, mode="drop")
    ids = jnp.arange(N_EXPERTS, dtype=I32)
    later = jnp.where(present > 0, ids, N_EXPERTS)
    next_present = lax.cummin(jnp.concatenate([later[1:], jnp.full((1,), N_EXPERTS, I32)])[::-1])[::-1]
    nxt = next_present[tile_e]
    nxt = jnp.where(nxt >= N_EXPERTS, -1, nxt).astype(I32)
    slot = ((jnp.cumsum(present) - 1)[tile_e] % 2).astype(I32)
    return nxt, slot


def _moe_ffn(tile_e, n_used, xs, w1g, w1l, b1g, b1l, w2, b2, layer):
    m = xs.shape[0] // ROW_TILE
    nxt, slot = _expert_runs(tile_e, n_used)
    emap = lambda i, te, nu, nx, sl: (layer * N_EXPERTS + te[i], 0, 0)
    row_tiles = pl.BlockSpec((TMM * ROW_TILE, LANES), lambda i, te, nu, nx, sl: (i, 0))
    hbm = pl.BlockSpec(memory_space=pl.ANY)
    return pl.pallas_call(
        functools.partial(_moe_kernel, layer=layer),
        grid_spec=pltpu.PrefetchScalarGridSpec(
            num_scalar_prefetch=4,
            grid=(m // TMM,),
            in_specs=[
                row_tiles,
                hbm,
                hbm,
                pl.BlockSpec((None, 1, EXPERT_FF), emap),
                pl.BlockSpec((None, 1, EXPERT_FF), emap),
                hbm,
                pl.BlockSpec((None, 1, D_MODEL), emap),
            ],
            out_specs=row_tiles,
            scratch_shapes=[
                pltpu.VMEM((2, EXPERT_FF, D_MODEL), BF16),
                pltpu.VMEM((2, EXPERT_FF, D_MODEL), BF16),
                pltpu.VMEM((2, EXPERT_FF, D_MODEL), F32),
                pltpu.VMEM((EXPERT_FF, D_MODEL), BF16),
                pltpu.SemaphoreType.DMA((2,)),
            ],
        ),
        out_shape=jax.ShapeDtypeStruct((m * ROW_TILE, LANES), U32),
        compiler_params=_cparams(("arbitrary",)),
        name="moe_ffn",
    )(tile_e, n_used, nxt, slot, xs, w1g, w1l, b1g, b1l, w2, b2)


def _combine_kernel(pos_ref, y_hbm, gate_ref, x_ref, mod_ref, lng_ref, lnb_ref, o_ref, buf, sem, *, tpb, nlt, nb, alpha,
                    latent_only):
    i = pl.program_id(0)
    n_steps = pl.num_programs(0)
    row = _mod_row(i, tpb, nlt, nb)

    def issue(tile, slot):
        def start(r, c):
            for k in range(TOP_K):
                p = pos_ref[(tile * TM + r) * TOP_K + k]
                src = y_hbm.at[pl.ds(pl.multiple_of(p * ROW_TILE, ROW_TILE), ROW_TILE), :]
                dst = buf.at[slot, k, pl.ds(pl.multiple_of(r * ROW_TILE, ROW_TILE), ROW_TILE), :]
                pltpu.make_async_copy(src, dst, sem.at[slot]).start()
            return c

        lax.fori_loop(0, TM, start, 0, unroll=ISSUE_UNROLL)

    @pl.when(i == 0)
    def _():
        issue(0, 0)

    @pl.when(i + 1 < n_steps)
    def _():
        issue(i + 1, (i + 1) % 2)

    slot = i % 2
    for k in range(TOP_K):
        pltpu.make_async_copy(y_hbm.at[pl.ds(0, TM * ROW_TILE), :], buf.at[slot, k], sem.at[slot]).wait()

    gates = gate_ref[...]
    acc_lo = jnp.zeros((TM, HALF), F32)
    acc_hi = jnp.zeros((TM, HALF), F32)
    for k in range(TOP_K):
        lo, hi = _unpack_rows(_load_row_tiles(buf.at[slot, k], TM))
        gk = gates[:, k:k + 1]
        acc_lo = acc_lo + gk * lo
        acc_hi = acc_hi + gk * hi
    ffn = jnp.concatenate([acc_lo, acc_hi], axis=-1)
    x2 = _layer_norm(alpha * x_ref[...] + _mod(mod_ref, row, 5) * ffn) * lng_ref[...] + lnb_ref[...]
    if latent_only:
        @pl.when(i % tpb < nlt)
        def _():
            o_ref[...] = x2
    else:
        o_ref[...] = x2


def _combine(pos, ys, gates, x, mod_all, ln_g, ln_b, layer, *, latent_only, tpb, nlt, nb, alpha):
    n = x.shape[0]
    kern = functools.partial(_combine_kernel, tpb=tpb, nlt=nlt, nb=nb, alpha=alpha, latent_only=latent_only)
    full = lambda a: pl.BlockSpec(a.shape, lambda i, p: (0, 0))
    if latent_only:
        omap = lambda i, p: ((i // tpb) * nlt + jnp.minimum(i % tpb, nlt - 1), 0)
        n_out = nb * nlt * TM
    else:
        omap = lambda i, p: (i, 0)
        n_out = n
    return pl.pallas_call(
        kern,
        grid_spec=pltpu.PrefetchScalarGridSpec(
            num_scalar_prefetch=1,
            grid=(n // TM,),
            in_specs=[
                pl.BlockSpec(memory_space=pl.ANY),
                pl.BlockSpec((TM, LANES), lambda i, p: (i, 0)),
                pl.BlockSpec((TM, D_MODEL), lambda i, p: (i, 0)),
                _layer_block(mod_all, layer), full(ln_g), full(ln_b),
            ],
            out_specs=pl.BlockSpec((TM, D_MODEL), omap),
            scratch_shapes=[pltpu.VMEM((2, TOP_K, TM * ROW_TILE, LANES), U32), pltpu.SemaphoreType.DMA((2,))],
        ),
        out_shape=jax.ShapeDtypeStruct((n_out, D_MODEL), F32),
        compiler_params=_cparams(("arbitrary",)),
        name="moe_combine",
    )(pos, ys, gates, x, mod_all, ln_g, ln_b)


def _dft_pair(t_len):
    assert t_len % GRID_W == 0
    k = jnp.arange(t_len, dtype=I32)

    def table(t):
        ang = ((t[:, None] * k[None, :]) % t_len).astype(F32) * (2.0 * np.pi / t_len)
        return jnp.cos(ang), jnp.sin(ang)

    ca, sa = table(jnp.arange(t_len // GRID_W, dtype=I32) * GRID_W)
    cb, sb = table(jnp.arange(GRID_W, dtype=I32))
    s = 1.0 / np.sqrt(t_len)
    cos = ca[:, None, :] * cb[None, :, :] - sa[:, None, :] * sb[None, :, :]
    sin = sa[:, None, :] * cb[None, :, :] + ca[:, None, :] * sb[None, :, :]
    return (cos * s).reshape(t_len, t_len).astype(BF16), (sin * s).reshape(t_len, t_len).astype(BF16)


def _channel_dft():
    c = np.arange(FOURIER_GROUP)
    ang = 2.0 * np.pi * ((c[:, None] * c[None, :]) % FOURIER_GROUP) / FOURIER_GROUP
    s = 1.0 / np.sqrt(FOURIER_GROUP)
    groups = FOURIER_W // FOURIER_GROUP
    m = np.zeros((FOURIER_W, 2 * FOURIER_W), np.float32)
    for g in range(groups):
        a, b = g * FOURIER_GROUP, (g + 1) * FOURIER_GROUP
        m[a:b, a:b] = np.cos(ang) * s
        m[a:b, FOURIER_W + a:FOURIER_W + b] = np.sin(ang) * s
    return jnp.asarray(m, dtype=BF16)


def _pair_swap(w):
    pairs = w.reshape(*w.shape[:-1], w.shape[-1] // 2, 2)
    return jnp.stack([-pairs[..., 1], pairs[..., 0]], axis=-1).reshape(w.shape)


def _pad_lanes(w, width):
    return jnp.pad(w, [(0, 0)] * (w.ndim - 1) + [(0, width - w.shape[-1])])


def _rope_tables(nb, seq, ctx_len):
    t = jnp.arange(seq)
    n_freq = MLA_ROPE // 4
    inv = ROPE_BASE ** (-jnp.arange(n_freq, dtype=F32) / n_freq)
    ang = jnp.concatenate([(t // GRID_W)[:, None] * inv, (t % GRID_W)[:, None] * inv], axis=-1)
    ang = jnp.repeat(ang, 2, axis=-1)
    cos = jnp.concatenate([jnp.cos(ang), jnp.ones((ctx_len, MLA_ROPE), F32)], axis=0)
    sin = jnp.concatenate([jnp.sin(ang), jnp.zeros((ctx_len, MLA_ROPE), F32)], axis=0)
    cos = jnp.tile(_pad_lanes(cos, LANES), (nb, 1))
    sin = jnp.tile(_pad_lanes(sin, LANES), (nb, 1))
    return cos, sin


def _prep_in_proj(w_in):
    f, q, i, g, zf, zb, cq, ckv, kr = jnp.split(w_in, np.cumsum([512, 512, 512, 512, 512, 512, 512, 256]).tolist(), axis=-1)
    cols = [f, q, i, g, zf, zb, cq, ckv, _pad_lanes(kr, LANES), _pad_lanes(_pair_swap(kr), LANES)]
    return jnp.concatenate(cols, axis=-1).astype(BF16)


def _prep_wq(w_uq):
    w = w_uq.reshape(Q_LORA, MLA_HEADS, MLA_NOPE + MLA_ROPE)
    nope = w[:, :, :MLA_NOPE].reshape(Q_LORA, -1)
    rope = w[:, :, MLA_NOPE:]
    rope_p = _pad_lanes(rope, LANES).reshape(Q_LORA, -1)
    rope_s = _pad_lanes(_pair_swap(rope), LANES).reshape(Q_LORA, -1)
    return jnp.concatenate([nope, rope_p, rope_s], axis=-1).astype(BF16)


def _prep_wkv(w_ukv):
    w = w_ukv.reshape(KV_LORA, MLA_HEADS, MLA_NOPE + MLA_V)
    return jnp.concatenate([w[:, :, :MLA_NOPE].reshape(KV_LORA, -1), w[:, :, MLA_NOPE:].reshape(KV_LORA, -1)], axis=-1).astype(BF16)


def kernel(x, c, ctx, c_ctx, w_ada, b_ada, w_in, mla_q_norm, w_uq, mla_kv_norm, w_ukv, hgrn_lb_logits, hgrn_norm,
           w_out, ln1_g, ln1_b, router_w, router_b, w1, b1, w2, b2, ln2_g, ln2_b):
    nb, seq, d = x.shape
    ctx_len = ctx.shape[1]
    depth = w_ada.shape[0]
    assert d == D_MODEL and seq % TM == 0 and ctx_len % TM == 0 and seq % GRID_W == 0
    r = seq + ctx_len
    tpb, nlt = r // TM, seq // TM
    n = nb * r
    n_tiles = -(-n * TOP_K // TMM) + N_EXPERTS
    alpha = float((2 * depth) ** 0.25)
    geo = dict(tpb=tpb, nlt=nlt, nb=nb)

    xs = jnp.concatenate([x, ctx], axis=1).reshape(n, d)

    mod_rows = -(-(nb + 1) // 8) * 8
    cc = jnp.zeros((mod_rows, d), F32).at[:nb].set(c).at[nb].set(c_ctx)
    mod_all = _ada(cc, w_ada, b_ada)

    lb = jnp.cumsum(jax.nn.softmax(hgrn_lb_logits.astype(F32), axis=0), axis=0)
    lb = lb - lb[:1]
    lb_rows = jnp.stack([jnp.log(lb), jnp.log1p(-lb), 1.0 - lb] + [jnp.zeros_like(lb)] * 5, axis=2)

    dft_c = _channel_dft()
    dft_lat = _dft_pair(seq)
    dft_ctx = _dft_pair(ctx_len)
    cos_t, sin_t = _rope_tables(nb, seq, ctx_len)

    w_in_all = _prep_in_proj(w_in)
    wq_all = jax.vmap(_prep_wq)(w_uq)
    wkv_all = jax.vmap(_prep_wkv)(w_ukv)
    w_out_all = w_out.astype(BF16)
    w1g_all, w1l_all = _prep_w1(w1)
    le = depth * N_EXPERTS
    w2_all = w2.reshape(le, EXPERT_FF, d)
    b1g_all = b1[:, :, 0::2].reshape(le, 1, EXPERT_FF)
    b1l_all = b1[:, :, 1::2].reshape(le, 1, EXPERT_FF)
    b2_all = b2.reshape(le, 1, d)
    rw_f32 = _pad_lanes(router_w, LANES)
    rw_hi = rw_f32.astype(BF16)
    rw_all = jnp.concatenate([rw_hi, (rw_f32 - rw_hi.astype(F32)).astype(BF16)], axis=-1)

    for l in range(depth):
        row = lambda a: a[l].reshape(1, -1)
        z, hg, hz, q, k, v = _pre(xs, mod_all, w_in_all, dft_c, row(mla_q_norm), row(mla_kv_norm), wq_all, wkv_all,
                              cos_t, sin_t, l, **geo)
        yf = _fourier(z, dft_lat, dft_ctx, nb=nb, tpb=tpb, nlt=nlt, seq=seq, ctx_len=ctx_len)
        o_f, o_b = _hgrn(hg, hz, lb_rows[l, 0], lb_rows[l, 1], **geo)
        att = _attention(q, k, v, nb=nb, seq=seq, ctx_len=ctx_len)

        rb = jnp.full((1, LANES), NEG_BIG, F32).at[0, :N_EXPERTS].set(router_b[l])
        x1, h2p, eid, gates, base, cnt = _post(yf, o_f, o_b, hg, att, xs, mod_all, row(hgrn_norm), w_out_all, row(ln1_g), row(ln1_b),
                                    rw_all, rb, l, alpha=alpha, **geo)

        pos, tile_e, n_used, fill_start, fill_n = _route(eid, base, cnt, n_tiles)
        xg = _dispatch(pos, fill_start, fill_n, n_used, h2p, n_tiles)
        ys = _moe_ffn(tile_e, n_used, xg, w1g_all, w1l_all, b1g_all, b1l_all, w2_all, b2_all, l)
        xs = _combine(pos, ys, gates, x1, mod_all, row(ln2_g), row(ln2_b), l, latent_only=l == depth - 1, alpha=alpha, **geo)

    return xs.reshape(nb, seq, d)
```

```python
import functools

import numpy as np
import jax
import jax.numpy as jnp
from jax import lax
from jax.experimental import pallas as pl
from jax.experimental.pallas import tpu as pltpu

F32 = jnp.float32
BF16 = jnp.bfloat16
U32 = jnp.uint32
I32 = jnp.int32

D_MODEL = 2048
FOURIER_W = 512
FOURIER_GROUP = 128
HGRN_W = 512
HGRN_HEADS = 4
HGRN_DIM = 128
MLA_HEADS = 8
MLA_NOPE = 128
MLA_ROPE = 64
MLA_V = 128
Q_LORA = 512
KV_LORA = 256
MLA_SCALE = (MLA_NOPE + MLA_ROPE) ** -0.5
N_EXPERTS = 32
TOP_K = 4
EXPERT_FF = 768
SWIGLU_ALPHA = 1.702
SWIGLU_LIMIT = 7.0
N_MOD = 6
GRID_W = 64
ROPE_BASE = 10000.0
EPS = 1e-6
LOG2_E = 1.4426950408889634

LANES = 128
SUBLANES = 8
TM = 256
TMM = 512
TQ = 2 * TM
HEAD_SLOT = 2 * LANES
HEADS_PER_STEP = 4
KEY_BLOCK = 256
IN_W = 4096
HG_OFF = FOURIER_W
MLA_OFF = HG_OFF + 5 * HGRN_W
MLA_IN_W = IN_W - MLA_OFF
N_LEVELS = 9
ADA_TN = 1024
W1_TN = 768
HALF = D_MODEL // 2
ROW_TILE = HALF // LANES
assert ROW_TILE == SUBLANES
NEG_BIG = -1e30
ISSUE_UNROLL = 8
VMEM_LIMIT = 56 * 1024 * 1024


def _cparams(sem, vmem=VMEM_LIMIT):
    return pltpu.CompilerParams(dimension_semantics=sem, vmem_limit_bytes=vmem)


def _dot(a, b):
    return jnp.dot(a, b, preferred_element_type=F32)


def _dot_nt(a, b):
    return lax.dot_general(a, b, (((1,), (1,)), ((), ())), preferred_element_type=F32)


def _dot_tn(a, b):
    return lax.dot_general(a, b, (((0,), (0,)), ((), ())), preferred_element_type=F32)


def _split_dot(x, m):
    hi = x.astype(BF16)
    lo = (x - hi.astype(F32)).astype(BF16)
    return _dot(hi, m) + _dot(lo, m)


def _layer_norm(x):
    mu = jnp.mean(x, axis=-1, keepdims=True)
    xc = x - mu
    var = jnp.mean(xc * xc, axis=-1, keepdims=True)
    return xc * lax.rsqrt(var + EPS)


def _rms(x):
    return x * lax.rsqrt(jnp.mean(x * x, axis=-1, keepdims=True) + EPS)


def _sigmoid(x):
    return 1.0 / (1.0 + jnp.exp(-x))


def _mod_row(i, tpb, nlt, nb):
    return jnp.where(i % tpb >= nlt, nb, i // tpb)


def _mod(mod_ref, row, k):
    return mod_ref[pl.ds(row, 1), k * D_MODEL:(k + 1) * D_MODEL]


def _ada_kernel(c_ref, w_ref, b_ref, o_ref):
    c = c_ref[...]
    s = c * _sigmoid(c)
    hi = s.astype(BF16)
    lo = (s - hi.astype(F32)).astype(BF16)
    w = w_ref[...].astype(BF16)
    o_ref[...] = _dot(hi, w) + _dot(lo, w) + b_ref[...]


def _ada(cc, w_ada, b_ada):
    depth, d, n = w_ada.shape
    rows = cc.shape[0]
    return pl.pallas_call(
        _ada_kernel,
        grid=(depth, n // ADA_TN),
        in_specs=[
            pl.BlockSpec((rows, d), lambda l, j: (0, 0)),
            pl.BlockSpec((None, d, ADA_TN), lambda l, j: (l, 0, j)),
            pl.BlockSpec((None, 1, ADA_TN), lambda l, j: (l, 0, j)),
        ],
        out_specs=pl.BlockSpec((None, rows, ADA_TN), lambda l, j: (l, 0, j)),
        out_shape=jax.ShapeDtypeStruct((depth, rows, n), F32),
        compiler_params=_cparams(("arbitrary", "arbitrary")),
        name="ada",
    )(cc, w_ada, b_ada.reshape(depth, 1, n))


def _pre_kernel(x_ref, mod_ref, w_ref, dft_ref, qg_ref, kvg_ref, wq_ref, wkv_ref, cos_ref, sin_ref,
                z_ref, hg_ref, hz_ref, q_ref, k_ref, v_ref, *, tpb, nlt, nb):
    row = _mod_row(pl.program_id(0), tpb, nlt, nb)
    h = _layer_norm(x_ref[...]) * (1.0 + _mod(mod_ref, row, 1)) + _mod(mod_ref, row, 0)
    r = _dot(h.astype(BF16), w_ref[...])
    z_ref[...] = _dot(r[:, 0:FOURIER_W].astype(BF16), dft_ref[...]).astype(BF16)
    hg_ref[...] = r[:, HG_OFF:HG_OFF + 3 * HGRN_W].astype(BF16)
    hz_ref[...] = r[:, HG_OFF + 3 * HGRN_W:MLA_OFF]
    _mla_heads(r[:, MLA_OFF:IN_W], qg_ref[...], kvg_ref[...], wq_ref, wkv_ref, cos_ref[...], sin_ref[...],
               q_ref, k_ref, v_ref)


def _layer_block(a, layer, **kw):
    return pl.BlockSpec((None,) + a.shape[1:], lambda *_: (layer, 0, 0), **kw)


def _pre(x, mod_all, w_in_all, dft_c, q_gain, kv_gain, wq_all, wkv_all, cos_t, sin_t, layer, *, tpb, nlt, nb):
    n = x.shape[0]
    w = MLA_HEADS * HEAD_SLOT
    kern = functools.partial(_pre_kernel, tpb=tpb, nlt=nlt, nb=nb)
    full = lambda a: pl.BlockSpec(a.shape, lambda i: (0, 0))
    rows = lambda width: pl.BlockSpec((TM, width), lambda i: (i, 0))
    once = dict(pipeline_mode=pl.Buffered(1))
    return pl.pallas_call(
        kern,
        grid=(n // TM,),
        in_specs=[
            rows(D_MODEL),
            _layer_block(mod_all, layer),
            _layer_block(w_in_all, layer, **once),
            full(dft_c), full(q_gain), full(kv_gain),
            _layer_block(wq_all, layer, **once), _layer_block(wkv_all, layer, **once),
            rows(LANES), rows(LANES),
        ],
        out_specs=[rows(2 * FOURIER_W), rows(3 * HGRN_W), rows(2 * HGRN_W), rows(w), rows(w), rows(w)],
        out_shape=[
            jax.ShapeDtypeStruct((n, 2 * FOURIER_W), BF16),
            jax.ShapeDtypeStruct((n, 3 * HGRN_W), BF16),
            jax.ShapeDtypeStruct((n, 2 * HGRN_W), F32),
            jax.ShapeDtypeStruct((n, w), BF16),
            jax.ShapeDtypeStruct((n, w), BF16),
            jax.ShapeDtypeStruct((n, w), BF16),
        ],
        compiler_params=_cparams(("arbitrary",)),
        name="pre",
    )(x, mod_all, w_in_all, dft_c, q_gain, kv_gain, wq_all, wkv_all, cos_t, sin_t)


def _four_kernel(z_ref, cl_ref, sl_ref, cc_ref, sc_ref, y_ref, *, nlt, seq):
    j = pl.program_id(1)

    def emit(cm, sm, z):
        y = _dot(cm, z[:, 0:FOURIER_W]) - _dot(sm, z[:, FOURIER_W:])
        y_ref[...] = y.astype(BF16)

    @pl.when(j < nlt)
    def _():
        emit(cl_ref[...], sl_ref[...], z_ref[0:seq, :])

    @pl.when(j >= nlt)
    def _():
        emit(cc_ref[...], sc_ref[...], z_ref[seq:, :])


def _fourier(z, dft_lat, dft_ctx, *, nb, tpb, nlt, seq, ctx_len):
    n = z.shape[0]
    r = seq + ctx_len
    cl, sl = dft_lat
    cc, sc = dft_ctx
    kern = functools.partial(_four_kernel, nlt=nlt, seq=seq)
    lmap = lambda b, j: (jnp.minimum(j, nlt - 1), 0)
    cmap = lambda b, j: (jnp.maximum(j - nlt, 0), 0)
    return pl.pallas_call(
        kern,
        grid=(nb, tpb),
        in_specs=[
            pl.BlockSpec((None, r, 2 * FOURIER_W), lambda b, j: (b, 0, 0)),
            pl.BlockSpec((TM, seq), lmap),
            pl.BlockSpec((TM, seq), lmap),
            pl.BlockSpec((TM, ctx_len), cmap),
            pl.BlockSpec((TM, ctx_len), cmap),
        ],
        out_specs=pl.BlockSpec((TM, FOURIER_W), lambda b, j: (b * tpb + j, 0)),
        out_shape=jax.ShapeDtypeStruct((n, FOURIER_W), BF16),
        compiler_params=_cparams(("arbitrary", "arbitrary")),
        name="fourier",
    )(z.reshape(nb, r, 2 * FOURIER_W), cl, sl, cc, sc)


def _level_matrices(reverse):
    p = np.arange(TM)
    mats = [(p[None, :] <= p[:, None])]
    for lev in range(1, N_LEVELS):
        r = TM >> lev
        same = (p[None, :] // r) == (p[:, None] // r)
        is_q = ((p // r) % 2 == 1)[:, None]
        mats.append(same & np.where(is_q, p[None, :] <= p[:, None], p[None, :] > p[:, None]))
    g = np.stack(mats).astype(np.float32)
    if reverse:
        g = g[:, ::-1, ::-1]
    return jnp.asarray(g.reshape(N_LEVELS * TM, TM), dtype=BF16)


def _hgrn_kernel(qf_ref, vf_ref, zf_ref, lbf_ref, gf_ref, qb_ref, vb_ref, zb_ref, lbb_ref, gb_ref,
                 of_ref, ob_ref, stf_ref, stb_ref):
    @pl.when(pl.program_id(1) == 0)
    def _():
        stf_ref[...] = jnp.zeros_like(stf_ref)
        stb_ref[...] = jnp.zeros_like(stb_ref)

    gates_f = _hgrn_gates(zf_ref, lbf_ref, gf_ref)
    gates_b = _hgrn_gates(zb_ref, lbb_ref, gb_ref)
    _hgrn_chunk(qf_ref, vf_ref, gates_f, of_ref, stf_ref, reverse=False)
    _hgrn_chunk(qb_ref, vb_ref, gates_b, ob_ref, stb_ref, reverse=True)


def _hgrn_gates(z_ref, lb_ref, g_ref):
    z = z_ref[...]
    log_lb = lb_ref[0:1, :]
    log_1m_lb = lb_ref[1:2, :]
    one_m_lb = lb_ref[2:3, :]
    log_sig = jnp.minimum(z, 0.0) - jnp.log1p(jnp.exp(-jnp.abs(z)))
    a2 = log_1m_lb + log_sig
    log_f = jnp.maximum(log_lb, a2) + jnp.log1p(jnp.exp(-jnp.abs(log_lb - a2)))
    kk = one_m_lb * _sigmoid(-z)

    return kk, _split_dot_left(g_ref[...], log_f * LOG2_E)


def _hgrn_chunk(q_ref, v_ref, gates, o_ref, st_ref, *, reverse):
    kk, wall = gates
    a = wall[0:TM]
    a_end = a[0:1] if reverse else a[TM - 1:TM]
    q = q_ref[...].astype(F32)
    v = v_ref[...]
    qa = (q * jnp.exp2(a)).astype(BF16)
    kd = (kk * jnp.exp2(a_end - a)).astype(BF16)
    carry = jnp.exp2(a_end)

    half = TM // 2
    tiles = (slice(0, half), slice(half, TM))
    q_tile, k_tile = (0, 1) if reverse else (1, 0)
    row = lax.broadcasted_iota(I32, (half, half), 0)
    col = lax.broadcasted_iota(I32, (half, half), 1)
    diff = row ^ col
    pos = lax.broadcasted_iota(I32, (TM, HGRN_W), 0)
    if reverse:
        pos = TM - 1 - pos

    def is_q_piece(idx):
        return idx % 2 == (0 if reverse else 1)

    fac, fac_q, fac_k = {}, {}, {}
    for lev in range(1, N_LEVELS):
        r = TM >> lev
        fac[lev] = jnp.exp2(wall[lev * TM:(lev + 1) * TM])
        if r < SUBLANES:
            is_q = ((pos >> (r.bit_length() - 1)) & 1) == 1
            fac_q[lev] = jnp.where(is_q, fac[lev], 0.0)
            fac_k[lev] = jnp.where(is_q, 0.0, fac[lev])

    def level_operands(lev, qh, kh, sl, rows):
        r = TM >> lev
        if r < SUBLANES:
            return (qh[rows] * fac_q[lev][rows, sl]).astype(BF16), (kh[rows] * fac_k[lev][rows, sl]).astype(BF16)
        qs, ks = [], []
        for start in range(rows.start, rows.stop, r):
            piece = slice(start, start + r)
            zero = jnp.zeros((r, HGRN_DIM), F32)
            if is_q_piece(start // r):
                qs.append(qh[piece] * fac[lev][piece, sl])
                ks.append(zero)
            else:
                qs.append(zero)
                ks.append(kh[piece] * fac[lev][piece, sl])
        cat = lambda xs: (xs[0] if len(xs) == 1 else jnp.concatenate(xs, axis=0)).astype(BF16)
        return cat(qs), cat(ks)

    heads = [slice(h * HGRN_DIM, (h + 1) * HGRN_DIM) for h in range(HGRN_HEADS)]
    p = {}
    for h, sl in enumerate(heads):
        for d, rows in enumerate(tiles):
            p[h, d] = jnp.where(diff == 0, _dot_nt(q[rows, sl].astype(BF16), kk[rows, sl].astype(BF16)), 0.0)
    for lev in range(2, N_LEVELS):
        block = 2 * (TM >> lev)
        for h, sl in enumerate(heads):
            for d, rows in enumerate(tiles):
                qt, kt = level_operands(lev, q[:, sl], kk[:, sl], sl, rows)
                sc = _dot_nt(qt, kt)
                p[h, d] = p[h, d] + (sc if block == half else jnp.where((diff >> (block.bit_length() - 1)) == 0, sc, 0.0))

    for h, sl in enumerate(heads):
        st = st_ref[h]
        qh, kh, vh = q[:, sl], kk[:, sl], v[:, sl]
        p_off = _dot_nt((qh[tiles[q_tile]] * fac[1][tiles[q_tile], sl]).astype(BF16),
                        (kh[tiles[k_tile]] * fac[1][tiles[k_tile], sl]).astype(BF16))
        inter = _dot_nt(qa[:, sl], st.astype(BF16))
        for d, rows in enumerate(tiles):
            o = inter[rows] + _dot(p[h, d].astype(BF16), vh[rows])
            if d == q_tile:
                o = o + _dot(p_off.astype(BF16), vh[tiles[k_tile]])
            o_ref[rows, sl] = o
        st_ref[h] = st * carry[:, sl] + _dot_tn(vh, kd[:, sl])


def _split_dot_left(m, x):
    hi = x.astype(BF16)
    lo = (x - hi.astype(F32)).astype(BF16)
    return _dot(m, hi) + _dot(m, lo)


def _hgrn(hg, hz, lb_fwd, lb_bwd, *, nb, tpb, nlt):
    n = hg.shape[0]
    nct = tpb - nlt

    def chunk(reverse):
        def index(b, s):
            if reverse:
                j = jnp.where(s < nct, tpb - 1 - s, nlt - 1 - (s - nct))
            else:
                j = jnp.where(s < nct, nlt + s, s - nct)
            return b * tpb + j
        return index

    def direction_specs(reverse, z_block, gmat):
        c = chunk(reverse)
        return [
            pl.BlockSpec((TM, HGRN_W), lambda b, s: (c(b, s), 0)),
            pl.BlockSpec((TM, HGRN_W), lambda b, s: (c(b, s), 1)),
            pl.BlockSpec((TM, HGRN_W), lambda b, s: (c(b, s), z_block)),
            pl.BlockSpec((8, HGRN_W), lambda b, s: (0, 0)),
            pl.BlockSpec(gmat.shape, lambda b, s: (0, 0)),
        ]

    g_fwd, g_bwd = _level_matrices(False), _level_matrices(True)
    cf, cb = chunk(False), chunk(True)
    state = pltpu.VMEM((HGRN_HEADS, HGRN_DIM, HGRN_DIM), F32)
    return pl.pallas_call(
        _hgrn_kernel,
        grid=(nb, tpb),
        in_specs=direction_specs(False, 0, g_fwd) + direction_specs(True, 1, g_bwd),
        out_specs=[
            pl.BlockSpec((TM, HGRN_W), lambda b, s: (cf(b, s), 0)),
            pl.BlockSpec((TM, HGRN_W), lambda b, s: (cb(b, s), 0)),
        ],
        out_shape=[jax.ShapeDtypeStruct((n, HGRN_W), F32)] * 2,
        scratch_shapes=[state, state],
        compiler_params=_cparams(("arbitrary", "arbitrary")),
        name="hgrn",
    )(hg, hg, hz, lb_fwd, g_fwd, hg, hg, hz, lb_bwd, g_bwd)


def _mla_heads(m, q_gain, kv_gain, wq_ref, wkv_ref, cos, sin, q_ref, k_ref, v_ref):
    cq = m[:, 0:Q_LORA]
    ckv = m[:, Q_LORA:Q_LORA + KV_LORA]
    kr = m[:, Q_LORA + KV_LORA:Q_LORA + KV_LORA + LANES]
    kr_sw = m[:, Q_LORA + KV_LORA + LANES:MLA_IN_W]
    qall = _dot((_rms(cq) * q_gain).astype(BF16), wq_ref[...]) * (MLA_SCALE * LOG2_E)
    kv = _dot((_rms(ckv) * kv_gain).astype(BF16), wkv_ref[...])
    k_rot = (kr * cos + kr_sw * sin).astype(BF16)
    ones = jnp.ones((TM, LANES), BF16)
    nw = MLA_HEADS * LANES
    for h in range(MLA_HEADS):
        a, b = h * LANES, (h + 1) * LANES
        s = h * HEAD_SLOT
        q_ref[:, s:s + LANES] = qall[:, a:b].astype(BF16)
        q_ref[:, s + LANES:s + HEAD_SLOT] = (qall[:, nw + a:nw + b] * cos + qall[:, 2 * nw + a:2 * nw + b] * sin).astype(BF16)
        k_ref[:, s:s + LANES] = kv[:, a:b].astype(BF16)
        k_ref[:, s + LANES:s + HEAD_SLOT] = k_rot
        v_ref[:, s:s + LANES] = kv[:, nw + a:nw + b].astype(BF16)
        v_ref[:, s + LANES:s + HEAD_SLOT] = ones


def _attn_kernel(q_ref, k_ref, v_ref, o_ref, s_ref, *, n_lat_steps, seq, ctx_len):
    j = pl.program_id(2)

    def run(rows, k0, k1):
        blocks = [(kb, slice(k0 + kb * KEY_BLOCK, k0 + (kb + 1) * KEY_BLOCK), slice(kb * KEY_BLOCK, (kb + 1) * KEY_BLOCK))
                  for kb in range((k1 - k0) // KEY_BLOCK)]
        hslot = lambda hh: slice(hh * HEAD_SLOT, (hh + 1) * HEAD_SLOT)

        def scores(hh, blk, m_run):
            _, ks, ss = blk
            s = _dot_nt(q_ref[0:rows, hslot(hh)], k_ref[ks, hslot(hh)])
            s_ref[hh % 2, 0:rows, ss] = s
            for c in range(KEY_BLOCK // LANES):
                m_run = jnp.maximum(m_run, s[:, c * LANES:(c + 1) * LANES])
            return m_run

        def values(hh, blk, m, o):
            _, ks, ss = blk
            d = (s_ref[hh % 2, 0:rows, ss] - m).astype(BF16)
            return o + _dot(jnp.exp2(d), v_ref[ks, hslot(hh)])

        def emit(hh, o):
            o_ref[0:rows, hh * MLA_V:(hh + 1) * MLA_V] = (o[:, 0:MLA_V] / o[:, MLA_V:]).astype(BF16)

        neg = jnp.full((rows, LANES), -jnp.inf, F32)
        zero = jnp.zeros((rows, HEAD_SLOT), F32)
        m_run = neg
        for blk in blocks:
            m_run = scores(0, blk, m_run)
        m_prev = jnp.max(m_run, axis=-1, keepdims=True)
        for hh in range(1, HEADS_PER_STEP):
            m_run, o = neg, zero
            for blk in blocks:
                m_run = scores(hh, blk, m_run)
                o = values(hh - 1, blk, m_prev, o)
            emit(hh - 1, o)
            m_prev = jnp.max(m_run, axis=-1, keepdims=True)
        o = zero
        for blk in blocks:
            o = values(HEADS_PER_STEP - 1, blk, m_prev, o)
        emit(HEADS_PER_STEP - 1, o)

    @pl.when(j < n_lat_steps)
    def _():
        run(TQ, 0, seq + ctx_len)

    @pl.when(j >= n_lat_steps)
    def _():
        run(ctx_len, seq, seq + ctx_len)


def _attention(q, k, v, *, nb, seq, ctx_len):
    n = q.shape[0]
    r = seq + ctx_len
    assert seq % TQ == 0 and ctx_len <= TQ
    n_lat_steps = seq // TQ
    hp = MLA_HEADS // HEADS_PER_STEP
    wq = HEADS_PER_STEP * HEAD_SLOT
    wo = HEADS_PER_STEP * MLA_V
    kern = functools.partial(_attn_kernel, n_lat_steps=n_lat_steps, seq=seq, ctx_len=ctx_len)
    out = pl.pallas_call(
        kern,
        grid=(nb, hp, n_lat_steps + 1),
        in_specs=[
            pl.BlockSpec((None, TQ, wq), lambda b, h, j: (b, j, h)),
            pl.BlockSpec((None, r, wq), lambda b, h, j: (b, 0, h)),
            pl.BlockSpec((None, r, wq), lambda b, h, j: (b, 0, h), pipeline_mode=pl.Buffered(1)),
        ],
        out_specs=pl.BlockSpec((None, TQ, wo), lambda b, h, j: (b, j, h)),
        out_shape=jax.ShapeDtypeStruct((nb, r, MLA_HEADS * MLA_V), BF16),
        scratch_shapes=[pltpu.VMEM((2, TQ, r), F32)],
        compiler_params=_cparams(("arbitrary", "arbitrary", "arbitrary")),
        name="attention",
    )(q.reshape(nb, r, -1), k.reshape(nb, r, -1), v.reshape(nb, r, -1))
    return out.reshape(n, MLA_HEADS * MLA_V)


def _pack_rows(x):
    w = x.shape[1] // 2
    lo = pltpu.bitcast(x[:, :w].astype(BF16).astype(F32), U32)
    hi = pltpu.bitcast(x[:, w:].astype(BF16).astype(F32), U32)
    return (lo >> 16) | (hi & jnp.uint32(0xFFFF0000))


def _store_row_tiles(ref, packed):
    rows = packed.shape[0]
    for s in range(ROW_TILE):
        ref[pl.ds(s, rows, stride=ROW_TILE), :] = packed[:, s * LANES:(s + 1) * LANES]


def _load_row_tiles(ref, rows):
    return jnp.concatenate([ref[pl.ds(s, rows, stride=ROW_TILE), :] for s in range(ROW_TILE)], axis=1)


def _unpack_rows(w):
    lo = pltpu.bitcast(w << 16, F32)
    hi = pltpu.bitcast(w & jnp.uint32(0xFFFF0000), F32)
    return lo, hi


def _post_kernel(yf_ref, of_ref, ob_ref, g_ref, att_ref, x_ref, mod_ref, hn_ref, wo_ref, lng_ref, lnb_ref,
                 rw_ref, rb_ref, x1_ref, h2_ref, eid_ref, gate_ref, base_ref, cnt_ref, *, tpb, nlt, nb, alpha):
    row = _mod_row(pl.program_id(0), tpb, nlt, nb)
    o = of_ref[...] + ob_ref[...]
    hn = hn_ref[...]
    gate_act = g_ref[...].astype(F32)
    gate_act = gate_act * _sigmoid(gate_act)
    mix = _dot(yf_ref[...], wo_ref[0:FOURIER_W, :])
    heads = [slice(h * HGRN_DIM, (h + 1) * HGRN_DIM) for h in range(HGRN_HEADS)]
    readout = jnp.concatenate([(_rms(o[:, sl]) * hn[:, sl] * gate_act[:, sl]).astype(BF16) for sl in heads], axis=1)
    mix = mix + _dot(readout, wo_ref[FOURIER_W:FOURIER_W + HGRN_W, :])
    mix = mix + _dot(att_ref[...], wo_ref[FOURIER_W + HGRN_W:, :])

    x1 = _layer_norm(alpha * x_ref[...] + _mod(mod_ref, row, 2) * mix) * lng_ref[...] + lnb_ref[...]
    x1_ref[...] = x1
    h2 = _layer_norm(x1) * (1.0 + _mod(mod_ref, row, 4)) + _mod(mod_ref, row, 3)
    _store_row_tiles(h2_ref, _pack_rows(h2))

    hi = h2.astype(BF16)
    lo = (h2 - hi.astype(F32)).astype(BF16)
    rr = _dot(jnp.concatenate([hi, lo], axis=0), rw_ref[...])
    logits = rr[0:TM, 0:LANES] + rr[0:TM, LANES:] + rr[TM:, 0:LANES] + rr[TM:, LANES:] + rb_ref[...]
    lane = lax.broadcasted_iota(I32, logits.shape, 1)
    cur = logits
    vals, ids = [], []
    for _ in range(TOP_K):
        m = jnp.max(cur, axis=-1, keepdims=True)
        idx = jnp.min(jnp.where(cur == m, lane, LANES), axis=-1, keepdims=True)
        vals.append(m)
        ids.append(idx)
        cur = jnp.where(lane == idx, -3e38, cur)
    ex = [jnp.exp(vk - vals[0]) for vk in vals]
    den = ex[0] + ex[1] + ex[2] + ex[3]
    eid = jnp.zeros(logits.shape, I32)
    gate = jnp.zeros(logits.shape, F32)
    for k in range(TOP_K):
        eid = jnp.where(lane == k, ids[k], eid)
        gate = jnp.where(lane == k, ex[k] / den, gate)
    eid_ref[...] = eid
    gate_ref[...] = gate

    i = pl.program_id(0)

    @pl.when(i == 0)
    def _():
        base_ref[...] = jnp.zeros_like(base_ref)
        cnt_ref[...] = jnp.zeros_like(cnt_ref)

    base_ref[pl.ds(i, 1), :] = cnt_ref[0:1, :]
    tile_cnt = jnp.zeros((1, LANES), F32)
    for k in range(TOP_K):
        tile_cnt = tile_cnt + jnp.sum((lane == ids[k]).astype(F32), axis=0, keepdims=True)
    cnt_ref[0:1, :] = cnt_ref[0:1, :] + tile_cnt


def _post(yf, o_f, o_b, hg, att, x, mod_all, hnorm, w_out_all, ln_g, ln_b, rw_all, rb, layer, *, tpb, nlt, nb, alpha):
    n = x.shape[0]
    nt_pad = -(-(n // TM) // SUBLANES) * SUBLANES
    kern = functools.partial(_post_kernel, tpb=tpb, nlt=nlt, nb=nb, alpha=alpha)
    full = lambda a: pl.BlockSpec(a.shape, lambda i: (0, 0))
    return pl.pallas_call(
        kern,
        grid=(n // TM,),
        in_specs=[
            pl.BlockSpec((TM, FOURIER_W), lambda i: (i, 0)),
            pl.BlockSpec((TM, HGRN_W), lambda i: (i, 0)),
            pl.BlockSpec((TM, HGRN_W), lambda i: (i, 0)),
            pl.BlockSpec((TM, HGRN_W), lambda i: (i, 2)),
            pl.BlockSpec((TM, MLA_HEADS * MLA_V), lambda i: (i, 0)),
            pl.BlockSpec((TM, D_MODEL), lambda i: (i, 0)),
            _layer_block(mod_all, layer), full(hnorm),
            _layer_block(w_out_all, layer, pipeline_mode=pl.Buffered(1)),
            full(ln_g), full(ln_b), _layer_block(rw_all, layer), full(rb),
        ],
        out_specs=[
            pl.BlockSpec((TM, D_MODEL), lambda i: (i, 0)),
            pl.BlockSpec((TM * ROW_TILE, LANES), lambda i: (i, 0)),
            pl.BlockSpec((TM, LANES), lambda i: (i, 0)),
            pl.BlockSpec((TM, LANES), lambda i: (i, 0)),
            pl.BlockSpec((nt_pad, LANES), lambda i: (0, 0)),
            pl.BlockSpec((SUBLANES, LANES), lambda i: (0, 0)),
        ],
        out_shape=[
            jax.ShapeDtypeStruct((n, D_MODEL), F32),
            jax.ShapeDtypeStruct((n * ROW_TILE, LANES), U32),
            jax.ShapeDtypeStruct((n, LANES), I32),
            jax.ShapeDtypeStruct((n, LANES), F32),
            jax.ShapeDtypeStruct((nt_pad, LANES), F32),
            jax.ShapeDtypeStruct((SUBLANES, LANES), F32),
        ],
        compiler_params=_cparams(("arbitrary",)),
        name="post",
    )(yf, o_f, o_b, hg, att, x, mod_all, hnorm, w_out_all, ln_g, ln_b, rw_all, rb)


def _route_kernel(eid_ref, base_ref, cnt_ref, pos_ref, te_ref, meta_ref, *, n_tiles_pad):
    i = pl.program_id(0)
    lane = lax.broadcasted_iota(I32, (TM, LANES), 1)
    eid = eid_ref[...]
    onehot = [(lane == eid[:, k:k + 1]).astype(F32) for k in range(TOP_K)]
    colsum = [jnp.sum(oh, axis=0, keepdims=True) for oh in onehot]

    cnt = cnt_ref[0:1, :]
    tiles = jnp.floor((cnt + (TMM - 1)) * (1.0 / TMM))
    r128 = lax.broadcasted_iota(I32, (LANES, LANES), 0)
    c128 = lax.broadcasted_iota(I32, (LANES, LANES), 1)
    incl = (r128 <= c128).astype(BF16)
    pad_end = _split_dot(jnp.broadcast_to(tiles, (8, LANES)), incl)[0:1, :] * TMM
    pad_start = pad_end - tiles * TMM

    rr = lax.broadcasted_iota(I32, (TM, TM), 0)
    cc = lax.broadcasted_iota(I32, (TM, TM), 1)
    strict = (cc < rr).astype(BF16)
    run = pad_start + base_ref[pl.ds(i, 1), :]
    pos = jnp.zeros((TM, LANES), I32)
    for k in range(TOP_K):
        before = _dot(strict, onehot[k].astype(BF16))
        val = jnp.sum(onehot[k] * (before + run), axis=1, keepdims=True)
        pos = jnp.where(lane == k, val.astype(I32), pos)
        run = run + colsum[k]
    pos_ref[...] = pos

    @pl.when(i == 0)
    def _():
        tstart = (lax.broadcasted_iota(I32, (n_tiles_pad, LANES), 0) * TMM).astype(F32)
        lane_t = lax.broadcasted_iota(I32, (n_tiles_pad, LANES), 1)
        done = jnp.where((pad_end <= tstart) & (lane_t < N_EXPERTS), 1.0, 0.0)
        te = jnp.minimum(jnp.sum(done, axis=1, keepdims=True), N_EXPERTS - 1.0)
        te_ref[...] = jnp.broadcast_to(te, (n_tiles_pad, LANES)).astype(I32)
        sub = lax.broadcasted_iota(I32, (8, LANES), 0)
        meta = jnp.where(sub == 0, pad_start + cnt, jnp.where(sub == 1, pad_end - pad_start - cnt, pad_end))
        meta_ref[...] = meta.astype(I32)


def _route(eid, base, cnt, n_tiles):
    n = eid.shape[0]
    n_tiles_pad = -(-n_tiles // 8) * 8
    kern = functools.partial(_route_kernel, n_tiles_pad=n_tiles_pad)
    full = lambda a: pl.BlockSpec(a.shape, lambda i: (0, 0))
    pos, te, meta = pl.pallas_call(
        kern,
        grid=(n // TM,),
        in_specs=[pl.BlockSpec((TM, LANES), lambda i: (i, 0)), full(base), full(cnt)],
        out_specs=[
            pl.BlockSpec((TM, LANES), lambda i: (i, 0)),
            pl.BlockSpec((n_tiles_pad, LANES), lambda i: (0, 0)),
            pl.BlockSpec((8, LANES), lambda i: (0, 0)),
        ],
        out_shape=[
            jax.ShapeDtypeStruct((n, LANES), I32),
            jax.ShapeDtypeStruct((n_tiles_pad, LANES), I32),
            jax.ShapeDtypeStruct((8, LANES), I32),
        ],
        compiler_params=_cparams(("arbitrary",)),
        name="moe_route",
    )(eid, base, cnt)
    pos_flat = pos[:, :TOP_K].reshape(-1)
    tile_e = te[:n_tiles, 0]
    fill_start = meta[0, :N_EXPERTS]
    fill_n = meta[1, :N_EXPERTS]
    n_used = meta[2, N_EXPERTS - 1:N_EXPERTS] // TMM
    return pos_flat, tile_e, n_used, fill_start, fill_n


def _dispatch_kernel(pos_ref, fs_ref, fn_ref, nu_ref, h_ref, o_hbm, zero_ref, sem, fill_sem, *, n_tiles):
    i = pl.program_id(0)

    def rows(ref, first, count):
        return ref.at[pl.ds(pl.multiple_of(first * ROW_TILE, ROW_TILE), count * ROW_TILE), :]

    def start(r, c):
        for k in range(TOP_K):
            p = pos_ref[(i * TMM + r) * TOP_K + k]
            pltpu.make_async_copy(rows(h_ref, r, 1), rows(o_hbm, p, 1), sem).start()
        return c

    lax.fori_loop(0, TMM, start, 0, unroll=ISSUE_UNROLL)

    @pl.when(i == 0)
    def _():
        zero_ref[...] = jnp.zeros_like(zero_ref)

        def fill(first, count):
            return pltpu.make_async_copy(rows(zero_ref, 0, count), rows(o_hbm, first, count), fill_sem)

        def each(fn):
            for e in range(N_EXPERTS):
                first, count = fs_ref[e], fn_ref[e]
                piece = TMM // 2
                while piece:
                    @pl.when((count & piece) != 0)
                    def _(first=first, piece=piece):
                        fn(fill(first, piece))

                    first = first + (count & piece)
                    piece //= 2
            lax.fori_loop(nu_ref[0], n_tiles, lambda t, c: (fn(fill(t * TMM, TMM)), c)[1], 0)

        each(lambda cp: cp.start())
        each(lambda cp: cp.wait())

    for _ in range(TOP_K):
        pltpu.make_async_copy(h_ref, rows(o_hbm, 0, TMM), sem).wait()


def _dispatch(pos_flat, fill_start, fill_n, n_used, h2p, n_tiles):
    n = h2p.shape[0] // ROW_TILE
    assert n % TMM == 0
    return pl.pallas_call(
        functools.partial(_dispatch_kernel, n_tiles=n_tiles),
        grid_spec=pltpu.PrefetchScalarGridSpec(
            num_scalar_prefetch=4,
            grid=(n // TMM,),
            in_specs=[pl.BlockSpec((TMM * ROW_TILE, LANES), lambda i, p, fs, fn, nu: (i, 0))],
            out_specs=pl.BlockSpec(memory_space=pl.ANY),
            scratch_shapes=[pltpu.VMEM((TMM * ROW_TILE, LANES), U32), pltpu.SemaphoreType.DMA(()),
                            pltpu.SemaphoreType.DMA(())],
        ),
        out_shape=jax.ShapeDtypeStruct((n_tiles * TMM * ROW_TILE, LANES), U32),
        compiler_params=_cparams(("arbitrary",)),
        name="moe_dispatch",
    )(pos_flat, fill_start, fill_n, n_used, h2p)


def _w1_kernel(w_ref, g_ref, l_ref, t_ref):
    for c in range(D_MODEL // LANES):
        cs = slice(c * LANES, (c + 1) * LANES)
        t_ref[c] = w_ref[cs, :].T
        g_ref[:, cs] = t_ref[c, pl.ds(0, W1_TN // 2, stride=2), :].astype(BF16)
        l_ref[:, cs] = t_ref[c, pl.ds(1, W1_TN // 2, stride=2), :].astype(BF16)


def _prep_w1(w1):
    depth, ne, d, n2 = w1.shape
    out = jax.ShapeDtypeStruct((depth * ne, n2 // 2, d), BF16)
    g, l = pl.pallas_call(
        _w1_kernel,
        grid=(depth * ne, n2 // W1_TN),
        in_specs=[pl.BlockSpec((None, d, W1_TN), lambda e, j: (e, 0, j))],
        out_specs=[pl.BlockSpec((None, W1_TN // 2, d), lambda e, j: (e, j, 0))] * 2,
        out_shape=[out, out],
        scratch_shapes=[pltpu.VMEM((d // LANES, W1_TN, LANES), F32)],
        compiler_params=_cparams(("arbitrary", "arbitrary")),
        name="moe_w1_prep",
    )(w1.reshape(depth * ne, d, n2))
    return g, l


def _moe_kernel(te_ref, nu_ref, x_ref, w1g_ref, w1l_ref, b1g_ref, b1l_ref, w2_ref, b2_ref, y_ref, w2b_ref):
    i = pl.program_id(0)

    @pl.when(i < nu_ref[0])
    def _():
        @pl.when((i == 0) | (te_ref[i] != te_ref[jnp.maximum(i - 1, 0)]))
        def _():
            w2b_ref[...] = w2_ref[...].astype(BF16)

        lo, hi = _unpack_rows(_load_row_tiles(x_ref, TMM))
        lo = lo.astype(BF16)
        hi = hi.astype(BF16)
        ug = _dot_nt(lo, w1g_ref[:, 0:HALF]) + _dot_nt(hi, w1g_ref[:, HALF:]) + b1g_ref[...]
        ul = _dot_nt(lo, w1l_ref[:, 0:HALF]) + _dot_nt(hi, w1l_ref[:, HALF:]) + b1l_ref[...]
        xg = jnp.minimum(ug, SWIGLU_LIMIT)
        xl = jnp.clip(ul, -SWIGLU_LIMIT, SWIGLU_LIMIT)
        act = xg * _sigmoid(SWIGLU_ALPHA * xg) * (xl + 1.0)
        y = _dot(act.astype(BF16), w2b_ref[...]) + b2_ref[...]
        _store_row_tiles(y_ref, _pack_rows(y))

    @pl.when(i >= nu_ref[0])
    def _():
        y_ref[...] = jnp.zeros_like(y_ref)


def _moe_ffn(tile_e, n_used, xs, w1g, w1l, b1g, b1l, w2, b2, layer):
    m = xs.shape[0] // ROW_TILE
    emap = lambda i, te, nu: (layer * N_EXPERTS + te[i], 0, 0)
    row_tiles = pl.BlockSpec((TMM * ROW_TILE, LANES), lambda i, te, nu: (i, 0))
    return pl.pallas_call(
        _moe_kernel,
        grid_spec=pltpu.PrefetchScalarGridSpec(
            num_scalar_prefetch=2,
            grid=(m // TMM,),
            in_specs=[
                row_tiles,
                pl.BlockSpec((None, EXPERT_FF, D_MODEL), emap),
                pl.BlockSpec((None, EXPERT_FF, D_MODEL), emap),
                pl.BlockSpec((None, 1, EXPERT_FF), emap),
                pl.BlockSpec((None, 1, EXPERT_FF), emap),
                pl.BlockSpec((None, EXPERT_FF, D_MODEL), emap),
                pl.BlockSpec((None, 1, D_MODEL), emap),
            ],
            out_specs=row_tiles,
            scratch_shapes=[pltpu.VMEM((EXPERT_FF, D_MODEL), BF16)],
        ),
        out_shape=jax.ShapeDtypeStruct((m * ROW_TILE, LANES), U32),
        compiler_params=_cparams(("arbitrary",)),
        name="moe_ffn",
    )(tile_e, n_used, xs, w1g, w1l, b1g, b1l, w2, b2)


def _combine_kernel(pos_ref, y_hbm, gate_ref, x_ref, mod_ref, lng_ref, lnb_ref, o_ref, buf, sem, *, tpb, nlt, nb, alpha,
                    latent_only):
    i = pl.program_id(0)
    n_steps = pl.num_programs(0)
    row = _mod_row(i, tpb, nlt, nb)

    def issue(tile, slot):
        def start(r, c):
            for k in range(TOP_K):
                p = pos_ref[(tile * TM + r) * TOP_K + k]
                src = y_hbm.at[pl.ds(pl.multiple_of(p * ROW_TILE, ROW_TILE), ROW_TILE), :]
                dst = buf.at[slot, k, pl.ds(pl.multiple_of(r * ROW_TILE, ROW_TILE), ROW_TILE), :]
                pltpu.make_async_copy(src, dst, sem.at[slot]).start()
            return c

        lax.fori_loop(0, TM, start, 0, unroll=ISSUE_UNROLL)

    @pl.when(i == 0)
    def _():
        issue(0, 0)

    @pl.when(i + 1 < n_steps)
    def _():
        issue(i + 1, (i + 1) % 2)

    slot = i % 2
    for k in range(TOP_K):
        pltpu.make_async_copy(y_hbm.at[pl.ds(0, TM * ROW_TILE), :], buf.at[slot, k], sem.at[slot]).wait()

    gates = gate_ref[...]
    acc_lo = jnp.zeros((TM, HALF), F32)
    acc_hi = jnp.zeros((TM, HALF), F32)
    for k in range(TOP_K):
        lo, hi = _unpack_rows(_load_row_tiles(buf.at[slot, k], TM))
        gk = gates[:, k:k + 1]
        acc_lo = acc_lo + gk * lo
        acc_hi = acc_hi + gk * hi
    ffn = jnp.concatenate([acc_lo, acc_hi], axis=-1)
    x2 = _layer_norm(alpha * x_ref[...] + _mod(mod_ref, row, 5) * ffn) * lng_ref[...] + lnb_ref[...]
    if latent_only:
        @pl.when(i % tpb < nlt)
        def _():
            o_ref[...] = x2
    else:
        o_ref[...] = x2


def _combine(pos, ys, gates, x, mod_all, ln_g, ln_b, layer, *, latent_only, tpb, nlt, nb, alpha):
    n = x.shape[0]
    kern = functools.partial(_combine_kernel, tpb=tpb, nlt=nlt, nb=nb, alpha=alpha, latent_only=latent_only)
    full = lambda a: pl.BlockSpec(a.shape, lambda i, p: (0, 0))
    if latent_only:
        omap = lambda i, p: ((i // tpb) * nlt + jnp.minimum(i % tpb, nlt - 1), 0)
        n_out = nb * nlt * TM
    else:
        omap = lambda i, p: (i, 0)
        n_out = n
    return pl.pallas_call(
        kern,
        grid_spec=pltpu.PrefetchScalarGridSpec(
            num_scalar_prefetch=1,
            grid=(n // TM,),
            in_specs=[
                pl.BlockSpec(memory_space=pl.ANY),
                pl.BlockSpec((TM, LANES), lambda i, p: (i, 0)),
                pl.BlockSpec((TM, D_MODEL), lambda i, p: (i, 0)),
                _layer_block(mod_all, layer), full(ln_g), full(ln_b),
            ],
            out_specs=pl.BlockSpec((TM, D_MODEL), omap),
            scratch_shapes=[pltpu.VMEM((2, TOP_K, TM * ROW_TILE, LANES), U32), pltpu.SemaphoreType.DMA((2,))],
        ),
        out_shape=jax.ShapeDtypeStruct((n_out, D_MODEL), F32),
        compiler_params=_cparams(("arbitrary",)),
        name="moe_combine",
    )(pos, ys, gates, x, mod_all, ln_g, ln_b)


def _dft_pair(t_len):
    assert t_len % GRID_W == 0
    k = jnp.arange(t_len, dtype=I32)

    def table(t):
        ang = ((t[:, None] * k[None, :]) % t_len).astype(F32) * (2.0 * np.pi / t_len)
        return jnp.cos(ang), jnp.sin(ang)

    ca, sa = table(jnp.arange(t_len // GRID_W, dtype=I32) * GRID_W)
    cb, sb = table(jnp.arange(GRID_W, dtype=I32))
    s = 1.0 / np.sqrt(t_len)
    cos = ca[:, None, :] * cb[None, :, :] - sa[:, None, :] * sb[None, :, :]
    sin = sa[:, None, :] * cb[None, :, :] + ca[:, None, :] * sb[None, :, :]
    return (cos * s).reshape(t_len, t_len).astype(BF16), (sin * s).reshape(t_len, t_len).astype(BF16)


def _channel_dft():
    c = np.arange(FOURIER_GROUP)
    ang = 2.0 * np.pi * ((c[:, None] * c[None, :]) % FOURIER_GROUP) / FOURIER_GROUP
    s = 1.0 / np.sqrt(FOURIER_GROUP)
    groups = FOURIER_W // FOURIER_GROUP
    m = np.zeros((FOURIER_W, 2 * FOURIER_W), np.float32)
    for g in range(groups):
        a, b = g * FOURIER_GROUP, (g + 1) * FOURIER_GROUP
        m[a:b, a:b] = np.cos(ang) * s
        m[a:b, FOURIER_W + a:FOURIER_W + b] = np.sin(ang) * s
    return jnp.asarray(m, dtype=BF16)


def _pair_swap(w):
    pairs = w.reshape(*w.shape[:-1], w.shape[-1] // 2, 2)
    return jnp.stack([-pairs[..., 1], pairs[..., 0]], axis=-1).reshape(w.shape)


def _pad_lanes(w, width):
    return jnp.pad(w, [(0, 0)] * (w.ndim - 1) + [(0, width - w.shape[-1])])


def _rope_tables(nb, seq, ctx_len):
    t = jnp.arange(seq)
    n_freq = MLA_ROPE // 4
    inv = ROPE_BASE ** (-jnp.arange(n_freq, dtype=F32) / n_freq)
    ang = jnp.concatenate([(t // GRID_W)[:, None] * inv, (t % GRID_W)[:, None] * inv], axis=-1)
    ang = jnp.repeat(ang, 2, axis=-1)
    cos = jnp.concatenate([jnp.cos(ang), jnp.ones((ctx_len, MLA_ROPE), F32)], axis=0)
    sin = jnp.concatenate([jnp.sin(ang), jnp.zeros((ctx_len, MLA_ROPE), F32)], axis=0)
    cos = jnp.tile(_pad_lanes(cos, LANES), (nb, 1))
    sin = jnp.tile(_pad_lanes(sin, LANES), (nb, 1))
    return cos, sin


def _prep_in_proj(w_in):
    f, q, i, g, zf, zb, cq, ckv, kr = jnp.split(w_in, np.cumsum([512, 512, 512, 512, 512, 512, 512, 256]).tolist(), axis=-1)
    cols = [f, q, i, g, zf, zb, cq, ckv, _pad_lanes(kr, LANES), _pad_lanes(_pair_swap(kr), LANES)]
    return jnp.concatenate(cols, axis=-1).astype(BF16)


def _prep_wq(w_uq):
    w = w_uq.reshape(Q_LORA, MLA_HEADS, MLA_NOPE + MLA_ROPE)
    nope = w[:, :, :MLA_NOPE].reshape(Q_LORA, -1)
    rope = w[:, :, MLA_NOPE:]
    rope_p = _pad_lanes(rope, LANES).reshape(Q_LORA, -1)
    rope_s = _pad_lanes(_pair_swap(rope), LANES).reshape(Q_LORA, -1)
    return jnp.concatenate([nope, rope_p, rope_s], axis=-1).astype(BF16)


def _prep_wkv(w_ukv):
    w = w_ukv.reshape(KV_LORA, MLA_HEADS, MLA_NOPE + MLA_V)
    return jnp.concatenate([w[:, :, :MLA_NOPE].reshape(KV_LORA, -1), w[:, :, MLA_NOPE:].reshape(KV_LORA, -1)], axis=-1).astype(BF16)


def kernel(x, c, ctx, c_ctx, w_ada, b_ada, w_in, mla_q_norm, w_uq, mla_kv_norm, w_ukv, hgrn_lb_logits, hgrn_norm,
           w_out, ln1_g, ln1_b, router_w, router_b, w1, b1, w2, b2, ln2_g, ln2_b):
    nb, seq, d = x.shape
    ctx_len = ctx.shape[1]
    depth = w_ada.shape[0]
    assert d == D_MODEL and seq % TM == 0 and ctx_len % TM == 0 and seq % GRID_W == 0
    r = seq + ctx_len
    tpb, nlt = r // TM, seq // TM
    n = nb * r
    n_tiles = -(-n * TOP_K // TMM) + N_EXPERTS
    alpha = float((2 * depth) ** 0.25)
    geo = dict(tpb=tpb, nlt=nlt, nb=nb)

    xs = jnp.concatenate([x, ctx], axis=1).reshape(n, d)

    mod_rows = -(-(nb + 1) // 8) * 8
    cc = jnp.zeros((mod_rows, d), F32).at[:nb].set(c).at[nb].set(c_ctx)
    mod_all = _ada(cc, w_ada, b_ada)

    lb = jnp.cumsum(jax.nn.softmax(hgrn_lb_logits.astype(F32), axis=0), axis=0)
    lb = lb - lb[:1]
    lb_rows = jnp.stack([jnp.log(lb), jnp.log1p(-lb), 1.0 - lb] + [jnp.zeros_like(lb)] * 5, axis=2)

    dft_c = _channel_dft()
    dft_lat = _dft_pair(seq)
    dft_ctx = _dft_pair(ctx_len)
    cos_t, sin_t = _rope_tables(nb, seq, ctx_len)

    w_in_all = _prep_in_proj(w_in)
    wq_all = jax.vmap(_prep_wq)(w_uq)
    wkv_all = jax.vmap(_prep_wkv)(w_ukv)
    w_out_all = w_out.astype(BF16)
    w1g_all, w1l_all = _prep_w1(w1)
    le = depth * N_EXPERTS
    w2_all = w2.reshape(le, EXPERT_FF, d)
    b1g_all = b1[:, :, 0::2].reshape(le, 1, EXPERT_FF)
    b1l_all = b1[:, :, 1::2].reshape(le, 1, EXPERT_FF)
    b2_all = b2.reshape(le, 1, d)
    rw_f32 = _pad_lanes(router_w, LANES)
    rw_hi = rw_f32.astype(BF16)
    rw_all = jnp.concatenate([rw_hi, (rw_f32 - rw_hi.astype(F32)).astype(BF16)], axis=-1)

    for l in range(depth):
        row = lambda a: a[l].reshape(1, -1)
        z, hg, hz, q, k, v = _pre(xs, mod_all, w_in_all, dft_c, row(mla_q_norm), row(mla_kv_norm), wq_all, wkv_all,
                              cos_t, sin_t, l, **geo)
        yf = _fourier(z, dft_lat, dft_ctx, nb=nb, tpb=tpb, nlt=nlt, seq=seq, ctx_len=ctx_len)
        o_f, o_b = _hgrn(hg, hz, lb_rows[l, 0], lb_rows[l, 1], **geo)
        att = _attention(q, k, v, nb=nb, seq=seq, ctx_len=ctx_len)

        rb = jnp.full((1, LANES), NEG_BIG, F32).at[0, :N_EXPERTS].set(router_b[l])
        x1, h2p, eid, gates, base, cnt = _post(yf, o_f, o_b, hg, att, xs, mod_all, row(hgrn_norm), w_out_all, row(ln1_g), row(ln1_b),
                                    rw_all, rb, l, alpha=alpha, **geo)

        pos, tile_e, n_used, fill_start, fill_n = _route(eid, base, cnt, n_tiles)
        xg = _dispatch(pos, fill_start, fill_n, n_used, h2p, n_tiles)
        ys = _moe_ffn(tile_e, n_used, xg, w1g_all, w1l_all, b1g_all, b1l_all, w2_all, b2_all, l)
        xs = _combine(pos, ys, gates, x1, mod_all, row(ln2_g), row(ln2_b), l, latent_only=l == depth - 1, alpha=alpha, **geo)

    return xs.reshape(nb, seq, d)
```
